```python
import math
import jax
import jax.numpy as jnp
from jax import lax
import numpy as np

D_MODEL = 4096
BATCH = 4
SEQ = 2048
DEPTH = 2
DEC_BATCH = 8
DEC_SEQ = 1
PAST_LEN = 16384
PAGE_SIZE = 128

N_HEADS = 16
N_KV_HEADS = 4
GROUP = N_HEADS // N_KV_HEADS
HEAD_DIM = 128
D_ATTN = N_HEADS * HEAD_DIM
D_KV = N_KV_HEADS * HEAD_DIM
SCALE = HEAD_DIM ** -0.5
BLOCK = 64
N_SELECT = 16
WINDOW = 512
WIN_QBLOCK = 128
SEL_QCHUNK = 16
FORCED_SCORE = 1.0e6
ALIBI_MAX_EXP = 8.0
D_SSM = D_MODEL // 2
GROUP_CH = 16
N_GROUPS = D_SSM // GROUP_CH
P_STATE = 64
DT_MIN = 1.0e-3
DT_MAX = 1.0e-1
D_FF = 14336
N_EXPERTS = 8
TOP_K = 2
MOE_BLOCK = 128
N_DENSE = (DEPTH + 1) // 2
N_MOE = DEPTH // 2
EPS = 1.0e-6
OFF_Q = D_ATTN
OFF_KV = OFF_Q + 6 * D_KV
OFF_GN = OFF_KV + 3 * N_HEADS
OFF_U = OFF_GN + D_SSM
N_IN = OFF_U + 2 * D_MODEL

kernel_name = 'nsa_s5_gated_hybrid_step'


def rmsnorm(x, g):
    xf = x.astype(jnp.float32)
    y = xf * lax.rsqrt(jnp.mean(xf * xf, axis=-1, keepdims=True) + EPS)
    return (y * g.astype(jnp.float32)).astype(x.dtype)


def alibi_slopes():
    h = jnp.arange(1, N_HEADS + 1, dtype=jnp.float32)
    return jnp.exp2(-ALIBI_MAX_EXP * h / N_HEADS).reshape(N_KV_HEADS, GROUP)


def masked_softmax(s, mask):
    s = jnp.where(mask, s, -jnp.inf)
    m = jnp.max(s, axis=-1, keepdims=True)
    m = jnp.where(jnp.isfinite(m), m, 0.0)
    p = jnp.where(mask, jnp.exp(s - m), 0.0)
    return p / jnp.maximum(jnp.sum(p, axis=-1, keepdims=True), 1e-30)


def mixer_inputs(h, w_in, g_q, g_k):
    B, T = h.shape[:2]
    z = h @ w_in
    q = rmsnorm(z[..., :OFF_Q].reshape(B, T, N_KV_HEADS, GROUP, HEAD_DIM), g_q)
    kv = z[..., OFF_Q:OFF_KV].reshape(B, T, 3, 2, N_KV_HEADS, HEAD_DIM)
    k = rmsnorm(kv[:, :, :, 0], g_k[:, None, :])
    rows = jnp.stack([k, kv[:, :, :, 1]], axis=-2)
    return (q, rows[:, :, 0], rows[:, :, 1], rows[:, :, 2],
            z[..., OFF_KV:OFF_GN], z[..., OFF_GN:OFF_U], z[..., OFF_U:])


def pool_blocks(rows, w_cmp):
    B, S = rows.shape[:2]
    blk = rows.reshape(B, S // BLOCK, BLOCK, N_KV_HEADS, 2, HEAD_DIM)
    pooled = jnp.einsum('bnlktd,tlk->bnktd', blk, w_cmp)
    return pooled[..., 0, :], pooled[..., 1, :]


def cmp_branch(q, kc, vc, qpos, slopes):
    nb = kc.shape[1]
    blk = jnp.arange(nb)
    center = (blk * BLOCK).astype(jnp.float32) + 0.5 * (BLOCK - 1)
    end = (blk + 1) * BLOCK - 1
    s = jnp.einsum('bckgd,bnkd->bckgn', q, kc, preferred_element_type=jnp.float32) * SCALE
    dist = qpos.astype(jnp.float32)[:, None] - center[None, :]
    s = s - slopes[None, None, :, :, None] * dist[None, :, None, None, :]
    valid = (end[None, :] <= qpos[:, None])[None, :, None, None, :]
    p = masked_softmax(s, valid)
    o = jnp.einsum('bckgn,bnkd->bckgd', p.astype(vc.dtype), vc)
    return o, jnp.sum(p, axis=3)


def select_blocks(imp, qpos, nb):
    blk = jnp.arange(nb)[None, :]
    cur = (qpos // BLOCK)[:, None]
    forced = ((blk == cur) | (blk == cur - 1) | (blk == 0))[None, :, None, :]
    future = (blk > cur)[None, :, None, :]
    score = jnp.where(forced, FORCED_SCORE, jnp.where(future, -jnp.inf, imp))
    _, idx = lax.top_k(score, min(N_SELECT, nb))
    return idx


def sel_core(q, kg, vg, kpos, qpos, slopes):
    s = jnp.einsum('bckgd,bcknld->bckgnl', q, kg, preferred_element_type=jnp.float32) * SCALE
    dist = qpos[None, :, None, None, None] - kpos
    s = s - slopes[None, None, :, :, None, None] * dist.astype(jnp.float32)[:, :, :, None]
    b, c, k, g, n, l = s.shape
    mask = (dist >= 0)[:, :, :, None].reshape(b, c, k, 1, n * l)
    p = masked_softmax(s.reshape(b, c, k, g, n * l), mask).reshape(s.shape)
    return jnp.einsum('bckgnl,bcknld->bckgd', p.astype(vg.dtype), vg)


def sel_prompt(q, kv_s, idx, slopes):
    B, T = q.shape[:2]
    nb = T // BLOCK
    blocks = kv_s.reshape(B, nb, BLOCK, N_KV_HEADS, 2, HEAD_DIM).transpose(0, 3, 1, 2, 4, 5)
    qc = min(SEL_QCHUNK, T)
    nc = T // qc
    bi = jnp.arange(B)[:, None, None, None]
    hi = jnp.arange(N_KV_HEADS)[None, None, :, None]

    def chunk(args):
        q_c, idx_c, pos_c = args
        rows = blocks[bi, hi, idx_c]
        kpos = idx_c[..., None] * BLOCK + jnp.arange(BLOCK)
        return sel_core(q_c, rows[..., 0, :], rows[..., 1, :], kpos, pos_c, slopes)

    to_chunks = lambda a: a.reshape(B, nc, qc, *a.shape[2:]).swapaxes(0, 1)
    out = lax.map(chunk, (to_chunks(q), to_chunks(idx), jnp.arange(T).reshape(nc, qc)))
    return out.swapaxes(0, 1).reshape(q.shape)


def win_core(q, k, v, qpos, kpos, slopes):
    s = jnp.einsum('bjckgd,bjskd->bjkgcs', q, k, preferred_element_type=jnp.float32) * SCALE
    dist = qpos[:, :, None] - kpos[:, None, :]
    mask = (dist >= 0) & (dist < WINDOW) & (kpos[:, None, :] >= 0)
    s = s - slopes[None, None, :, :, None, None] * dist.astype(jnp.float32)[None, :, None, None]
    p = masked_softmax(s, mask[None, :, None, None])
    return jnp.einsum('bjkgcs,bjskd->bjckgd', p.astype(v.dtype), v)


def win_prompt(q, kv_w, slopes):
    B, T = q.shape[:2]
    nq = T // WIN_QBLOCK
    nw = WINDOW // WIN_QBLOCK
    padded = jnp.pad(kv_w, ((0, 0), (nw * WIN_QBLOCK, 0), (0, 0), (0, 0), (0, 0)))
    padded = padded.reshape(B, nq + nw, WIN_QBLOCK, N_KV_HEADS, 2, HEAD_DIM)
    band = jnp.concatenate([padded[:, i:i + nq] for i in range(nw + 1)], axis=2)
    qb = q.reshape(B, nq, WIN_QBLOCK, N_KV_HEADS, GROUP, HEAD_DIM)
    qpos = jnp.arange(T).reshape(nq, WIN_QBLOCK)
    kpos = (jnp.arange(nq)[:, None] - nw) * WIN_QBLOCK + jnp.arange((nw + 1) * WIN_QBLOCK)[None, :]
    o = win_core(qb, band[..., 0, :], band[..., 1, :], qpos, kpos, slopes)
    return o.reshape(q.shape)


def nsa_combine(o_c, o_s, o_w, g_nsa):
    B, T = g_nsa.shape[:2]
    g = jax.nn.sigmoid(g_nsa.astype(jnp.float32)).reshape(B, T, 3, N_KV_HEADS, GROUP, 1)
    o = g[:, :, 0] * o_c + g[:, :, 1] * o_s + g[:, :, 2] * o_w
    return o.astype(o_c.dtype).reshape(B, T, D_ATTN)


def nsa_prompt(q, kv_c, kv_s, kv_w, g_nsa, w_cmp, slopes):
    T = q.shape[1]
    qpos = jnp.arange(T)
    kc, vc = pool_blocks(kv_c, w_cmp)
    o_c, imp = cmp_branch(q, kc, vc, qpos, slopes)
    idx = select_blocks(imp, qpos, T // BLOCK)
    o_s = sel_prompt(q, kv_s, idx, slopes)
    o_w = win_prompt(q, kv_w, slopes)
    return nsa_combine(o_c, o_s, o_w, g_nsa)


def nsa_sample(q, kv_c, kv_s, kv_w, g_nsa, w_cmp, slopes, cache_cmp, cache_sel, cache_win, page_table, layer):
    DB, C = q.shape[:2]
    past = page_table.shape[1] * PAGE_SIZE
    qpos = past + jnp.arange(C)
    past_c = cache_cmp[layer, page_table].reshape(DB, past, N_KV_HEADS, 2, HEAD_DIM)
    n_new = -(-C // BLOCK)
    new_c = jnp.pad(kv_c, ((0, 0), (0, n_new * BLOCK - C), (0, 0), (0, 0), (0, 0)))
    kc_p, vc_p = pool_blocks(past_c, w_cmp)
    kc_n, vc_n = pool_blocks(new_c, w_cmp)
    kc = jnp.concatenate([kc_p, kc_n], axis=1)
    vc = jnp.concatenate([vc_p, vc_n], axis=1)
    o_c, imp = cmp_branch(q, kc, vc, qpos, slopes)
    idx = select_blocks(imp, qpos, past // BLOCK + n_new)
    kpos = idx[..., None] * BLOCK + jnp.arange(BLOCK)
    pp = jnp.minimum(kpos, past - 1)
    bi = jnp.arange(DB)[:, None, None, None, None]
    hi = jnp.arange(N_KV_HEADS)[None, None, :, None, None]
    phys = page_table[bi, pp // PAGE_SIZE]
    past_rows = cache_sel[layer, phys, pp % PAGE_SIZE, hi]
    new_rows = kv_s[bi, jnp.clip(kpos - past, 0, C - 1), hi]
    rows = jnp.where((kpos < past)[..., None, None], past_rows, new_rows)
    o_s = sel_core(q, rows[..., 0, :], rows[..., 1, :], kpos, qpos, slopes)
    w_buf = cache_win.shape[2]
    keys = jnp.concatenate([cache_win[layer], kv_w], axis=1)
    kpos_w = past - w_buf + jnp.arange(w_buf + C)
    o_w = win_core(q[:, None], keys[:, None, ..., 0, :], keys[:, None, ..., 1, :],
                   qpos[None], kpos_w[None], slopes)[:, 0]
    return nsa_combine(o_c, o_s, o_w, g_nsa), keys[:, C:]


def s5_scan(u, h0, a_re, a_im, log_dt, b_re, b_im, c_re, c_im, d_skip):
    Bsz, T = u.shape[:2]
    x = u.astype(jnp.float32).reshape(Bsz, T, N_GROUPS, GROUP_CH)
    dt = jnp.exp(log_dt.astype(jnp.float32))[:, None]
    ar = a_re.astype(jnp.float32)
    ai = a_im.astype(jnp.float32)
    mag = jnp.exp(dt * ar)
    abar_re = mag * jnp.cos(dt * ai)
    abar_im = mag * jnp.sin(dt * ai)
    den = ar * ar + ai * ai
    cf_re = ((abar_re - 1.0) * ar + abar_im * ai) / den
    cf_im = (abar_im * ar - (abar_re - 1.0) * ai) / den
    br = b_re.astype(jnp.float32)
    bim = b_im.astype(jnp.float32)
    bb_re = cf_re[..., None] * br - cf_im[..., None] * bim
    bb_im = cf_re[..., None] * bim + cf_im[..., None] * br
    bu_re = jnp.einsum('btgc,gpc->btgp', x, bb_re)
    bu_im = jnp.einsum('btgc,gpc->btgp', x, bb_im)
    h0r = h0[..., 0].astype(jnp.float32)
    h0i = h0[..., 1].astype(jnp.float32)
    bu_re = bu_re.at[:, 0].add(abar_re * h0r - abar_im * h0i)
    bu_im = bu_im.at[:, 0].add(abar_re * h0i + abar_im * h0r)

    def combine(e1, e2):
        ar1, ai1, br1, bi1 = e1
        ar2, ai2, br2, bi2 = e2
        return (ar2 * ar1 - ai2 * ai1, ar2 * ai1 + ai2 * ar1,
                ar2 * br1 - ai2 * bi1 + br2, ar2 * bi1 + ai2 * br1 + bi2)

    elems = (jnp.broadcast_to(abar_re, bu_re.shape), jnp.broadcast_to(abar_im, bu_im.shape), bu_re, bu_im)
    _, _, s_re, s_im = lax.associative_scan(combine, elems, axis=1)
    y = (jnp.einsum('btgp,gcp->btgc', s_re, c_re.astype(jnp.float32))
         - jnp.einsum('btgp,gcp->btgc', s_im, c_im.astype(jnp.float32))
         + d_skip.astype(jnp.float32) * x)
    h_last = jnp.stack([s_re[:, -1], s_im[:, -1]], axis=-1)
    return y.reshape(Bsz, T, D_SSM).astype(u.dtype), h_last


def merge(o_attn, y_ssm, g_merge, w_attn_out, w_glu, w_out):
    p_a = o_attn @ w_attn_out
    ab = y_ssm @ w_glu
    p_b = ab[..., :D_MODEL] * jax.nn.sigmoid(ab[..., D_MODEL:])
    g_a = jax.nn.sigmoid(g_merge[..., :D_MODEL])
    g_b = jax.nn.sigmoid(g_merge[..., D_MODEL:])
    return (g_a * p_a + g_b * p_b) @ w_out


def swiglu(h, w1, w3, w2):
    return (jax.nn.silu(h @ w1) * (h @ w3)) @ w2


def moe(h, w_router, w1, w3, w2, j):
    shp = h.shape
    x = h.reshape(-1, D_MODEL)
    N = x.shape[0]
    logits = jnp.matmul(x, w_router, preferred_element_type=jnp.float32)
    top_v, top_i = lax.top_k(logits, TOP_K)
    gates = jax.nn.softmax(top_v, axis=-1)
    A = N * TOP_K
    e_flat = top_i.reshape(A)
    order = jnp.argsort(e_flat)
    e_sorted = e_flat[order]
    tok_sorted = (order // TOP_K).astype(jnp.int32)
    gate_sorted = gates.reshape(A)[order]
    counts = jnp.bincount(e_flat, length=N_EXPERTS)
    padded = (counts + MOE_BLOCK - 1) // MOE_BLOCK * MOE_BLOCK
    pad_end = jnp.cumsum(padded)
    start = jnp.cumsum(counts) - counts
    dest = (pad_end - padded)[e_sorted] + jnp.arange(A) - start[e_sorted]
    n_blk = -(-A // MOE_BLOCK) + N_EXPERTS
    n_rows = n_blk * MOE_BLOCK
    row_tok = jnp.full((n_rows,), N, jnp.int32).at[dest].set(tok_sorted)
    row_gate = jnp.zeros((n_rows,), jnp.float32).at[dest].set(gate_sorted)
    blk_exp = jnp.minimum(jnp.searchsorted(pad_end, jnp.arange(n_blk) * MOE_BLOCK, side='right'), N_EXPERTS - 1)
    xp = jnp.concatenate([x, jnp.zeros((1, D_MODEL), x.dtype)], axis=0)
    xb = xp[row_tok].reshape(n_blk, MOE_BLOCK, D_MODEL)
    yb = lax.map(lambda a: swiglu(a[0], w1[j, a[1]], w3[j, a[1]], w2[j, a[1]]), (xb, blk_exp))
    y = jnp.zeros((N + 1, D_MODEL), jnp.float32).at[row_tok].add(
        yb.reshape(n_rows, D_MODEL).astype(jnp.float32) * row_gate[:, None])
    return y[:N].astype(h.dtype).reshape(shp)


def channel_mixer(h, layer, ffn_w1, ffn_w3, ffn_w2, moe_router, moe_w1, moe_w3, moe_w2):
    if layer % 2 == 0:
        j = layer // 2
        return swiglu(h, ffn_w1[j], ffn_w3[j], ffn_w2[j])
    j = layer // 2
    return moe(h, moe_router[j], moe_w1, moe_w3, moe_w2, j)


def setup_inputs(seed: int = 0) -> dict:
    key = jax.random.key(seed)
    ks = iter(jax.random.split(key, 48))
    nrm = lambda shape, scale: scale * jax.random.normal(next(ks), shape, jnp.float32)
    gain = lambda shape: 1.0 + nrm(shape, 0.02)
    n_pages = PAST_LEN // PAGE_SIZE
    n_used = DEC_BATCH * n_pages
    n_pool = n_used + max(1, n_used // 4)
    w_buf = min(WINDOW, PAST_LEN)
    page_table = jax.random.permutation(next(ks), n_pool)[:n_used].reshape(DEC_BATCH, n_pages).astype(jnp.int32)
    n_idx = jnp.arange(P_STATE, dtype=jnp.float32)
    return {
        'x_prompt': nrm((BATCH, SEQ, D_MODEL), 1.0),
        'x_sample': nrm((DEC_BATCH, DEC_SEQ, D_MODEL), 1.0),
        'cache_kv_cmp': nrm((DEPTH, n_pool, PAGE_SIZE, N_KV_HEADS, 2, HEAD_DIM), 1.0),
        'cache_kv_sel': nrm((DEPTH, n_pool, PAGE_SIZE, N_KV_HEADS, 2, HEAD_DIM), 1.0),
        'cache_kv_win': nrm((DEPTH, DEC_BATCH, w_buf, N_KV_HEADS, 2, HEAD_DIM), 1.0),
        'state_ssm': nrm((DEPTH, DEC_BATCH, N_GROUPS, P_STATE, 2), 0.3),
        'page_table': page_table,
        'norm_mix': gain((DEPTH, D_MODEL)),
        'norm_ffn': gain((DEPTH, D_MODEL)),
        'w_in': nrm((DEPTH, D_MODEL, N_IN), D_MODEL ** -0.5),
        'g_q': gain((DEPTH, HEAD_DIM)),
        'g_k': gain((DEPTH, 3, HEAD_DIM)),
        'w_cmp': (1.0 + nrm((DEPTH, 2, BLOCK, N_KV_HEADS), 0.1)) / BLOCK,
        'w_attn_out': nrm((DEPTH, D_ATTN, D_MODEL), D_ATTN ** -0.5),
        'ssm_a_re': -0.5 + nrm((DEPTH, N_GROUPS, P_STATE), 0.01),
        'ssm_a_im': math.pi * n_idx + nrm((DEPTH, N_GROUPS, P_STATE), 0.01),
        'ssm_log_dt': jax.random.uniform(next(ks), (DEPTH, N_GROUPS), jnp.float32, math.log(DT_MIN), math.log(DT_MAX)),
        'ssm_b_re': nrm((DEPTH, N_GROUPS, P_STATE, GROUP_CH), (2 * GROUP_CH) ** -0.5),
        'ssm_b_im': nrm((DEPTH, N_GROUPS, P_STATE, GROUP_CH), (2 * GROUP_CH) ** -0.5),
        'ssm_c_re': nrm((DEPTH, N_GROUPS, GROUP_CH, P_STATE), P_STATE ** -0.5),
        'ssm_c_im': nrm((DEPTH, N_GROUPS, GROUP_CH, P_STATE), P_STATE ** -0.5),
        'ssm_d': nrm((DEPTH, N_GROUPS, GROUP_CH), 1.0),
        'w_glu': nrm((DEPTH, D_SSM, 2 * D_MODEL), D_SSM ** -0.5),
        'w_out': nrm((DEPTH, D_MODEL, D_MODEL), D_MODEL ** -0.5),
        'ffn_w1': nrm((N_DENSE, D_MODEL, D_FF), D_MODEL ** -0.5),
        'ffn_w3': nrm((N_DENSE, D_MODEL, D_FF), D_MODEL ** -0.5),
        'ffn_w2': nrm((N_DENSE, D_FF, D_MODEL), D_FF ** -0.5),
        'moe_router': nrm((N_MOE, D_MODEL, N_EXPERTS), D_MODEL ** -0.5),
        'moe_w1': nrm((N_MOE, N_EXPERTS, D_MODEL, D_FF), D_MODEL ** -0.5),
        'moe_w3': nrm((N_MOE, N_EXPERTS, D_MODEL, D_FF), D_MODEL ** -0.5),
        'moe_w2': nrm((N_MOE, N_EXPERTS, D_FF, D_MODEL), D_FF ** -0.5),
    }


def reference(x_prompt, x_sample, cache_kv_cmp, cache_kv_sel, cache_kv_win, state_ssm, page_table,
              norm_mix, norm_ffn, w_in, g_q, g_k, w_cmp, w_attn_out,
              ssm_a_re, ssm_a_im, ssm_log_dt, ssm_b_re, ssm_b_im, ssm_c_re, ssm_c_im, ssm_d,
              w_glu, w_out, ffn_w1, ffn_w3, ffn_w2, moe_router, moe_w1, moe_w3, moe_w2):
    slopes = alibi_slopes()
    xp = x_prompt
    xs = x_sample
    B, T = xp.shape[:2]
    n_win_p = min(WINDOW, T)
    cmp_p, cmp_s, sel_p, sel_s, win_p, win_s, ssm_p, ssm_s = [], [], [], [], [], [], [], []
    for l in range(DEPTH):
        ssm = (ssm_a_re[l], ssm_a_im[l], ssm_log_dt[l], ssm_b_re[l], ssm_b_im[l], ssm_c_re[l], ssm_c_im[l], ssm_d[l])
        q, kc, ks, kw, gn, u, gm = mixer_inputs(rmsnorm(xp, norm_mix[l]), w_in[l], g_q[l], g_k[l])
        o_a = nsa_prompt(q, kc, ks, kw, gn, w_cmp[l], slopes)
        y_b, h_fin = s5_scan(u, jnp.zeros((B, N_GROUPS, P_STATE, 2), jnp.float32), *ssm)
        xp = xp + merge(o_a, y_b, gm, w_attn_out[l], w_glu[l], w_out[l])
        xp = xp + channel_mixer(rmsnorm(xp, norm_ffn[l]), l, ffn_w1, ffn_w3, ffn_w2,
                                moe_router, moe_w1, moe_w3, moe_w2)
        cmp_p.append(kc)
        sel_p.append(ks)
        win_p.append(kw[:, T - n_win_p:])
        ssm_p.append(h_fin)
        q, kc, ks, kw, gn, u, gm = mixer_inputs(rmsnorm(xs, norm_mix[l]), w_in[l], g_q[l], g_k[l])
        o_a, new_win = nsa_sample(q, kc, ks, kw, gn, w_cmp[l], slopes,
                                  cache_kv_cmp, cache_kv_sel, cache_kv_win, page_table, l)
        y_b, h_fin = s5_scan(u, state_ssm[l], *ssm)
        xs = xs + merge(o_a, y_b, gm, w_attn_out[l], w_glu[l], w_out[l])
        xs = xs + channel_mixer(rmsnorm(xs, norm_ffn[l]), l, ffn_w1, ffn_w3, ffn_w2,
                                moe_router, moe_w1, moe_w3, moe_w2)
        cmp_s.append(kc)
        sel_s.append(ks)
        win_s.append(new_win)
        ssm_s.append(h_fin)
    return (xp, xs, jnp.stack(cmp_p), jnp.stack(cmp_s), jnp.stack(sel_p), jnp.stack(sel_s),
            jnp.stack(win_p), jnp.stack(win_s), jnp.stack(ssm_p), jnp.stack(ssm_s))
```

```python
import functools
import math

import numpy as np
import jax
import jax.numpy as jnp
from jax import lax
from jax.experimental import pallas as pl
from jax.experimental.pallas import tpu as pltpu

HEAD_DIM = 128
N_HEADS = 16
N_KV_HEADS = 4
GROUP = N_HEADS // N_KV_HEADS
BLOCK = 64
N_SELECT = 16
WINDOW = 512
PAGE_SIZE = 128
GROUP_CH = 16
P_STATE = 64
N_EXPERTS = 8
TOP_K = 2
EPS = 1.0e-6
FORCED_SCORE = 1.0e6
ALIBI_MAX_EXP = 8.0
SCALE = HEAD_DIM ** -0.5
SSM_CHUNK = 16
LANE = 128
SUBLANE = 8
VMEM_LIMIT = 56 << 20

_BF = jnp.bfloat16
_F32 = jnp.float32
_HI = lax.Precision.HIGHEST


def _tile(n, pref, mult=SUBLANE):
    if n <= pref:
        return n
    for t in range(pref - pref % mult, 0, -mult):
        if n % t == 0:
            return t
    return n


def _params(sem):
    return pltpu.CompilerParams(dimension_semantics=sem, vmem_limit_bytes=VMEM_LIMIT)


def _rms_kernel(x_ref, g_ref, o_ref):
    x = x_ref[...]
    inv = lax.rsqrt(jnp.mean(x * x, axis=-1, keepdims=True) + EPS)
    o_ref[...] = (x * inv * g_ref[...]).astype(o_ref.dtype)


def _rmsnorm(x, g, out_dtype=_BF):
    m, d = x.shape
    tm = _tile(m, 256)
    return pl.pallas_call(
        _rms_kernel,
        grid=(m // tm,),
        in_specs=[pl.BlockSpec((tm, d), lambda i: (i, 0)), pl.BlockSpec((1, d), lambda i: (0, 0))],
        out_specs=pl.BlockSpec((tm, d), lambda i: (i, 0)),
        out_shape=jax.ShapeDtypeStruct((m, d), out_dtype),
        compiler_params=_params(("arbitrary",)),
        name="rmsnorm",
    )(x, g.reshape(1, d))


def _sigmoid(x):
    return 1.0 / (1.0 + jnp.exp(-x))


def _mm_kernel(*refs, n_w, mode):
    x_ref, xs_ref = refs[0], refs[1]
    w_refs = refs[2:2 + n_w]
    p = 2 + n_w
    extra = ()
    if mode in ("headnorm", "resid"):
        extra = refs[p:p + 2]
        p += 2
    o_ref, os_ref = refs[p], refs[p + 1]
    wb_refs = refs[p + 2:p + 2 + n_w]
    i = pl.program_id(1)

    @pl.when(i == 0)
    def _():
        for w_ref, wb_ref in zip(w_refs, wb_refs):
            wb_ref[...] = w_ref[...].astype(_BF)

    def compute(x, out_ref, res):
        accs = [jnp.dot(x, wb[...], preferred_element_type=_F32) for wb in wb_refs]
        if mode == "plain":
            out_ref[...] = accs[0].astype(out_ref.dtype)
        elif mode == "swiglu":
            a = accs[0]
            out_ref[...] = ((a * _sigmoid(a)) * accs[1]).astype(out_ref.dtype)
        elif mode == "resid":
            out_ref[...] = (res + accs[0]).astype(out_ref.dtype)
        else:
            acc = accs[0]
            gain_ref, flag_ref = extra
            for c in range(acc.shape[1] // HEAD_DIM):
                sl = slice(c * HEAD_DIM, (c + 1) * HEAD_DIM)
                a = acc[:, sl]
                nrm = a * lax.rsqrt(jnp.mean(a * a, axis=-1, keepdims=True) + EPS) * gain_ref[:, sl]
                out_ref[:, sl] = jnp.where(flag_ref[:, sl] > 0.0, nrm, a).astype(out_ref.dtype)

    compute(x_ref[...], o_ref, extra[0][...] if mode == "resid" else None)

    @pl.when(i == 0)
    def _():
        compute(xs_ref[...], os_ref, extra[1][...] if mode == "resid" else None)


def _mm(x, xs, ws, n_cols, *, mode, out_dtype, tm, tn, gain=None, flag=None, res=None, res_s=None,
        name="mm"):
    m, k = x.shape
    ms = xs.shape[0]
    nj, ni = n_cols // tn, m // tm
    in_specs = [pl.BlockSpec((tm, k), lambda j, i: (i, 0)), pl.BlockSpec((ms, k), lambda j, i: (0, 0))]
    args = [x, xs]
    for w, lid, off in ws:
        in_specs.append(pl.BlockSpec((None, k, tn), functools.partial(lambda j, i, lid, off: (lid, 0, j + off),
                                                                       lid=lid, off=off)))
        args.append(w)
    if mode == "headnorm":
        in_specs += [pl.BlockSpec((1, tn), lambda j, i: (0, j)), pl.BlockSpec((1, tn), lambda j, i: (0, j))]
        args += [gain, flag]
    elif mode == "resid":
        in_specs += [pl.BlockSpec((tm, tn), lambda j, i: (i, j)), pl.BlockSpec((ms, tn), lambda j, i: (0, j))]
        args += [res, res_s]
    return pl.pallas_call(
        functools.partial(_mm_kernel, n_w=len(ws), mode=mode),
        grid=(nj, ni),
        in_specs=in_specs,
        out_specs=[pl.BlockSpec((tm, tn), lambda j, i: (i, j)), pl.BlockSpec((ms, tn), lambda j, i: (0, j))],
        out_shape=[jax.ShapeDtypeStruct((m, n_cols), out_dtype), jax.ShapeDtypeStruct((ms, n_cols), out_dtype)],
        scratch_shapes=[pltpu.VMEM((k, tn), _BF) for _ in ws],
        compiler_params=_params(("arbitrary", "arbitrary")),
        name=name,
    )(*args)


def _merge_kernel(oa_ref, oas_ref, y_ref, ys_ref, wa_ref, w1_ref, w2_ref, ga_ref, gb_ref, gas_ref, gbs_ref,
                  o_ref, os_ref, wab, w1b, w2b):
    i = pl.program_id(1)

    @pl.when(i == 0)
    def _():
        wab[...] = wa_ref[...].astype(_BF)
        w1b[...] = w1_ref[...].astype(_BF)
        w2b[...] = w2_ref[...].astype(_BF)

    def compute(oa, y, ga, gb, out_ref):
        pa = jnp.dot(oa, wab[...], preferred_element_type=_F32)
        ab1 = jnp.dot(y, w1b[...], preferred_element_type=_F32)
        ab2 = jnp.dot(y, w2b[...], preferred_element_type=_F32)
        pb = ab1 * _sigmoid(ab2)
        out_ref[...] = (_sigmoid(ga) * pa + _sigmoid(gb) * pb).astype(out_ref.dtype)

    compute(oa_ref[...], y_ref[...], ga_ref[...], gb_ref[...], o_ref)

    @pl.when(i == 0)
    def _():
        compute(oas_ref[...], ys_ref[...], gas_ref[...], gbs_ref[...], os_ref)


def _merge(oa, oas, y, ys, w_attn_out, w_glu, layer, zug, zug_s, d_model, d_ssm, *, tm, tn):
    m, ka = oa.shape
    ks = y.shape[1]
    ms = oas.shape[0]
    nj, ni = d_model // tn, m // tm
    off_a = d_ssm // tn
    off_b = (d_ssm + d_model) // tn
    nglu = d_model // tn
    return pl.pallas_call(
        _merge_kernel,
        grid=(nj, ni),
        in_specs=[
            pl.BlockSpec((tm, ka), lambda j, i: (i, 0)), pl.BlockSpec((ms, ka), lambda j, i: (0, 0)),
            pl.BlockSpec((tm, ks), lambda j, i: (i, 0)), pl.BlockSpec((ms, ks), lambda j, i: (0, 0)),
            pl.BlockSpec((None, ka, tn), lambda j, i: (layer, 0, j)),
            pl.BlockSpec((None, ks, tn), lambda j, i: (layer, 0, j)),
            pl.BlockSpec((None, ks, tn), lambda j, i: (layer, 0, j + nglu)),
            pl.BlockSpec((tm, tn), lambda j, i: (i, j + off_a)), pl.BlockSpec((tm, tn), lambda j, i: (i, j + off_b)),
            pl.BlockSpec((ms, tn), lambda j, i: (0, j + off_a)), pl.BlockSpec((ms, tn), lambda j, i: (0, j + off_b)),
        ],
        out_specs=[pl.BlockSpec((tm, tn), lambda j, i: (i, j)), pl.BlockSpec((ms, tn), lambda j, i: (0, j))],
        out_shape=[jax.ShapeDtypeStruct((m, d_model), _BF), jax.ShapeDtypeStruct((ms, d_model), _BF)],
        scratch_shapes=[pltpu.VMEM((ka, tn), _BF), pltpu.VMEM((ks, tn), _BF), pltpu.VMEM((ks, tn), _BF)],
        compiler_params=_params(("arbitrary", "arbitrary")),
        name="merge",
    )(oa, oas, y, ys, w_attn_out, w_glu, w_glu, zug, zug, zug_s, zug_s)


def _down_kernel(a_ref, as_ref, w_ref, res_ref, ress_ref, o_ref, os_ref):
    i = pl.program_id(0)
    k = pl.program_id(1)
    wb = w_ref[...].astype(_BF)
    part = jnp.dot(a_ref[...], wb, preferred_element_type=_F32)

    @pl.when(k == 0)
    def _():
        o_ref[...] = res_ref[...] + part

    @pl.when(k > 0)
    def _():
        o_ref[...] += part

    @pl.when(i == 0)
    def _():
        part_s = jnp.dot(as_ref[...], wb, preferred_element_type=_F32)

        @pl.when(k == 0)
        def _():
            os_ref[...] = ress_ref[...] + part_s

        @pl.when(k > 0)
        def _():
            os_ref[...] += part_s


def _ffn_down(a, a_s, w2, layer, res, res_s, *, tm, tk):
    m, f = a.shape
    ms = a_s.shape[0]
    n = w2.shape[2]
    return pl.pallas_call(
        _down_kernel,
        grid=(m // tm, f // tk),
        in_specs=[
            pl.BlockSpec((tm, tk), lambda i, k: (i, k)), pl.BlockSpec((ms, tk), lambda i, k: (0, k)),
            pl.BlockSpec((None, tk, n), lambda i, k: (layer, k, 0)),
            pl.BlockSpec((tm, n), lambda i, k: (i, 0)), pl.BlockSpec((ms, n), lambda i, k: (0, 0)),
        ],
        out_specs=[pl.BlockSpec((tm, n), lambda i, k: (i, 0)), pl.BlockSpec((ms, n), lambda i, k: (0, 0))],
        out_shape=[jax.ShapeDtypeStruct((m, n), _F32), jax.ShapeDtypeStruct((ms, n), _F32)],
        compiler_params=_params(("arbitrary", "arbitrary")),
        name="ffn_down",
    )(a, a_s, w2, res, res_s)


def _masked_softmax(s, mask):
    s = jnp.where(mask, s, -jnp.inf)
    m = jnp.max(s, axis=-1, keepdims=True)
    m = jnp.where(jnp.isfinite(m), m, 0.0)
    p = jnp.where(mask, jnp.exp(s - m), 0.0)
    den = jnp.maximum(jnp.sum(p, axis=-1, keepdims=True), 1e-30)
    return p * (1.0 / den)


def _dot_t(a, b, precision=None):
    return lax.dot_general(a, b, (((1,), (1,)), ((), ())), precision=precision, preferred_element_type=_F32)


def _nsa_prompt_kernel(slope_ref, q_ref, kc_ref, vc_ref, ks_ref, vs_ref, kw_ref, vw_ref, gate_ref,
                       pk_ref, pv_ref, e_ref, o_ref, kcp, vcp, ksb, vsb, kwb, vwb, *, tq, seq, nb, band):
    kh = pl.program_id(1)
    qi = pl.program_id(2)

    @pl.when(qi == 0)
    def _():
        kcp[...] = jnp.dot(pk_ref[...].astype(_BF), kc_ref[...].astype(_BF),
                           preferred_element_type=_F32).astype(_BF)
        vcp[...] = jnp.dot(pv_ref[...].astype(_BF), vc_ref[...].astype(_BF),
                           preferred_element_type=_F32).astype(_BF)
        ksb[...] = ks_ref[...].astype(_BF)
        vsb[...] = vs_ref[...].astype(_BF)
        kwb[...] = kw_ref[...].astype(_BF)
        vwb[...] = vw_ref[...].astype(_BF)

    t0 = qi * tq
    qpos = t0 + lax.broadcasted_iota(jnp.int32, (tq, 1), 0)
    qpos_f = qpos.astype(_F32)
    q = q_ref[...]
    gate = _sigmoid(gate_ref[...])

    blk = lax.broadcasted_iota(jnp.int32, (tq, nb), 1)
    center = (blk * BLOCK).astype(_F32) + 0.5 * (BLOCK - 1)
    valid_c = ((blk + 1) * BLOCK - 1) <= qpos
    dist_c = qpos_f - center
    kc = kcp[...]
    vcb = vcp[...]
    o_cmp = []
    imp = jnp.zeros((tq, nb), _F32)
    for g in range(GROUP):
        slope = slope_ref[kh * GROUP + g]
        qg = q[:, g * HEAD_DIM:(g + 1) * HEAD_DIM].astype(_BF)
        s = _dot_t(qg, kc) * SCALE - slope * dist_c
        p = _masked_softmax(s, valid_c)
        imp = imp + p
        o_cmp.append(jnp.dot(p.astype(_BF), vcb, preferred_element_type=_F32))

    cur = qpos // BLOCK
    forced = (blk == cur) | (blk == cur - 1) | (blk == 0)
    score = jnp.where(forced, FORCED_SCORE, jnp.where(blk > cur, -jnp.inf, imp))
    rank = jnp.zeros((tq, nb), _F32)
    for i in range(nb):
        col = score[:, i:i + 1]
        beats = (col > score) | ((col == score) & (blk > i))
        rank = rank + jnp.where(beats, 1.0, 0.0)
    sel = jnp.where(rank < float(min(N_SELECT, nb)), 1.0, 0.0).astype(_BF)
    sel_keys = jnp.dot(sel, e_ref[...], preferred_element_type=_F32) > 0.5

    kpos = lax.broadcasted_iota(jnp.int32, (1, seq), 1)
    dist_s = qpos - kpos
    mask_s = sel_keys & (dist_s >= 0)
    dist_sf = dist_s.astype(_F32)

    ws = pl.multiple_of(jnp.maximum(t0 - WINDOW, 0), math.gcd(tq, WINDOW))
    kpos_w = ws + lax.broadcasted_iota(jnp.int32, (1, band), 1)
    dist_w = qpos - kpos_w
    mask_w = (dist_w >= 0) & (dist_w < WINDOW)
    dist_wf = dist_w.astype(_F32)
    kw_band = kwb[pl.ds(ws, band), :]
    vw_band = vwb[pl.ds(ws, band), :]

    for g in range(GROUP):
        slope = slope_ref[kh * GROUP + g]
        qg = q[:, g * HEAD_DIM:(g + 1) * HEAD_DIM].astype(_BF)
        s = _dot_t(qg, ksb[...]) * SCALE - slope * dist_sf
        p = _masked_softmax(s, mask_s)
        o_sel = jnp.dot(p.astype(_BF), vsb[...], preferred_element_type=_F32)
        s = _dot_t(qg, kw_band) * SCALE - slope * dist_wf
        p = _masked_softmax(s, mask_w)
        o_win = jnp.dot(p.astype(_BF), vw_band, preferred_element_type=_F32)
        o = (gate[:, g:g + 1] * o_cmp[g] + gate[:, GROUP + g:GROUP + g + 1] * o_sel
             + gate[:, 2 * GROUP + g:2 * GROUP + g + 1] * o_win)
        o_ref[:, g * HEAD_DIM:(g + 1) * HEAD_DIM] = o.astype(o_ref.dtype)


def _alibi_slopes():
    h = np.arange(1, N_HEADS + 1, dtype=np.float32)
    return jnp.asarray(np.exp2(-ALIBI_MAX_EXP * h / N_HEADS).astype(np.float32))


def _nsa_prompt(zqkv, gates_r, pool_k, pool_v, expand, batch, seq):
    nb = seq // BLOCK
    tq = _tile(seq, 256)
    nq = seq // tq
    band = min(WINDOW + tq, seq)
    d_attn = N_HEADS * HEAD_DIM
    kv0 = d_attn // HEAD_DIM

    def kv_spec(branch, which):
        base = kv0 + (branch * 2 + which) * N_KV_HEADS
        return pl.BlockSpec((seq, HEAD_DIM), lambda b, k, qi, sl: (b, base + k))

    grid_spec = pltpu.PrefetchScalarGridSpec(
        num_scalar_prefetch=1,
        grid=(batch, N_KV_HEADS, nq),
        in_specs=[
            pl.BlockSpec((tq, GROUP * HEAD_DIM), lambda b, k, qi, sl: (b * nq + qi, k)),
            kv_spec(0, 0), kv_spec(0, 1), kv_spec(1, 0), kv_spec(1, 1), kv_spec(2, 0), kv_spec(2, 1),
            pl.BlockSpec((None, tq, 3 * GROUP), lambda b, k, qi, sl: (k, b * nq + qi, 0)),
            pl.BlockSpec((None, nb, seq), lambda b, k, qi, sl: (k, 0, 0)),
            pl.BlockSpec((None, nb, seq), lambda b, k, qi, sl: (k, 0, 0)),
            pl.BlockSpec((nb, seq), lambda b, k, qi, sl: (0, 0)),
        ],
        out_specs=pl.BlockSpec((tq, GROUP * HEAD_DIM), lambda b, k, qi, sl: (b * nq + qi, k)),
        scratch_shapes=[pltpu.VMEM((nb, HEAD_DIM), _BF), pltpu.VMEM((nb, HEAD_DIM), _BF)]
        + [pltpu.VMEM((seq, HEAD_DIM), _BF) for _ in range(4)],
    )
    return pl.pallas_call(
        functools.partial(_nsa_prompt_kernel, tq=tq, seq=seq, nb=nb, band=band),
        grid_spec=grid_spec,
        out_shape=jax.ShapeDtypeStruct((batch * seq, d_attn), _BF),
        compiler_params=_params(("arbitrary", "arbitrary", "arbitrary")),
        name="nsa_prompt",
    )(_alibi_slopes(), zqkv, zqkv, zqkv, zqkv, zqkv, zqkv, zqkv, gates_r, pool_k, pool_v, expand)


def _pool_pages_kernel(pt_ref, c_ref, w_ref, o_ref):
    x = c_ref[...].astype(_BF).astype(_F32)
    w = w_ref[...].astype(_BF).astype(_F32)
    rows = []
    for n in range(PAGE_SIZE // BLOCK):
        rows.append(jnp.sum(x[n * BLOCK:(n + 1) * BLOCK] * w, axis=0, keepdims=True))
    o_ref[...] = jnp.concatenate(rows, axis=0).astype(_BF).astype(_F32)


def _pool_pages(cache_flat, page_table, w_exp, layer, n_pool):
    db, n_pages = page_table.shape
    width = cache_flat.shape[-1]
    per = PAGE_SIZE // BLOCK
    grid_spec = pltpu.PrefetchScalarGridSpec(
        num_scalar_prefetch=1,
        grid=(db, n_pages),
        in_specs=[
            pl.BlockSpec((None, PAGE_SIZE, width), lambda b, p, pt: (layer * n_pool + pt[b * n_pages + p], 0, 0)),
            pl.BlockSpec((BLOCK, width), lambda b, p, pt: (0, 0)),
        ],
        out_specs=pl.BlockSpec((None, None, per, width), lambda b, p, pt: (b, p, 0, 0)),
    )
    return pl.pallas_call(
        _pool_pages_kernel,
        grid_spec=grid_spec,
        out_shape=jax.ShapeDtypeStruct((db, n_pages, per, width), _F32),
        compiler_params=_params(("arbitrary", "arbitrary")),
        name="pool_pages",
    )(page_table.reshape(-1), cache_flat, w_exp)


def _cmp_select_kernel(slope_ref, q_ref, pool_ref, new_ref, w0_ref, oc_ref, idx_ref, *, past, n_sel):
    nbp = pool_ref.shape[0]
    qpos = past
    q = q_ref[...]
    pooled_new = new_ref[...] * w0_ref[...]
    blk = lax.broadcasted_iota(jnp.int32, (1, nbp), 1)
    center = (blk * BLOCK).astype(_F32) + 0.5 * (BLOCK - 1)
    valid = ((blk + 1) * BLOCK - 1) <= qpos
    dist = float(qpos) - center
    center_n = float(nbp * BLOCK) + 0.5 * (BLOCK - 1)
    valid_n = jnp.full((1, 1), (nbp + 1) * BLOCK - 1 <= qpos)
    dist_n = float(qpos) - center_n
    rows_i = lax.broadcasted_iota(jnp.int32, (nbp, nbp), 0)
    cols_i = lax.broadcasted_iota(jnp.int32, (nbp, nbp), 1)
    sel_r = lax.broadcasted_iota(jnp.int32, (n_sel, nbp), 0).astype(_F32)
    blk_f = blk.astype(_F32)
    cur = qpos // BLOCK
    for kh in range(N_KV_HEADS):
        kc = pool_ref[:, (2 * kh) * HEAD_DIM:(2 * kh + 1) * HEAD_DIM]
        vc = pool_ref[:, (2 * kh + 1) * HEAD_DIM:(2 * kh + 2) * HEAD_DIM]
        kn = pooled_new[:, (2 * kh) * HEAD_DIM:(2 * kh + 1) * HEAD_DIM]
        vn = pooled_new[:, (2 * kh + 1) * HEAD_DIM:(2 * kh + 2) * HEAD_DIM]
        qk = q[kh * GROUP:(kh + 1) * GROUP]
        slope = jnp.concatenate([jnp.full((1, 1), slope_ref[kh * GROUP + g], _F32) for g in range(GROUP)], axis=0)
        s = _dot_t(qk.astype(_BF), kc.astype(_BF)) * SCALE - slope * dist
        s_n = jnp.sum(qk * kn, axis=-1, keepdims=True) * SCALE - slope * dist_n
        s = jnp.where(valid, s, -jnp.inf)
        s_n = jnp.where(valid_n, s_n, -jnp.inf)
        m = jnp.maximum(jnp.max(s, axis=-1, keepdims=True), s_n)
        m = jnp.where(jnp.isfinite(m), m, 0.0)
        p = jnp.where(valid, jnp.exp(s - m), 0.0)
        p_n = jnp.where(valid_n, jnp.exp(s_n - m), 0.0)
        den = jnp.maximum(jnp.sum(p, axis=-1, keepdims=True) + p_n, 1e-30)
        inv = 1.0 / den
        p = p * inv
        p_n = p_n * inv
        oc_ref[kh * GROUP:(kh + 1) * GROUP, :] = (
            jnp.dot(p.astype(_BF), vc.astype(_BF), preferred_element_type=_F32) + p_n * vn)
        imp = jnp.sum(p, axis=0, keepdims=True)
        imp_n = jnp.sum(p_n, axis=0, keepdims=True)
        forced = (blk == cur) | (blk == cur - 1) | (blk == 0)
        score = jnp.where(forced, FORCED_SCORE, jnp.where(blk > cur, -jnp.inf, imp))
        forced_n = (nbp == cur) | (nbp == cur - 1) | (nbp == 0)
        score_n = jnp.where(forced_n, FORCED_SCORE, jnp.where(nbp > cur, -jnp.inf, imp_n))
        sj = jnp.broadcast_to(score, (nbp, nbp))
        si = jnp.transpose(sj)
        ahead = (si > sj) | ((si == sj) & (rows_i < cols_i))
        rank = jnp.sum(jnp.where(ahead, 1.0, 0.0), axis=0, keepdims=True) + jnp.where(score_n > score, 1.0, 0.0)
        rank_n = jnp.sum(jnp.where(score >= score_n, 1.0, 0.0), axis=-1, keepdims=True)
        onehot = jnp.where(rank == sel_r, 1.0, 0.0)
        sel_col = lax.broadcasted_iota(jnp.int32, (n_sel, 1), 0).astype(_F32)
        idx = jnp.sum(onehot * blk_f, axis=-1, keepdims=True) + jnp.where(rank_n == sel_col, float(nbp), 0.0)
        idx_ref[kh] = jnp.broadcast_to(idx, (n_sel, LANE)).astype(jnp.int32)


def _cmp_select(q_s, pooled, new_rows, w0, past, n_sel):
    db, nbp, width = pooled.shape
    grid_spec = pltpu.PrefetchScalarGridSpec(
        num_scalar_prefetch=1,
        grid=(db,),
        in_specs=[
            pl.BlockSpec((None, N_HEADS, HEAD_DIM), lambda b, sl: (b, 0, 0)),
            pl.BlockSpec((None, nbp, width), lambda b, sl: (b, 0, 0)),
            pl.BlockSpec((None, 1, width), lambda b, sl: (b, 0, 0)),
            pl.BlockSpec((1, width), lambda b, sl: (0, 0)),
        ],
        out_specs=[
            pl.BlockSpec((None, N_HEADS, HEAD_DIM), lambda b, sl: (b, 0, 0)),
            pl.BlockSpec((None, N_KV_HEADS, n_sel, LANE), lambda b, sl: (b, 0, 0, 0)),
        ],
    )
    return pl.pallas_call(
        functools.partial(_cmp_select_kernel, past=past, n_sel=n_sel),
        grid_spec=grid_spec,
        out_shape=[jax.ShapeDtypeStruct((db, N_HEADS, HEAD_DIM), _F32),
                   jax.ShapeDtypeStruct((db, N_KV_HEADS, n_sel, LANE), jnp.int32)],
        compiler_params=_params(("arbitrary",)),
        name="cmp_select",
    )(_alibi_slopes(), q_s, pooled, new_rows, w0)


def _sel_sample_kernel(idx_ref, pt_ref, slope_ref, q_ref, blk_ref, new_ref, o_ref, rows_ref, *, past, n_sel):
    b = pl.program_id(0)
    kh = pl.program_id(1)
    j = pl.program_id(2)
    base = (b * N_KV_HEADS + kh) * n_sel
    kpos_c = idx_ref[base + j] * BLOCK + lax.broadcasted_iota(jnp.int32, (BLOCK, 1), 0)
    rows_ref[pl.ds(pl.multiple_of(j * BLOCK, BLOCK), BLOCK), :] = jnp.where(kpos_c < past, blk_ref[...], new_ref[...])

    @pl.when(j == n_sel - 1)
    def _():
        qpos = past
        n_keys = n_sel * BLOCK
        lane = lax.broadcasted_iota(jnp.int32, (1, n_keys), 1)
        blk_of = jnp.zeros((1, n_keys), jnp.int32)
        for jj in range(n_sel):
            blk_of = jnp.where(lane // BLOCK == jj, idx_ref[base + jj], blk_of)
        dist = qpos - (blk_of * BLOCK + lane % BLOCK)
        mask = dist >= 0
        rows = rows_ref[...]
        slope = jnp.concatenate([jnp.full((1, 1), slope_ref[kh * GROUP + g], _F32) for g in range(GROUP)], axis=0)
        s = _dot_t(q_ref[...].astype(_BF), rows[:, :HEAD_DIM].astype(_BF)) * SCALE - slope * dist.astype(_F32)
        p = _masked_softmax(s, mask)
        o_ref[...] = jnp.dot(p.astype(_BF), rows[:, HEAD_DIM:].astype(_BF), preferred_element_type=_F32)


def _sel_sample(q_s, cache_half, idx, page_table, new_rows, layer, n_pool, past, n_sel):
    db, n_pages = page_table.shape
    halves = PAGE_SIZE // BLOCK

    def blk_map(b, k, j, idx_ref, pt_ref, sl):
        blk = idx_ref[(b * N_KV_HEADS + k) * n_sel + j]
        pos = jnp.minimum(blk * BLOCK, past - 1)
        page = pt_ref[b * n_pages + pos // PAGE_SIZE]
        return ((layer * n_pool + page) * halves + (pos % PAGE_SIZE) // BLOCK, 0, k)

    grid_spec = pltpu.PrefetchScalarGridSpec(
        num_scalar_prefetch=3,
        grid=(db, N_KV_HEADS, n_sel),
        in_specs=[
            pl.BlockSpec((None, GROUP, HEAD_DIM), lambda b, k, j, *_: (b * N_KV_HEADS + k, 0, 0)),
            pl.BlockSpec((None, BLOCK, 2 * HEAD_DIM), blk_map),
            pl.BlockSpec((None, 1, 2 * HEAD_DIM), lambda b, k, j, *_: (b * N_KV_HEADS + k, 0, 0)),
        ],
        out_specs=pl.BlockSpec((None, GROUP, HEAD_DIM), lambda b, k, j, *_: (b * N_KV_HEADS + k, 0, 0)),
        scratch_shapes=[pltpu.VMEM((n_sel * BLOCK, 2 * HEAD_DIM), _F32)],
    )
    return pl.pallas_call(
        functools.partial(_sel_sample_kernel, past=past, n_sel=n_sel),
        grid_spec=grid_spec,
        out_shape=jax.ShapeDtypeStruct((db * N_KV_HEADS, GROUP, HEAD_DIM), _F32),
        compiler_params=_params(("arbitrary", "arbitrary", "arbitrary")),
        name="sel_sample",
    )(idx, page_table.reshape(-1), _alibi_slopes(), q_s, cache_half, new_rows)


def _win_combine_kernel(slope_ref, q_ref, win_ref, new_ref, oc_ref, os_ref, gate_ref, o_ref, *, past, w_buf):
    qpos = past
    q = q_ref[...]
    gate = _sigmoid(gate_ref[...])
    kpos = past - w_buf + lax.broadcasted_iota(jnp.int32, (1, w_buf), 1)
    dist = qpos - kpos
    mask = (dist >= 0) & (dist < WINDOW) & (kpos >= 0)
    dist_f = dist.astype(_F32)
    new = new_ref[...]
    mask_n = jnp.full((1, 1), 0 < WINDOW)
    for kh in range(N_KV_HEADS):
        kw = win_ref[:, (2 * kh) * HEAD_DIM:(2 * kh + 1) * HEAD_DIM].astype(_BF)
        vw = win_ref[:, (2 * kh + 1) * HEAD_DIM:(2 * kh + 2) * HEAD_DIM].astype(_BF)
        kn = new[:, (2 * kh) * HEAD_DIM:(2 * kh + 1) * HEAD_DIM]
        vn = new[:, (2 * kh + 1) * HEAD_DIM:(2 * kh + 2) * HEAD_DIM]
        hs = slice(kh * GROUP, (kh + 1) * GROUP)
        qk = q[hs]
        slope = jnp.concatenate([jnp.full((1, 1), slope_ref[kh * GROUP + g], _F32) for g in range(GROUP)], axis=0)
        s = _dot_t(qk.astype(_BF), kw) * SCALE - slope * dist_f
        qb = qk.astype(_BF).astype(_F32)
        s_n = jnp.sum(qb * kn.astype(_BF).astype(_F32), axis=-1, keepdims=True) * SCALE
        s = jnp.where(mask, s, -jnp.inf)
        s_n = jnp.where(mask_n, s_n, -jnp.inf)
        m = jnp.maximum(jnp.max(s, axis=-1, keepdims=True), s_n)
        m = jnp.where(jnp.isfinite(m), m, 0.0)
        p = jnp.where(mask, jnp.exp(s - m), 0.0)
        p_n = jnp.where(mask_n, jnp.exp(s_n - m), 0.0)
        inv = 1.0 / jnp.maximum(jnp.sum(p, axis=-1, keepdims=True) + p_n, 1e-30)
        o_w = (jnp.dot((p * inv).astype(_BF), vw, preferred_element_type=_F32)
               + (p_n * inv).astype(_BF).astype(_F32) * vn.astype(_BF).astype(_F32))
        g3 = gate[hs]
        o_ref[hs, :] = g3[:, 0:1] * oc_ref[hs, :] + g3[:, 1:2] * os_ref[hs, :] + g3[:, 2:3] * o_w


def _win_combine(q_s, win, new_rows, o_cmp, o_sel, gates, past):
    db, w_buf, width = win.shape
    head = pl.BlockSpec((None, N_HEADS, HEAD_DIM), lambda b, sl: (b, 0, 0))
    grid_spec = pltpu.PrefetchScalarGridSpec(
        num_scalar_prefetch=1,
        grid=(db,),
        in_specs=[
            head,
            pl.BlockSpec((None, w_buf, width), lambda b, sl: (b, 0, 0)),
            pl.BlockSpec((None, 1, width), lambda b, sl: (b, 0, 0)),
            head, head,
            pl.BlockSpec((None, N_HEADS, 3), lambda b, sl: (b, 0, 0)),
        ],
        out_specs=head,
    )
    return pl.pallas_call(
        functools.partial(_win_combine_kernel, past=past, w_buf=w_buf),
        grid_spec=grid_spec,
        out_shape=jax.ShapeDtypeStruct((db, N_HEADS, HEAD_DIM), _F32),
        compiler_params=_params(("arbitrary",)),
        name="win_combine",
    )(_alibi_slopes(), q_s, win, new_rows, o_cmp, o_sel, gates)


def _ssm_discretise(a_re, a_im, log_dt, b_re, b_im):
    dt = jnp.exp(log_dt)[:, None]
    mag = jnp.exp(dt * a_re)
    abar_re = mag * jnp.cos(dt * a_im)
    abar_im = mag * jnp.sin(dt * a_im)
    den = a_re * a_re + a_im * a_im
    cf_re = ((abar_re - 1.0) * a_re + abar_im * a_im) / den
    cf_im = (abar_im * a_re - (abar_re - 1.0) * a_im) / den
    bb_re = cf_re[..., None] * b_re - cf_im[..., None] * b_im
    bb_im = cf_re[..., None] * b_im + cf_im[..., None] * b_re
    return abar_re, abar_im, bb_re, bb_im


def _pair_diag(x):
    g, r, c = x.shape
    x = x.reshape(g // 2, 2, r, c)
    z = jnp.zeros((g // 2, r, c), x.dtype)
    top = jnp.concatenate([x[:, 0], z], axis=2)
    bot = jnp.concatenate([z, x[:, 1]], axis=2)
    return jnp.concatenate([top, bot], axis=1)


def _ssm_chunk_matrices(abar_re, abar_im, bb_re, bb_im, c_re, c_im):
    L = SSM_CHUNK
    g, p, ch = bb_re.shape
    pw_re = [jnp.ones_like(abar_re)]
    pw_im = [jnp.zeros_like(abar_im)]
    for _ in range(L):
        r, i = pw_re[-1], pw_im[-1]
        pw_re.append(r * abar_re - i * abar_im)
        pw_im.append(r * abar_im + i * abar_re)
    pw_re = jnp.stack(pw_re, axis=1)
    pw_im = jnp.stack(pw_im, axis=1)
    e_re = pw_re[:, :L, :, None] * bb_re[:, None] - pw_im[:, :L, :, None] * bb_im[:, None]
    e_im = pw_re[:, :L, :, None] * bb_im[:, None] + pw_im[:, :L, :, None] * bb_re[:, None]
    k_lag = (jnp.einsum("gcp,glpd->glcd", c_re, e_re, precision=_HI)
             - jnp.einsum("gcp,glpd->glcd", c_im, e_im, precision=_HI))
    jj = np.arange(L)[:, None]
    tt = np.arange(L)[None, :]
    lag = np.clip(tt - jj, 0, L - 1)
    m = k_lag[:, lag]
    m = jnp.where((tt >= jj)[None, :, :, None, None], m, 0.0)
    m = m.transpose(0, 1, 4, 2, 3).reshape(g, L * ch, L * ch)
    w_re = e_re[:, ::-1].transpose(0, 1, 3, 2).reshape(g, L * ch, p)
    w_im = e_im[:, ::-1].transpose(0, 1, 3, 2).reshape(g, L * ch, p)
    pr = pw_re[:, 1:, None, :]
    pi = pw_im[:, 1:, None, :]
    v_re = (c_re[:, None] * pr - c_im[:, None] * pi).transpose(0, 3, 1, 2).reshape(g, p, L * ch)
    v_im = (-(c_re[:, None] * pi + c_im[:, None] * pr)).transpose(0, 3, 1, 2).reshape(g, p, L * ch)
    al_re = pw_re[:, L].reshape(g // 2, 1, 2 * p)
    al_im = pw_im[:, L].reshape(g // 2, 1, 2 * p)
    return (_pair_diag(m).astype(_BF), _pair_diag(w_re).astype(_BF), _pair_diag(w_im).astype(_BF),
            _pair_diag(v_re).astype(_BF), _pair_diag(v_im).astype(_BF), al_re, al_im)


def _ssm_prompt_kernel(x_ref, m_ref, wre_ref, wim_ref, vre_ref, vim_ref, alre_ref, alim_ref, d_ref,
                       y_ref, hre_ref, him_ref, loc_re, loc_im, hp_re, hp_im, *, n_chunk, bp):
    x = x_ref[...]
    xb = x.astype(_BF)
    y = jnp.dot(xb, m_ref[...], preferred_element_type=_F32)
    loc_re[...] = jnp.dot(xb, wre_ref[...], preferred_element_type=_F32)
    loc_im[...] = jnp.dot(xb, wim_ref[...], preferred_element_type=_F32)
    a_re = alre_ref[...]
    a_im = alim_ref[...]

    def step(c, carry):
        h_re, h_im = carry
        r0 = pl.multiple_of(c * bp, SUBLANE)
        hp_re[pl.ds(r0, bp), :] = h_re
        hp_im[pl.ds(r0, bp), :] = h_im
        l_re = loc_re[pl.ds(r0, bp), :]
        l_im = loc_im[pl.ds(r0, bp), :]
        return (a_re * h_re - a_im * h_im + l_re, a_re * h_im + a_im * h_re + l_im)

    zero = jnp.zeros((bp, a_re.shape[1]), _F32)
    h_re, h_im = lax.fori_loop(0, n_chunk, step, (zero, zero))
    y = y + jnp.dot(hp_re[...].astype(_BF), vre_ref[...], preferred_element_type=_F32)
    y = y + jnp.dot(hp_im[...].astype(_BF), vim_ref[...], preferred_element_type=_F32)
    y_ref[...] = (y + d_ref[...] * x).astype(y_ref.dtype)
    hre_ref[...] = h_re
    him_ref[...] = h_im


def _ssm_prompt(u, mats, d_skip, batch, seq):
    m2, wre, wim, vre, vim, al_re, al_im = mats
    L = SSM_CHUNK
    g2 = m2.shape[0]
    n_chunk = seq // L
    bp = -(-batch // SUBLANE) * SUBLANE
    width = 2 * L * GROUP_CH
    x = u.reshape(batch, n_chunk, L, g2, 2, GROUP_CH).transpose(3, 1, 0, 4, 2, 5)
    x = jnp.pad(x, ((0, 0), (0, 0), (0, bp - batch), (0, 0), (0, 0), (0, 0))).reshape(g2, n_chunk * bp, width)
    d2 = jnp.broadcast_to(d_skip.reshape(g2, 2, 1, GROUP_CH), (g2, 2, L, GROUP_CH)).reshape(g2, 1, width)
    rows = n_chunk * bp
    pp = 2 * P_STATE
    per_g = lambda shape: pl.BlockSpec((None,) + shape, lambda g: (g, 0, 0))
    y, h_re, h_im = pl.pallas_call(
        functools.partial(_ssm_prompt_kernel, n_chunk=n_chunk, bp=bp),
        grid=(g2,),
        in_specs=[per_g((rows, width)), per_g((width, width)), per_g((width, pp)), per_g((width, pp)),
                  per_g((pp, width)), per_g((pp, width)), per_g((1, pp)), per_g((1, pp)), per_g((1, width))],
        out_specs=[per_g((rows, width)), per_g((bp, pp)), per_g((bp, pp))],
        out_shape=[jax.ShapeDtypeStruct((g2, rows, width), _BF),
                   jax.ShapeDtypeStruct((g2, bp, pp), _F32), jax.ShapeDtypeStruct((g2, bp, pp), _F32)],
        scratch_shapes=[pltpu.VMEM((rows, pp), _F32) for _ in range(4)],
        compiler_params=_params(("arbitrary",)),
        name="ssm_prompt",
    )(x, m2, wre, wim, vre, vim, al_re, al_im, d2)
    y = y.reshape(g2, n_chunk, bp, 2, L, GROUP_CH)[:, :, :batch].transpose(2, 1, 4, 0, 3, 5)
    y = y.reshape(batch * seq, g2 * 2 * GROUP_CH)
    fin = lambda h: h[:, :batch].reshape(g2, batch, 2, P_STATE).transpose(1, 0, 2, 3).reshape(batch, 2 * g2, P_STATE)
    return y, jnp.stack([fin(h_re), fin(h_im)], axis=-1)


def _ssm_sample_kernel(xe_ref, bre_ref, bim_ref, h0r_ref, h0i_ref, ar_ref, ai_ref, cre_ref, cim_ref, seg_ref,
                       xt_ref, d_ref, y_ref, sre_ref, sim_ref):
    ch = xe_ref.shape[0]
    h0r, h0i = h0r_ref[...], h0i_ref[...]
    a_re, a_im = ar_ref[...], ai_ref[...]
    rnd = lambda v: v.astype(_BF).astype(_F32)
    bu_re = rnd(xe_ref[0]) * rnd(bre_ref[0])
    bu_im = rnd(xe_ref[0]) * rnd(bim_ref[0])
    for c in range(1, ch):
        bu_re = bu_re + rnd(xe_ref[c]) * rnd(bre_ref[c])
        bu_im = bu_im + rnd(xe_ref[c]) * rnd(bim_ref[c])
    s_re = bu_re + (a_re * h0r - a_im * h0i)
    s_im = bu_im + (a_re * h0i + a_im * h0r)
    sre_ref[...] = s_re
    sim_ref[...] = s_im
    seg = seg_ref[...]
    sb_re, sb_im = rnd(s_re), rnd(s_im)
    for c in range(ch):
        y_re = jnp.dot(sb_re * rnd(cre_ref[c]), seg, precision=_HI, preferred_element_type=_F32)
        y_im = jnp.dot(sb_im * rnd(cim_ref[c]), seg, precision=_HI, preferred_element_type=_F32)
        y_ref[c] = y_re - y_im + d_ref[c] * xt_ref[c]


def _ssm_sample(u_s, h0, abar_re, abar_im, bb_re, bb_im, c_re, c_im, d_skip):
    db = u_s.shape[0]
    g, p, ch = bb_re.shape
    gp = g * p
    x3 = u_s.reshape(db, g, ch)
    xe = jnp.broadcast_to(x3.transpose(2, 0, 1)[..., None], (ch, db, g, p)).reshape(ch, db, gp)
    flat_b = lambda b: b.transpose(2, 0, 1).reshape(ch, 1, gp)
    flat_c = lambda c: c.transpose(1, 0, 2).reshape(ch, 1, gp)
    seg = jnp.asarray(np.repeat(np.eye(g, dtype=np.float32), p, axis=0))
    xt = x3.transpose(2, 0, 1)
    dt = d_skip.transpose(1, 0).reshape(ch, 1, g)
    y, s_re, s_im = pl.pallas_call(
        _ssm_sample_kernel,
        out_shape=[jax.ShapeDtypeStruct((ch, db, g), _F32), jax.ShapeDtypeStruct((db, gp), _F32),
                   jax.ShapeDtypeStruct((db, gp), _F32)],
        compiler_params=pltpu.CompilerParams(vmem_limit_bytes=VMEM_LIMIT),
        name="ssm_sample",
    )(xe, flat_b(bb_re), flat_b(bb_im), h0[..., 0].reshape(db, gp), h0[..., 1].reshape(db, gp),
      abar_re.reshape(1, gp), abar_im.reshape(1, gp), flat_c(c_re), flat_c(c_im), seg, xt, dt)
    y = y.transpose(1, 2, 0).reshape(db, g * ch)
    h_fin = jnp.stack([s_re.reshape(db, g, p), s_im.reshape(db, g, p)], axis=-1)
    return y, h_fin


def _router_kernel(x_ref, g_ref, wr_ref, idx_ref, gate_ref):
    x = x_ref[...]
    h = x * lax.rsqrt(jnp.mean(x * x, axis=-1, keepdims=True) + EPS) * g_ref[...]
    logits = jnp.dot(h.astype(_BF), wr_ref[...].astype(_BF), preferred_element_type=_F32)
    lane = lax.broadcasted_iota(jnp.int32, logits.shape, 1)
    lane_f = lane.astype(_F32)
    lg = jnp.where(lane < N_EXPERTS, logits, -jnp.inf)
    v1 = jnp.max(lg, axis=-1, keepdims=True)
    i1 = jnp.min(jnp.where(lg == v1, lane_f, float(LANE)), axis=-1, keepdims=True)
    lg2 = jnp.where(lane_f == i1, -jnp.inf, lg)
    v2 = jnp.max(lg2, axis=-1, keepdims=True)
    i2 = jnp.min(jnp.where(lg2 == v2, lane_f, float(LANE)), axis=-1, keepdims=True)
    e = jnp.exp(v2 - v1)
    den = 1.0 + e
    idx_ref[...] = jnp.where(lane == 0, i1, jnp.where(lane == 1, i2, 0.0)).astype(jnp.int32)
    gate_ref[...] = jnp.where(lane == 0, 1.0 / den, jnp.where(lane == 1, e / den, 0.0))


def _router(x, g, wr_pad):
    n, d = x.shape
    tm = _tile(n, 256)
    return pl.pallas_call(
        _router_kernel,
        grid=(n // tm,),
        in_specs=[pl.BlockSpec((tm, d), lambda i: (i, 0)), pl.BlockSpec((1, d), lambda i: (0, 0)),
                  pl.BlockSpec((d, LANE), lambda i: (0, 0))],
        out_specs=[pl.BlockSpec((tm, LANE), lambda i: (i, 0)), pl.BlockSpec((tm, LANE), lambda i: (i, 0))],
        out_shape=[jax.ShapeDtypeStruct((n, LANE), jnp.int32), jax.ShapeDtypeStruct((n, LANE), _F32)],
        compiler_params=_params(("arbitrary",)),
        name="router",
    )(x, g.reshape(1, d), wr_pad)


def _row_copy(src_hbm, row, dst, slot, sem):
    return pltpu.make_async_copy(src_hbm.at[pl.ds(row, 1)], dst.at[pl.ds(slot, 1)], sem)


def _moe_gather_kernel(tok_ref, x_hbm, g_ref, o_ref, buf, sem, *, tm):
    def start(r, c):
        _row_copy(x_hbm, tok_ref[0, 0, r], buf, r, sem).start()
        return c

    lax.fori_loop(0, tm, start, 0)

    def wait(r, c):
        _row_copy(x_hbm, 0, buf, r, sem).wait()
        return c

    lax.fori_loop(0, tm, wait, 0)
    x = buf[...]
    inv = lax.rsqrt(jnp.mean(x * x, axis=-1, keepdims=True) + EPS)
    o_ref[...] = (x * inv * g_ref[...]).astype(o_ref.dtype)


def _moe_gather(x_all, g, row_tok, tm):
    n_rows = row_tok.shape[0]
    d = x_all.shape[1]
    n_tiles = n_rows // tm
    return pl.pallas_call(
        functools.partial(_moe_gather_kernel, tm=tm),
        grid=(n_tiles,),
        in_specs=[pl.BlockSpec((1, 1, tm), lambda i: (i, 0, 0), memory_space=pltpu.SMEM),
                  pl.BlockSpec(memory_space=pl.ANY),
                  pl.BlockSpec((1, d), lambda i: (0, 0))],
        out_specs=pl.BlockSpec((tm, d), lambda i: (i, 0)),
        out_shape=jax.ShapeDtypeStruct((n_rows, d), _BF),
        scratch_shapes=[pltpu.VMEM((tm, d), _F32), pltpu.SemaphoreType.DMA(())],
        compiler_params=_params(("arbitrary",)),
        name="moe_gather",
    )(row_tok.reshape(n_tiles, 1, tm), x_all, g.reshape(1, d))


def _moe_up_kernel(texp_ref, nused_ref, x_ref, w1_ref, w3_ref, o_ref, w1b, w3b):
    i = pl.program_id(1)
    n_used = nused_ref[0]
    prev = texp_ref[jnp.maximum(i - 1, 0)]
    fresh = (i == 0) | (texp_ref[i] != prev)

    @pl.when((i < n_used) & fresh)
    def _():
        w1b[...] = w1_ref[...].astype(_BF)
        w3b[...] = w3_ref[...].astype(_BF)

    @pl.when(i < n_used)
    def _():
        x = x_ref[...]
        a = jnp.dot(x, w1b[...], preferred_element_type=_F32)
        b = jnp.dot(x, w3b[...], preferred_element_type=_F32)
        o_ref[...] = ((a * _sigmoid(a)) * b).astype(o_ref.dtype)

    @pl.when(i >= n_used)
    def _():
        o_ref[...] = jnp.zeros(o_ref.shape, o_ref.dtype)


def _moe_up(xg, w1, w3, layer, tile_exp, n_used, *, tm, tf):
    n_rows, d = xg.shape
    f = w1.shape[3]
    n_tiles = n_rows // tm

    def live(i, nu):
        return jnp.minimum(i, nu[0] - 1)

    grid_spec = pltpu.PrefetchScalarGridSpec(
        num_scalar_prefetch=2,
        grid=(f // tf, n_tiles),
        in_specs=[
            pl.BlockSpec((tm, d), lambda j, i, te, nu: (live(i, nu), 0)),
            pl.BlockSpec((None, None, d, tf), lambda j, i, te, nu: (layer, te[live(i, nu)], 0, j)),
            pl.BlockSpec((None, None, d, tf), lambda j, i, te, nu: (layer, te[live(i, nu)], 0, j)),
        ],
        out_specs=pl.BlockSpec((tm, tf), lambda j, i, te, nu: (i, j)),
        scratch_shapes=[pltpu.VMEM((d, tf), _BF), pltpu.VMEM((d, tf), _BF)],
    )
    return pl.pallas_call(
        _moe_up_kernel,
        grid_spec=grid_spec,
        out_shape=jax.ShapeDtypeStruct((n_rows, f), _BF),
        compiler_params=_params(("arbitrary", "arbitrary")),
        name="moe_up",
    )(tile_exp, n_used, xg, w1, w3)


def _moe_down_kernel(texp_ref, nused_ref, a_ref, w_ref, o_ref):
    i = pl.program_id(0)
    k = pl.program_id(1)

    @pl.when(i < nused_ref[0])
    def _():
        part = jnp.dot(a_ref[...], w_ref[...].astype(_BF), preferred_element_type=_F32)

        @pl.when(k == 0)
        def _():
            o_ref[...] = part

        @pl.when(k > 0)
        def _():
            o_ref[...] += part

    @pl.when((i >= nused_ref[0]) & (k == 0))
    def _():
        o_ref[...] = jnp.zeros(o_ref.shape, o_ref.dtype)


def _moe_down(a, w2, layer, tile_exp, n_used, *, tm, tk):
    n_rows, f = a.shape
    d = w2.shape[3]
    n_tiles = n_rows // tm
    nk = f // tk

    def live(i, nu):
        return jnp.minimum(i, nu[0] - 1)

    def kk(i, k, nu):
        return jnp.where(i < nu[0], k, nk - 1)

    grid_spec = pltpu.PrefetchScalarGridSpec(
        num_scalar_prefetch=2,
        grid=(n_tiles, nk),
        in_specs=[
            pl.BlockSpec((tm, tk), lambda i, k, te, nu: (live(i, nu), kk(i, k, nu))),
            pl.BlockSpec((None, None, tk, d), lambda i, k, te, nu: (layer, te[live(i, nu)], kk(i, k, nu), 0)),
        ],
        out_specs=pl.BlockSpec((tm, d), lambda i, k, te, nu: (i, 0)),
    )
    return pl.pallas_call(
        _moe_down_kernel,
        grid_spec=grid_spec,
        out_shape=jax.ShapeDtypeStruct((n_rows, d), _F32),
        compiler_params=_params(("arbitrary", "arbitrary")),
        name="moe_down",
    )(tile_exp, n_used, a, w2)


def _moe_combine_kernel(pos_ref, x_ref, gate_ref, y_hbm, o_ref, b0, b1, sems, *, tm):
    def start(r, c):
        _row_copy(y_hbm, pos_ref[0, 0, 2 * r], b0, r, sems.at[0]).start()
        _row_copy(y_hbm, pos_ref[0, 0, 2 * r + 1], b1, r, sems.at[1]).start()
        return c

    lax.fori_loop(0, tm, start, 0)

    def wait(r, c):
        _row_copy(y_hbm, 0, b0, r, sems.at[0]).wait()
        _row_copy(y_hbm, 0, b1, r, sems.at[1]).wait()
        return c

    lax.fori_loop(0, tm, wait, 0)
    gate = gate_ref[...]
    o_ref[...] = x_ref[...] + (gate[:, 0:1] * b0[...] + gate[:, 1:2] * b1[...])


def _moe_combine(x, gates, pos, y_rows):
    n, d = x.shape
    tm = _tile(n, 256)
    n_tiles = n // tm
    return pl.pallas_call(
        functools.partial(_moe_combine_kernel, tm=tm),
        grid=(n_tiles,),
        in_specs=[pl.BlockSpec((1, 1, 2 * tm), lambda i: (i, 0, 0), memory_space=pltpu.SMEM),
                  pl.BlockSpec((tm, d), lambda i: (i, 0)),
                  pl.BlockSpec((tm, LANE), lambda i: (i, 0)),
                  pl.BlockSpec(memory_space=pl.ANY)],
        out_specs=pl.BlockSpec((tm, d), lambda i: (i, 0)),
        out_shape=jax.ShapeDtypeStruct((n, d), _F32),
        scratch_shapes=[pltpu.VMEM((tm, d), _F32), pltpu.VMEM((tm, d), _F32), pltpu.SemaphoreType.DMA((2,))],
        compiler_params=_params(("arbitrary",)),
        name="moe_combine",
    )(pos.reshape(n_tiles, 1, 2 * tm), x, gates, y_rows)


def _moe_layer(xp, xs, g, w_router, w1, w3, w2, layer):
    n_p, d = xp.shape
    n_s = xs.shape[0]
    n = n_p + n_s
    wr_pad = jnp.pad(w_router, ((0, 0), (0, LANE - N_EXPERTS)))
    idx_p, gate_p = _router(xp, g, wr_pad)
    idx_s, gate_s = _router(xs, g, wr_pad)
    top_i = jnp.concatenate([idx_p[:, :TOP_K], idx_s[:, :TOP_K]], axis=0)
    n_asg = n * TOP_K
    tm = 512 if n_asg >= 4096 else 64
    n_tiles = -(-n_asg // tm) + N_EXPERTS
    e_flat = top_i.reshape(n_asg)
    onehot = (e_flat[:, None] == jnp.arange(N_EXPERTS)[None, :]).astype(jnp.int32)
    within = jnp.sum((jnp.cumsum(onehot, axis=0) - onehot) * onehot, axis=1)
    counts = jnp.sum(onehot, axis=0)
    padded = (counts + tm - 1) // tm * tm
    pad_end = jnp.cumsum(padded)
    dest = (pad_end - padded)[e_flat] + within
    row_tok = jnp.zeros((n_tiles * tm,), jnp.int32).at[dest].set(jnp.arange(n_asg, dtype=jnp.int32) // TOP_K)
    tile_exp = jnp.minimum(jnp.searchsorted(pad_end, jnp.arange(n_tiles) * tm, side="right"),
                           N_EXPERTS - 1).astype(jnp.int32)
    n_used = (pad_end[-1:] // tm).astype(jnp.int32)
    pos = dest.reshape(n, TOP_K).astype(jnp.int32)

    x_all = jnp.concatenate([xp, xs], axis=0)
    xg = _moe_gather(x_all, g, row_tok, tm)
    f = w1.shape[3]
    a = _moe_up(xg, w1, w3, layer, tile_exp, n_used, tm=tm, tf=_tile(f, 512, LANE))
    y_rows = _moe_down(a, w2, layer, tile_exp, n_used, tm=tm, tk=_tile(f, 512, LANE))
    out_p = _moe_combine(xp, gate_p, pos[:n_p], y_rows)
    out_s = _moe_combine(xs, gate_s, pos[n_p:], y_rows)
    return out_p, out_s


def kernel(x_prompt, x_sample, cache_kv_cmp, cache_kv_sel, cache_kv_win, state_ssm, page_table, norm_mix, norm_ffn, w_in, g_q, g_k, w_cmp, w_attn_out, ssm_a_re, ssm_a_im, ssm_log_dt, ssm_b_re, ssm_b_im, ssm_c_re, ssm_c_im, ssm_d, w_glu, w_out, ffn_w1, ffn_w3, ffn_w2, moe_router, moe_w1, moe_w3, moe_w2):
    batch, seq, d_model = x_prompt.shape
    db, dec_seq, _ = x_sample.shape
    assert dec_seq == 1, "the sample group advances one token per step"
    depth = w_in.shape[0]
    n_pool = cache_kv_cmp.shape[1]
    n_pages = page_table.shape[1]
    past = n_pages * PAGE_SIZE
    d_attn = N_HEADS * HEAD_DIM
    d_kv = N_KV_HEADS * HEAD_DIM
    d_ssm = ssm_d.shape[1] * GROUP_CH
    n_gate = 3 * N_HEADS
    off_gn = d_attn + 6 * d_kv
    off_u = off_gn + n_gate
    n_ug = d_ssm + 2 * d_model
    row_w = N_KV_HEADS * 2 * HEAD_DIM
    nb = seq // BLOCK
    n_p = batch * seq

    tm = _tile(n_p, 1024)
    tn = 512
    assert off_gn % tn == 0 and n_ug % tn == 0 and d_ssm % tn == 0 and d_model % tn == 0

    xp = x_prompt.reshape(n_p, d_model)
    xs = x_sample.reshape(db, d_model)
    w_ug = w_in[:, :, off_u:]
    flag_cols = np.zeros((1, off_gn), np.float32)
    flag_cols[:, :d_attn] = 1.0
    for br in range(3):
        lo = d_attn + br * 2 * d_kv
        flag_cols[:, lo:lo + d_kv] = 1.0
    flag_cols = jnp.asarray(flag_cols)
    expand = jnp.asarray(np.repeat(np.eye(nb, dtype=np.float32), BLOCK, axis=1)).astype(_BF)
    eye_nb = jnp.asarray(np.eye(nb, dtype=np.float32))
    n_sel = min(N_SELECT, past // BLOCK + 1)

    outs = {k: [] for k in ("cmp_p", "cmp_s", "sel_p", "sel_s", "win_p", "win_s", "ssm_p", "ssm_s")}
    for l in range(depth):
        hp = _rmsnorm(xp, norm_mix[l])
        hs = _rmsnorm(xs, norm_mix[l])
        gain = jnp.concatenate([jnp.tile(g_q[l], N_HEADS)]
                               + [jnp.concatenate([jnp.tile(g_k[l, br], N_KV_HEADS), jnp.ones((d_kv,), _F32)])
                                  for br in range(3)]).reshape(1, off_gn)
        zqkv, zqkv_s = _mm(hp, hs, [(w_in, l, 0)], off_gn, mode="headnorm", out_dtype=_F32, tm=tm, tn=tn,
                           gain=gain, flag=flag_cols, name="in_proj_qkv")
        zg, zg_s = _mm(hp, hs, [(w_in, l, off_gn // LANE)], LANE, mode="plain", out_dtype=_F32, tm=tm, tn=LANE,
                       name="in_proj_gate")
        zug, zug_s = _mm(hp, hs, [(w_ug, l, 0)], n_ug, mode="plain", out_dtype=_F32, tm=tm, tn=tn,
                         name="in_proj_ug")

        def kv_rows(z, lead):
            r = z[:, d_attn:off_gn].reshape(lead + (3, 2, N_KV_HEADS, HEAD_DIM))
            return jnp.swapaxes(r, -3, -2)

        rows_p = kv_rows(zqkv, (batch, seq))
        rows_s = kv_rows(zqkv_s, (db, 1))
        outs["cmp_p"].append(rows_p[:, :, 0])
        outs["sel_p"].append(rows_p[:, :, 1])
        outs["win_p"].append(rows_p[:, seq - min(WINDOW, seq):, 2])
        outs["cmp_s"].append(rows_s[:, :, 0])
        outs["sel_s"].append(rows_s[:, :, 1])

        gates_r = zg[:, :n_gate].reshape(n_p, 3, N_KV_HEADS, GROUP).transpose(2, 0, 1, 3).reshape(
            N_KV_HEADS, n_p, 3 * GROUP)
        wk = w_cmp[l].transpose(0, 2, 1)
        pool = (eye_nb[None, None, :, :, None] * wk[:, :, None, None, :]).reshape(2, N_KV_HEADS, nb, seq)
        o_attn = _nsa_prompt(zqkv, gates_r, pool[0], pool[1], expand, batch, seq)

        q_s = zqkv_s[:, :d_attn].reshape(db, N_HEADS, HEAD_DIM)
        w_exp = jnp.broadcast_to(w_cmp[l].transpose(1, 2, 0)[..., None],
                                 (BLOCK, N_KV_HEADS, 2, HEAD_DIM)).reshape(BLOCK, row_w)
        pooled = _pool_pages(cache_kv_cmp.reshape(depth * n_pool, PAGE_SIZE, row_w), page_table, w_exp, l, n_pool)
        pooled = pooled.reshape(db, past // BLOCK, row_w)
        new_c = rows_s[:, :, 0].reshape(db, 1, row_w)
        o_cmp_s, idx = _cmp_select(q_s, pooled, new_c, w_exp[:1], past, n_sel)
        idx = idx[..., 0].reshape(-1)
        new_sel = rows_s[:, 0, 1].reshape(db * N_KV_HEADS, 1, 2 * HEAD_DIM)
        o_sel_s = _sel_sample(q_s.reshape(db * N_KV_HEADS, GROUP, HEAD_DIM),
                              cache_kv_sel.reshape(depth * n_pool * (PAGE_SIZE // BLOCK), BLOCK, row_w),
                              idx, page_table, new_sel, l, n_pool, past, n_sel)
        win = cache_kv_win[l]
        w_buf = win.shape[1]
        new_w = rows_s[:, :, 2]
        gates_s = zg_s[:, :n_gate].reshape(db, 3, N_HEADS).transpose(0, 2, 1)
        o_attn_s = _win_combine(q_s, win.reshape(db, w_buf, row_w), new_w.reshape(db, 1, row_w),
                                o_cmp_s, o_sel_s.reshape(db, N_HEADS, HEAD_DIM), gates_s, past)
        outs["win_s"].append(jnp.concatenate([win, new_w], axis=1)[:, 1:])

        abar_re, abar_im, bb_re, bb_im = _ssm_discretise(ssm_a_re[l], ssm_a_im[l], ssm_log_dt[l],
                                                         ssm_b_re[l], ssm_b_im[l])
        mats = _ssm_chunk_matrices(abar_re, abar_im, bb_re, bb_im, ssm_c_re[l], ssm_c_im[l])
        y_p, h_p = _ssm_prompt(zug[:, :d_ssm], mats, ssm_d[l], batch, seq)
        y_s, h_s = _ssm_sample(zug_s[:, :d_ssm], state_ssm[l], abar_re, abar_im, bb_re, bb_im,
                               ssm_c_re[l], ssm_c_im[l], ssm_d[l])
        outs["ssm_p"].append(h_p)
        outs["ssm_s"].append(h_s)

        mg, mg_s = _merge(o_attn, o_attn_s.reshape(db, d_attn).astype(_BF), y_p, y_s.astype(_BF),
                          w_attn_out, w_glu, l, zug, zug_s, d_model, d_ssm, tm=_tile(n_p, 512), tn=tn)
        xp, xs = _mm(mg, mg_s, [(w_out, l, 0)], d_model, mode="resid", out_dtype=_F32, tm=tm, tn=tn,
                     res=xp, res_s=xs, name="out_proj")

        if l % 2 == 0:
            j = l // 2
            hp = _rmsnorm(xp, norm_ffn[l])
            hs = _rmsnorm(xs, norm_ffn[l])
            f = ffn_w1.shape[2]
            a, a_s = _mm(hp, hs, [(ffn_w1, j, 0), (ffn_w3, j, 0)], f, mode="swiglu", out_dtype=_BF,
                         tm=_tile(n_p, 512), tn=_tile(f, 512, LANE), name="ffn_up")
            xp, xs = _ffn_down(a, a_s, ffn_w2, j, xp, xs, tm=_tile(n_p, 512), tk=_tile(f, 256, LANE))
        else:
            j = l // 2
            xp, xs = _moe_layer(xp, xs, norm_ffn[l], moe_router[j], moe_w1, moe_w3, moe_w2, j)

    st = lambda k: jnp.stack(outs[k])
    return (xp.reshape(batch, seq, d_model), xs.reshape(db, 1, d_model), st("cmp_p"), st("cmp_s"), st("sel_p"),
            st("sel_s"), st("win_p"), st("win_s"), st("ssm_p"), st("ssm_s"))
```

```python
import functools
import math

import numpy as np
import jax
import jax.numpy as jnp
from jax import lax
from jax.experimental import pallas as pl
from jax.experimental.pallas import tpu as pltpu

HEAD_DIM = 128
N_HEADS = 16
N_KV_HEADS = 4
GROUP = N_HEADS // N_KV_HEADS
BLOCK = 64
N_SELECT = 16
WINDOW = 512
PAGE_SIZE = 128
GROUP_CH = 16
P_STATE = 64
N_EXPERTS = 8
TOP_K = 2
EPS = 1.0e-6
FORCED_SCORE = 1.0e6
ALIBI_MAX_EXP = 8.0
SCALE = HEAD_DIM ** -0.5
SSM_CHUNK = 16
LANE = 128
SUBLANE = 8
VMEM_LIMIT = 56 << 20

_BF = jnp.bfloat16
_F32 = jnp.float32
_HI = lax.Precision.HIGHEST


def _tile(n, pref, mult=SUBLANE):
    if n <= pref:
        return n
    for t in range(pref - pref % mult, 0, -mult):
        if n % t == 0:
            return t
    return n


def _params(sem):
    return pltpu.CompilerParams(dimension_semantics=sem, vmem_limit_bytes=VMEM_LIMIT)


def _rms_kernel(x_ref, g_ref, o_ref):
    x = x_ref[...]
    inv = lax.rsqrt(jnp.mean(x * x, axis=-1, keepdims=True) + EPS)
    o_ref[...] = (x * inv * g_ref[...]).astype(o_ref.dtype)


def _rmsnorm(x, g, out_dtype=_BF):
    m, d = x.shape
    tm = _tile(m, 256)
    return pl.pallas_call(
        _rms_kernel,
        grid=(m // tm,),
        in_specs=[pl.BlockSpec((tm, d), lambda i: (i, 0)), pl.BlockSpec((1, d), lambda i: (0, 0))],
        out_specs=pl.BlockSpec((tm, d), lambda i: (i, 0)),
        out_shape=jax.ShapeDtypeStruct((m, d), out_dtype),
        compiler_params=_params(("arbitrary",)),
        name="rmsnorm",
    )(x, g.reshape(1, d))


def _sigmoid(x):
    return 1.0 / (1.0 + jnp.exp(-x))


def _mm_kernel(*refs, n_w, mode):
    x_ref, xs_ref = refs[0], refs[1]
    w_refs = refs[2:2 + n_w]
    p = 2 + n_w
    extra = ()
    if mode in ("headnorm", "resid"):
        extra = refs[p:p + 2]
        p += 2
    o_ref, os_ref = refs[p], refs[p + 1]
    wb_refs = refs[p + 2:p + 2 + n_w]
    i = pl.program_id(1)

    @pl.when(i == 0)
    def _():
        for w_ref, wb_ref in zip(w_refs, wb_refs):
            wb_ref[...] = w_ref[...].astype(_BF)

    def compute(x, out_ref, res):
        accs = [jnp.dot(x, wb[...], preferred_element_type=_F32) for wb in wb_refs]
        if mode == "plain":
            out_ref[...] = accs[0].astype(out_ref.dtype)
        elif mode == "swiglu":
            a = accs[0]
            out_ref[...] = ((a * _sigmoid(a)) * accs[1]).astype(out_ref.dtype)
        elif mode == "resid":
            out_ref[...] = (res + accs[0]).astype(out_ref.dtype)
        else:
            acc = accs[0]
            gain_ref, flag_ref = extra
            for c in range(acc.shape[1] // HEAD_DIM):
                sl = slice(c * HEAD_DIM, (c + 1) * HEAD_DIM)
                a = acc[:, sl]
                nrm = a * lax.rsqrt(jnp.mean(a * a, axis=-1, keepdims=True) + EPS) * gain_ref[:, sl]
                out_ref[:, sl] = jnp.where(flag_ref[:, sl] > 0.0, nrm, a).astype(out_ref.dtype)

    compute(x_ref[...], o_ref, extra[0][...] if mode == "resid" else None)

    @pl.when(i == 0)
    def _():
        compute(xs_ref[...], os_ref, extra[1][...] if mode == "resid" else None)


def _mm(x, xs, ws, n_cols, *, mode, out_dtype, tm, tn, gain=None, flag=None, res=None, res_s=None,
        name="mm"):
    m, k = x.shape
    ms = xs.shape[0]
    nj, ni = n_cols // tn, m // tm
    in_specs = [pl.BlockSpec((tm, k), lambda j, i: (i, 0)), pl.BlockSpec((ms, k), lambda j, i: (0, 0))]
    args = [x, xs]
    for w, lid, off in ws:
        in_specs.append(pl.BlockSpec((None, k, tn), functools.partial(lambda j, i, lid, off: (lid, 0, j + off),
                                                                       lid=lid, off=off)))
        args.append(w)
    if mode == "headnorm":
        in_specs += [pl.BlockSpec((1, tn), lambda j, i: (0, j)), pl.BlockSpec((1, tn), lambda j, i: (0, j))]
        args += [gain, flag]
    elif mode == "resid":
        in_specs += [pl.BlockSpec((tm, tn), lambda j, i: (i, j)), pl.BlockSpec((ms, tn), lambda j, i: (0, j))]
        args += [res, res_s]
    return pl.pallas_call(
        functools.partial(_mm_kernel, n_w=len(ws), mode=mode),
        grid=(nj, ni),
        in_specs=in_specs,
        out_specs=[pl.BlockSpec((tm, tn), lambda j, i: (i, j)), pl.BlockSpec((ms, tn), lambda j, i: (0, j))],
        out_shape=[jax.ShapeDtypeStruct((m, n_cols), out_dtype), jax.ShapeDtypeStruct((ms, n_cols), out_dtype)],
        scratch_shapes=[pltpu.VMEM((k, tn), _BF) for _ in ws],
        compiler_params=_params(("arbitrary", "arbitrary")),
        name=name,
    )(*args)


def _mm_shift_kernel(x_ref, xs_ref, wlo_ref, whi_ref, o_ref, os_ref, wb_ref, *, shift, kc):
    i = pl.program_id(1)

    @pl.when(i == 0)
    def _():
        for c in range(0, wb_ref.shape[0], kc):
            sl = slice(c, c + kc)
            wb_ref[sl, :] = jnp.concatenate([wlo_ref[sl, shift:], whi_ref[sl, :shift]], axis=1).astype(_BF)

    o_ref[...] = jnp.dot(x_ref[...], wb_ref[...], preferred_element_type=_F32).astype(o_ref.dtype)

    @pl.when(i == 0)
    def _():
        os_ref[...] = jnp.dot(xs_ref[...], wb_ref[...], preferred_element_type=_F32).astype(os_ref.dtype)


def _mm_shifted(x, xs, w, layer, col0, n_cols, *, tm, tn, name):
    m, k = x.shape
    ms = xs.shape[0]
    off, shift = divmod(col0, tn)
    assert 0 < shift < tn
    return pl.pallas_call(
        functools.partial(_mm_shift_kernel, shift=shift, kc=_tile(k, 512)),
        grid=(n_cols // tn, m // tm),
        in_specs=[pl.BlockSpec((tm, k), lambda j, i: (i, 0)), pl.BlockSpec((ms, k), lambda j, i: (0, 0)),
                  pl.BlockSpec((None, k, tn), lambda j, i: (layer, 0, j + off)),
                  pl.BlockSpec((None, k, tn), lambda j, i: (layer, 0, j + off + 1))],
        out_specs=[pl.BlockSpec((tm, tn), lambda j, i: (i, j)), pl.BlockSpec((ms, tn), lambda j, i: (0, j))],
        out_shape=[jax.ShapeDtypeStruct((m, n_cols), _F32), jax.ShapeDtypeStruct((ms, n_cols), _F32)],
        scratch_shapes=[pltpu.VMEM((k, tn), _BF)],
        compiler_params=_params(("arbitrary", "arbitrary")),
        name=name,
    )(x, xs, w, w)


def _merge_kernel(oa_ref, oas_ref, y_ref, ys_ref, wa_ref, w1_ref, w2_ref, ga_ref, gb_ref, gas_ref, gbs_ref,
                  o_ref, os_ref, wab, w1b, w2b):
    i = pl.program_id(1)

    @pl.when(i == 0)
    def _():
        wab[...] = wa_ref[...].astype(_BF)
        w1b[...] = w1_ref[...].astype(_BF)
        w2b[...] = w2_ref[...].astype(_BF)

    def compute(oa, y, ga, gb, out_ref):
        pa = jnp.dot(oa, wab[...], preferred_element_type=_F32)
        ab1 = jnp.dot(y, w1b[...], preferred_element_type=_F32)
        ab2 = jnp.dot(y, w2b[...], preferred_element_type=_F32)
        pb = ab1 * _sigmoid(ab2)
        out_ref[...] = (_sigmoid(ga) * pa + _sigmoid(gb) * pb).astype(out_ref.dtype)

    compute(oa_ref[...], y_ref[...], ga_ref[...], gb_ref[...], o_ref)

    @pl.when(i == 0)
    def _():
        compute(oas_ref[...], ys_ref[...], gas_ref[...], gbs_ref[...], os_ref)


def _merge(oa, oas, y, ys, w_attn_out, w_glu, layer, zug, zug_s, d_model, d_ssm, *, tm, tn):
    m, ka = oa.shape
    ks = y.shape[1]
    ms = oas.shape[0]
    nj, ni = d_model // tn, m // tm
    off_a = d_ssm // tn
    off_b = (d_ssm + d_model) // tn
    nglu = d_model // tn
    return pl.pallas_call(
        _merge_kernel,
        grid=(nj, ni),
        in_specs=[
            pl.BlockSpec((tm, ka), lambda j, i: (i, 0)), pl.BlockSpec((ms, ka), lambda j, i: (0, 0)),
            pl.BlockSpec((tm, ks), lambda j, i: (i, 0)), pl.BlockSpec((ms, ks), lambda j, i: (0, 0)),
            pl.BlockSpec((None, ka, tn), lambda j, i: (layer, 0, j)),
            pl.BlockSpec((None, ks, tn), lambda j, i: (layer, 0, j)),
            pl.BlockSpec((None, ks, tn), lambda j, i: (layer, 0, j + nglu)),
            pl.BlockSpec((tm, tn), lambda j, i: (i, j + off_a)), pl.BlockSpec((tm, tn), lambda j, i: (i, j + off_b)),
            pl.BlockSpec((ms, tn), lambda j, i: (0, j + off_a)), pl.BlockSpec((ms, tn), lambda j, i: (0, j + off_b)),
        ],
        out_specs=[pl.BlockSpec((tm, tn), lambda j, i: (i, j)), pl.BlockSpec((ms, tn), lambda j, i: (0, j))],
        out_shape=[jax.ShapeDtypeStruct((m, d_model), _BF), jax.ShapeDtypeStruct((ms, d_model), _BF)],
        scratch_shapes=[pltpu.VMEM((ka, tn), _BF), pltpu.VMEM((ks, tn), _BF), pltpu.VMEM((ks, tn), _BF)],
        compiler_params=_params(("arbitrary", "arbitrary")),
        name="merge",
    )(oa, oas, y, ys, w_attn_out, w_glu, w_glu, zug, zug, zug_s, zug_s)


def _down_kernel(a_ref, as_ref, w_ref, res_ref, ress_ref, o_ref, os_ref):
    i = pl.program_id(0)
    k = pl.program_id(1)
    wb = w_ref[...].astype(_BF)
    part = jnp.dot(a_ref[...], wb, preferred_element_type=_F32)

    @pl.when(k == 0)
    def _():
        o_ref[...] = res_ref[...] + part

    @pl.when(k > 0)
    def _():
        o_ref[...] += part

    @pl.when(i == 0)
    def _():
        part_s = jnp.dot(as_ref[...], wb, preferred_element_type=_F32)

        @pl.when(k == 0)
        def _():
            os_ref[...] = ress_ref[...] + part_s

        @pl.when(k > 0)
        def _():
            os_ref[...] += part_s


def _ffn_down(a, a_s, w2, layer, res, res_s, *, tm, tk):
    m, f = a.shape
    ms = a_s.shape[0]
    n = w2.shape[2]
    return pl.pallas_call(
        _down_kernel,
        grid=(m // tm, f // tk),
        in_specs=[
            pl.BlockSpec((tm, tk), lambda i, k: (i, k)), pl.BlockSpec((ms, tk), lambda i, k: (0, k)),
            pl.BlockSpec((None, tk, n), lambda i, k: (layer, k, 0)),
            pl.BlockSpec((tm, n), lambda i, k: (i, 0)), pl.BlockSpec((ms, n), lambda i, k: (0, 0)),
        ],
        out_specs=[pl.BlockSpec((tm, n), lambda i, k: (i, 0)), pl.BlockSpec((ms, n), lambda i, k: (0, 0))],
        out_shape=[jax.ShapeDtypeStruct((m, n), _F32), jax.ShapeDtypeStruct((ms, n), _F32)],
        compiler_params=_params(("arbitrary", "arbitrary")),
        name="ffn_down",
    )(a, a_s, w2, res, res_s)


def _masked_softmax(s, mask):
    s = jnp.where(mask, s, -jnp.inf)
    m = jnp.max(s, axis=-1, keepdims=True)
    m = jnp.where(jnp.isfinite(m), m, 0.0)
    p = jnp.where(mask, jnp.exp(s - m), 0.0)
    den = jnp.maximum(jnp.sum(p, axis=-1, keepdims=True), 1e-30)
    return p * (1.0 / den)


def _dot_t(a, b, precision=None):
    return lax.dot_general(a, b, (((1,), (1,)), ((), ())), precision=precision, preferred_element_type=_F32)


def _nsa_prompt_kernel(slope_ref, q_ref, kc_ref, vc_ref, ks_ref, vs_ref, kw_ref, vw_ref, gate_ref,
                       pk_ref, pv_ref, e_ref, o_ref, kcp, vcp, ksb, vsb, kwb, vwb, *, tq, seq, nb, band):
    kh = pl.program_id(1)
    qi = pl.program_id(2)

    @pl.when(qi == 0)
    def _():
        kcp[...] = jnp.dot(pk_ref[...].astype(_BF), kc_ref[...].astype(_BF),
                           preferred_element_type=_F32).astype(_BF)
        vcp[...] = jnp.dot(pv_ref[...].astype(_BF), vc_ref[...].astype(_BF),
                           preferred_element_type=_F32).astype(_BF)
        ksb[...] = ks_ref[...].astype(_BF)
        vsb[...] = vs_ref[...].astype(_BF)
        kwb[...] = kw_ref[...].astype(_BF)
        vwb[...] = vw_ref[...].astype(_BF)

    t0 = qi * tq
    qpos = t0 + lax.broadcasted_iota(jnp.int32, (tq, 1), 0)
    qpos_f = qpos.astype(_F32)
    q = q_ref[...]
    gate = _sigmoid(gate_ref[...])

    blk = lax.broadcasted_iota(jnp.int32, (tq, nb), 1)
    center = (blk * BLOCK).astype(_F32) + 0.5 * (BLOCK - 1)
    valid_c = ((blk + 1) * BLOCK - 1) <= qpos
    dist_c = qpos_f - center
    kc = kcp[...]
    vcb = vcp[...]
    o_cmp = []
    imp = jnp.zeros((tq, nb), _F32)
    for g in range(GROUP):
        slope = slope_ref[kh * GROUP + g]
        qg = q[:, g * HEAD_DIM:(g + 1) * HEAD_DIM].astype(_BF)
        s = _dot_t(qg, kc) * SCALE - slope * dist_c
        p = _masked_softmax(s, valid_c)
        imp = imp + p
        o_cmp.append(jnp.dot(p.astype(_BF), vcb, preferred_element_type=_F32))

    cur = qpos // BLOCK
    forced = (blk == cur) | (blk == cur - 1) | (blk == 0)
    score = jnp.where(forced, FORCED_SCORE, jnp.where(blk > cur, -jnp.inf, imp))
    rank = jnp.zeros((tq, nb), _F32)
    for i in range(nb):
        col = score[:, i:i + 1]
        beats = (col > score) | ((col == score) & (blk > i))
        rank = rank + jnp.where(beats, 1.0, 0.0)
    sel = jnp.where(rank < float(min(N_SELECT, nb)), 1.0, 0.0).astype(_BF)
    sel_keys = jnp.dot(sel, e_ref[...], preferred_element_type=_F32) > 0.5

    kpos = lax.broadcasted_iota(jnp.int32, (1, seq), 1)
    dist_s = qpos - kpos
    mask_s = sel_keys & (dist_s >= 0)
    dist_sf = dist_s.astype(_F32)

    ws = pl.multiple_of(jnp.maximum(t0 - WINDOW, 0), math.gcd(tq, WINDOW))
    kpos_w = ws + lax.broadcasted_iota(jnp.int32, (1, band), 1)
    dist_w = qpos - kpos_w
    mask_w = (dist_w >= 0) & (dist_w < WINDOW)
    dist_wf = dist_w.astype(_F32)
    kw_band = kwb[pl.ds(ws, band), :]
    vw_band = vwb[pl.ds(ws, band), :]

    for g in range(GROUP):
        slope = slope_ref[kh * GROUP + g]
        qg = q[:, g * HEAD_DIM:(g + 1) * HEAD_DIM].astype(_BF)
        s = _dot_t(qg, ksb[...]) * SCALE - slope * dist_sf
        p = _masked_softmax(s, mask_s)
        o_sel = jnp.dot(p.astype(_BF), vsb[...], preferred_element_type=_F32)
        s = _dot_t(qg, kw_band) * SCALE - slope * dist_wf
        p = _masked_softmax(s, mask_w)
        o_win = jnp.dot(p.astype(_BF), vw_band, preferred_element_type=_F32)
        o = (gate[:, g:g + 1] * o_cmp[g] + gate[:, GROUP + g:GROUP + g + 1] * o_sel
             + gate[:, 2 * GROUP + g:2 * GROUP + g + 1] * o_win)
        o_ref[:, g * HEAD_DIM:(g + 1) * HEAD_DIM] = o.astype(o_ref.dtype)


def _alibi_slopes():
    h = np.arange(1, N_HEADS + 1, dtype=np.float32)
    return jnp.asarray(np.exp2(-ALIBI_MAX_EXP * h / N_HEADS).astype(np.float32))


def _nsa_prompt(zqkv, gates_r, pool_k, pool_v, expand, batch, seq):
    nb = seq // BLOCK
    tq = _tile(seq, 256)
    nq = seq // tq
    band = min(WINDOW + tq, seq)
    d_attn = N_HEADS * HEAD_DIM
    kv0 = d_attn // HEAD_DIM

    def kv_spec(branch, which):
        base = kv0 + (branch * 2 + which) * N_KV_HEADS
        return pl.BlockSpec((seq, HEAD_DIM), lambda b, k, qi, sl: (b, base + k))

    grid_spec = pltpu.PrefetchScalarGridSpec(
        num_scalar_prefetch=1,
        grid=(batch, N_KV_HEADS, nq),
        in_specs=[
            pl.BlockSpec((tq, GROUP * HEAD_DIM), lambda b, k, qi, sl: (b * nq + qi, k)),
            kv_spec(0, 0), kv_spec(0, 1), kv_spec(1, 0), kv_spec(1, 1), kv_spec(2, 0), kv_spec(2, 1),
            pl.BlockSpec((None, tq, 3 * GROUP), lambda b, k, qi, sl: (k, b * nq + qi, 0)),
            pl.BlockSpec((None, nb, seq), lambda b, k, qi, sl: (k, 0, 0)),
            pl.BlockSpec((None, nb, seq), lambda b, k, qi, sl: (k, 0, 0)),
            pl.BlockSpec((nb, seq), lambda b, k, qi, sl: (0, 0)),
        ],
        out_specs=pl.BlockSpec((tq, GROUP * HEAD_DIM), lambda b, k, qi, sl: (b * nq + qi, k)),
        scratch_shapes=[pltpu.VMEM((nb, HEAD_DIM), _BF), pltpu.VMEM((nb, HEAD_DIM), _BF)]
        + [pltpu.VMEM((seq, HEAD_DIM), _BF) for _ in range(4)],
    )
    return pl.pallas_call(
        functools.partial(_nsa_prompt_kernel, tq=tq, seq=seq, nb=nb, band=band),
        grid_spec=grid_spec,
        out_shape=jax.ShapeDtypeStruct((batch * seq, d_attn), _BF),
        compiler_params=_params(("arbitrary", "arbitrary", "arbitrary")),
        name="nsa_prompt",
    )(_alibi_slopes(), zqkv, zqkv, zqkv, zqkv, zqkv, zqkv, zqkv, gates_r, pool_k, pool_v, expand)


def _pool_pages_kernel(pt_ref, c_ref, w_ref, o_ref):
    x = c_ref[...].astype(_BF).astype(_F32)
    w = w_ref[...].astype(_BF).astype(_F32)
    for n in range(PAGE_SIZE // BLOCK):
        o_ref[n] = jnp.sum(x[n * BLOCK:(n + 1) * BLOCK] * w, axis=0).astype(_BF).astype(_F32)


def _pool_pages(cache, page_table, w_exp, layer):
    db, n_pages = page_table.shape
    per = PAGE_SIZE // BLOCK
    row = cache.shape[3:]
    grid_spec = pltpu.PrefetchScalarGridSpec(
        num_scalar_prefetch=1,
        grid=(db, n_pages),
        in_specs=[
            pl.BlockSpec((None, None, PAGE_SIZE) + row, lambda b, p, pt: (layer, pt[b * n_pages + p], 0, 0, 0, 0)),
            pl.BlockSpec((BLOCK,) + row, lambda b, p, pt: (0, 0, 0, 0)),
        ],
        out_specs=pl.BlockSpec((None, None, per) + row, lambda b, p, pt: (b, p, 0, 0, 0, 0)),
    )
    return pl.pallas_call(
        _pool_pages_kernel,
        grid_spec=grid_spec,
        out_shape=jax.ShapeDtypeStruct((db, n_pages, per) + row, _F32),
        compiler_params=_params(("arbitrary", "arbitrary")),
        name="pool_pages",
    )(page_table.reshape(-1), cache, w_exp)


def _cmp_select_kernel(slope_ref, q_ref, pool_ref, new_ref, w0_ref, oc_ref, idx_ref, *, past, n_sel):
    nbp = pool_ref.shape[0]
    qpos = past
    q = q_ref[...]
    pooled_new = new_ref[...] * w0_ref[...]
    blk = lax.broadcasted_iota(jnp.int32, (1, nbp), 1)
    center = (blk * BLOCK).astype(_F32) + 0.5 * (BLOCK - 1)
    valid = ((blk + 1) * BLOCK - 1) <= qpos
    dist = float(qpos) - center
    center_n = float(nbp * BLOCK) + 0.5 * (BLOCK - 1)
    valid_n = jnp.full((1, 1), (nbp + 1) * BLOCK - 1 <= qpos)
    dist_n = float(qpos) - center_n
    rows_i = lax.broadcasted_iota(jnp.int32, (nbp, nbp), 0)
    cols_i = lax.broadcasted_iota(jnp.int32, (nbp, nbp), 1)
    sel_r = lax.broadcasted_iota(jnp.int32, (n_sel, nbp), 0).astype(_F32)
    blk_f = blk.astype(_F32)
    cur = qpos // BLOCK
    for kh in range(N_KV_HEADS):
        kc = pool_ref[:, (2 * kh) * HEAD_DIM:(2 * kh + 1) * HEAD_DIM]
        vc = pool_ref[:, (2 * kh + 1) * HEAD_DIM:(2 * kh + 2) * HEAD_DIM]
        kn = pooled_new[:, (2 * kh) * HEAD_DIM:(2 * kh + 1) * HEAD_DIM]
        vn = pooled_new[:, (2 * kh + 1) * HEAD_DIM:(2 * kh + 2) * HEAD_DIM]
        qk = q[kh * GROUP:(kh + 1) * GROUP]
        slope = jnp.concatenate([jnp.full((1, 1), slope_ref[kh * GROUP + g], _F32) for g in range(GROUP)], axis=0)
        s = _dot_t(qk.astype(_BF), kc.astype(_BF)) * SCALE - slope * dist
        s_n = jnp.sum(qk * kn, axis=-1, keepdims=True) * SCALE - slope * dist_n
        s = jnp.where(valid, s, -jnp.inf)
        s_n = jnp.where(valid_n, s_n, -jnp.inf)
        m = jnp.maximum(jnp.max(s, axis=-1, keepdims=True), s_n)
        m = jnp.where(jnp.isfinite(m), m, 0.0)
        p = jnp.where(valid, jnp.exp(s - m), 0.0)
        p_n = jnp.where(valid_n, jnp.exp(s_n - m), 0.0)
        den = jnp.maximum(jnp.sum(p, axis=-1, keepdims=True) + p_n, 1e-30)
        inv = 1.0 / den
        p = p * inv
        p_n = p_n * inv
        oc_ref[kh * GROUP:(kh + 1) * GROUP, :] = (
            jnp.dot(p.astype(_BF), vc.astype(_BF), preferred_element_type=_F32) + p_n * vn)
        imp = jnp.sum(p, axis=0, keepdims=True)
        imp_n = jnp.sum(p_n, axis=0, keepdims=True)
        forced = (blk == cur) | (blk == cur - 1) | (blk == 0)
        score = jnp.where(forced, FORCED_SCORE, jnp.where(blk > cur, -jnp.inf, imp))
        forced_n = (nbp == cur) | (nbp == cur - 1) | (nbp == 0)
        score_n = jnp.where(forced_n, FORCED_SCORE, jnp.where(nbp > cur, -jnp.inf, imp_n))
        sj = jnp.broadcast_to(score, (nbp, nbp))
        si = jnp.transpose(sj)
        ahead = (si > sj) | ((si == sj) & (rows_i < cols_i))
        rank = jnp.sum(jnp.where(ahead, 1.0, 0.0), axis=0, keepdims=True) + jnp.where(score_n > score, 1.0, 0.0)
        rank_n = jnp.sum(jnp.where(score >= score_n, 1.0, 0.0), axis=-1, keepdims=True)
        onehot = jnp.where(rank == sel_r, 1.0, 0.0)
        sel_col = lax.broadcasted_iota(jnp.int32, (n_sel, 1), 0).astype(_F32)
        idx = jnp.sum(onehot * blk_f, axis=-1, keepdims=True) + jnp.where(rank_n == sel_col, float(nbp), 0.0)
        idx_ref[kh] = jnp.broadcast_to(idx, (n_sel, LANE)).astype(jnp.int32)


def _cmp_select(q_s, pooled, new_rows, w0, past, n_sel):
    db, nbp, width = pooled.shape
    grid_spec = pltpu.PrefetchScalarGridSpec(
        num_scalar_prefetch=1,
        grid=(db,),
        in_specs=[
            pl.BlockSpec((None, N_HEADS, HEAD_DIM), lambda b, sl: (b, 0, 0)),
            pl.BlockSpec((None, nbp, width), lambda b, sl: (b, 0, 0)),
            pl.BlockSpec((None, 1, width), lambda b, sl: (b, 0, 0)),
            pl.BlockSpec((1, width), lambda b, sl: (0, 0)),
        ],
        out_specs=[
            pl.BlockSpec((None, N_HEADS, HEAD_DIM), lambda b, sl: (b, 0, 0)),
            pl.BlockSpec((None, N_KV_HEADS, n_sel, LANE), lambda b, sl: (b, 0, 0, 0)),
        ],
    )
    return pl.pallas_call(
        functools.partial(_cmp_select_kernel, past=past, n_sel=n_sel),
        grid_spec=grid_spec,
        out_shape=[jax.ShapeDtypeStruct((db, N_HEADS, HEAD_DIM), _F32),
                   jax.ShapeDtypeStruct((db, N_KV_HEADS, n_sel, LANE), jnp.int32)],
        compiler_params=_params(("arbitrary",)),
        name="cmp_select",
    )(_alibi_slopes(), q_s, pooled, new_rows, w0)


def _sel_sample_kernel(idx_ref, pt_ref, slope_ref, q_ref, blk_ref, new_ref, o_ref, rows_ref, *, past, n_sel):
    b = pl.program_id(0)
    kh = pl.program_id(1)
    j = pl.program_id(2)
    base = (b * N_KV_HEADS + kh) * n_sel
    rw = 2 * N_KV_HEADS
    kpos_c = idx_ref[base + j] * BLOCK + lax.broadcasted_iota(jnp.int32, (BLOCK, 1, 1), 0)
    rows_ref[pl.ds(pl.multiple_of(j * BLOCK, BLOCK), BLOCK)] = jnp.where(kpos_c < past, blk_ref[...], new_ref[...])

    @pl.when(j == n_sel - 1)
    def _():
        qpos = past
        n_rows = n_sel * BLOCK * rw
        rows = rows_ref[...].reshape(n_rows, HEAD_DIM).astype(_BF)
        col = lax.broadcasted_iota(jnp.int32, (1, n_rows), 1)
        pos = col // rw
        blk_of = jnp.zeros((1, n_rows), jnp.int32)
        for jj in range(n_sel):
            blk_of = jnp.where(pos // BLOCK == jj, idx_ref[base + jj], blk_of)
        dist = qpos - (blk_of * BLOCK + pos % BLOCK)
        mask = (dist >= 0) & (col % rw == 2 * kh)
        slope = jnp.concatenate([jnp.full((1, 1), slope_ref[kh * GROUP + g], _F32) for g in range(GROUP)], axis=0)
        s = _dot_t(q_ref[...].astype(_BF), rows) * SCALE - slope * dist.astype(_F32)
        p = _masked_softmax(s, mask)
        p_val = pltpu.roll(p, 1, 1)
        o_ref[...] = jnp.dot(p_val.astype(_BF), rows, preferred_element_type=_F32)


def _sel_sample(q_s, cache, idx, page_table, new_rows, layer, past, n_sel):
    db, n_pages = page_table.shape
    rw = 2 * N_KV_HEADS

    def blk_map(b, k, j, idx_ref, pt_ref, sl):
        blk = idx_ref[(b * N_KV_HEADS + k) * n_sel + j]
        pos = jnp.minimum(blk * BLOCK, past - 1)
        return (layer, pt_ref[b * n_pages + pos // PAGE_SIZE], (pos % PAGE_SIZE) // BLOCK, 0, 0)

    grid_spec = pltpu.PrefetchScalarGridSpec(
        num_scalar_prefetch=3,
        grid=(db, N_KV_HEADS, n_sel),
        in_specs=[
            pl.BlockSpec((None, GROUP, HEAD_DIM), lambda b, k, j, *_: (b * N_KV_HEADS + k, 0, 0)),
            pl.BlockSpec((None, None, BLOCK, rw, HEAD_DIM), blk_map),
            pl.BlockSpec((None, 1, rw, HEAD_DIM), lambda b, k, j, *_: (b, 0, 0, 0)),
        ],
        out_specs=pl.BlockSpec((None, GROUP, HEAD_DIM), lambda b, k, j, *_: (b * N_KV_HEADS + k, 0, 0)),
        scratch_shapes=[pltpu.VMEM((n_sel * BLOCK, rw, HEAD_DIM), _F32)],
    )
    return pl.pallas_call(
        functools.partial(_sel_sample_kernel, past=past, n_sel=n_sel),
        grid_spec=grid_spec,
        out_shape=jax.ShapeDtypeStruct((db * N_KV_HEADS, GROUP, HEAD_DIM), _F32),
        compiler_params=_params(("arbitrary", "arbitrary", "arbitrary")),
        name="sel_sample",
    )(idx, page_table.reshape(-1), _alibi_slopes(), q_s, cache, new_rows)


def _win_combine_kernel(slope_ref, q_ref, win_ref, new_ref, oc_ref, os_ref, gate_ref, o_ref, *, past, w_buf):
    qpos = past
    q = q_ref[...]
    gate = _sigmoid(gate_ref[...])
    kpos = past - w_buf + lax.broadcasted_iota(jnp.int32, (1, w_buf), 1)
    dist = qpos - kpos
    mask = (dist >= 0) & (dist < WINDOW) & (kpos >= 0)
    dist_f = dist.astype(_F32)
    new = new_ref[...]
    mask_n = jnp.full((1, 1), 0 < WINDOW)
    for kh in range(N_KV_HEADS):
        kw = win_ref[:, (2 * kh) * HEAD_DIM:(2 * kh + 1) * HEAD_DIM].astype(_BF)
        vw = win_ref[:, (2 * kh + 1) * HEAD_DIM:(2 * kh + 2) * HEAD_DIM].astype(_BF)
        kn = new[:, (2 * kh) * HEAD_DIM:(2 * kh + 1) * HEAD_DIM]
        vn = new[:, (2 * kh + 1) * HEAD_DIM:(2 * kh + 2) * HEAD_DIM]
        hs = slice(kh * GROUP, (kh + 1) * GROUP)
        qk = q[hs]
        slope = jnp.concatenate([jnp.full((1, 1), slope_ref[kh * GROUP + g], _F32) for g in range(GROUP)], axis=0)
        s = _dot_t(qk.astype(_BF), kw) * SCALE - slope * dist_f
        qb = qk.astype(_BF).astype(_F32)
        s_n = jnp.sum(qb * kn.astype(_BF).astype(_F32), axis=-1, keepdims=True) * SCALE
        s = jnp.where(mask, s, -jnp.inf)
        s_n = jnp.where(mask_n, s_n, -jnp.inf)
        m = jnp.maximum(jnp.max(s, axis=-1, keepdims=True), s_n)
        m = jnp.where(jnp.isfinite(m), m, 0.0)
        p = jnp.where(mask, jnp.exp(s - m), 0.0)
        p_n = jnp.where(mask_n, jnp.exp(s_n - m), 0.0)
        inv = 1.0 / jnp.maximum(jnp.sum(p, axis=-1, keepdims=True) + p_n, 1e-30)
        o_w = (jnp.dot((p * inv).astype(_BF), vw, preferred_element_type=_F32)
               + (p_n * inv).astype(_BF).astype(_F32) * vn.astype(_BF).astype(_F32))
        g3 = gate[hs]
        o_ref[hs, :] = g3[:, 0:1] * oc_ref[hs, :] + g3[:, 1:2] * os_ref[hs, :] + g3[:, 2:3] * o_w


def _win_combine(q_s, win, new_rows, o_cmp, o_sel, gates, past):
    db, w_buf, width = win.shape
    head = pl.BlockSpec((None, N_HEADS, HEAD_DIM), lambda b, sl: (b, 0, 0))
    grid_spec = pltpu.PrefetchScalarGridSpec(
        num_scalar_prefetch=1,
        grid=(db,),
        in_specs=[
            head,
            pl.BlockSpec((None, w_buf, width), lambda b, sl: (b, 0, 0)),
            pl.BlockSpec((None, 1, width), lambda b, sl: (b, 0, 0)),
            head, head,
            pl.BlockSpec((None, N_HEADS, 3), lambda b, sl: (b, 0, 0)),
        ],
        out_specs=head,
    )
    return pl.pallas_call(
        functools.partial(_win_combine_kernel, past=past, w_buf=w_buf),
        grid_spec=grid_spec,
        out_shape=jax.ShapeDtypeStruct((db, N_HEADS, HEAD_DIM), _F32),
        compiler_params=_params(("arbitrary",)),
        name="win_combine",
    )(_alibi_slopes(), q_s, win, new_rows, o_cmp, o_sel, gates)


def _ssm_discretise(a_re, a_im, log_dt, b_re, b_im):
    dt = jnp.exp(log_dt)[:, None]
    mag = jnp.exp(dt * a_re)
    abar_re = mag * jnp.cos(dt * a_im)
    abar_im = mag * jnp.sin(dt * a_im)
    den = a_re * a_re + a_im * a_im
    cf_re = ((abar_re - 1.0) * a_re + abar_im * a_im) / den
    cf_im = (abar_im * a_re - (abar_re - 1.0) * a_im) / den
    bb_re = cf_re[..., None] * b_re - cf_im[..., None] * b_im
    bb_im = cf_re[..., None] * b_im + cf_im[..., None] * b_re
    return abar_re, abar_im, bb_re, bb_im


def _ssm_chunk_matrices(abar_re, abar_im, bb_re, bb_im, c_re, c_im):
    L = SSM_CHUNK
    g, p, ch = bb_re.shape
    nq = LANE // ch
    nblk = g // nq
    pw_re = [jnp.ones_like(abar_re)]
    pw_im = [jnp.zeros_like(abar_im)]
    for _ in range(L):
        r, i = pw_re[-1], pw_im[-1]
        pw_re.append(r * abar_re - i * abar_im)
        pw_im.append(r * abar_im + i * abar_re)
    pw_re = jnp.stack(pw_re, axis=1)
    pw_im = jnp.stack(pw_im, axis=1)
    e_re = pw_re[:, :L, :, None] * bb_re[:, None] - pw_im[:, :L, :, None] * bb_im[:, None]
    e_im = pw_re[:, :L, :, None] * bb_im[:, None] + pw_im[:, :L, :, None] * bb_re[:, None]
    k_lag = (jnp.einsum("gcp,glpd->glcd", c_re, e_re, precision=_HI)
             - jnp.einsum("gcp,glpd->glcd", c_im, e_im, precision=_HI))
    eye = jnp.eye(nq, dtype=_F32)
    bd = jnp.einsum("bqlcd,qr->blqdrc", k_lag.reshape(nblk, nq, L, ch, ch), eye).reshape(nblk, L, LANE, LANE)
    jj = np.arange(L)[:, None]
    tt = np.arange(L)[None, :]
    m = bd[:, np.clip(tt - jj, 0, L - 1)]
    m = jnp.where((tt >= jj)[None, :, :, None, None], m, 0.0)
    m = m.transpose(0, 1, 3, 2, 4).reshape(nblk, L * LANE, L * LANE)

    def state_in(e):
        eb = e[:, ::-1].reshape(nblk, nq, L, p, ch)
        return jnp.einsum("bqjpd,qr->bjqdrp", eb, eye).reshape(nblk, L * LANE, nq * p)

    pr = pw_re[:, 1:, None, :]
    pi = pw_im[:, 1:, None, :]

    def state_out(v):
        vb = v.reshape(nblk, nq, L, ch, p)
        return jnp.einsum("bqtcp,qr->bqptrc", vb, eye).reshape(nblk, nq * p, L * LANE)

    v_re = state_out(c_re[:, None] * pr - c_im[:, None] * pi)
    v_im = state_out(-(c_re[:, None] * pi + c_im[:, None] * pr))
    al_re = pw_re[:, L].reshape(nblk, 1, nq * p)
    al_im = pw_im[:, L].reshape(nblk, 1, nq * p)
    return (m.astype(_BF), state_in(e_re).astype(_BF), state_in(e_im).astype(_BF),
            v_re.astype(_BF), v_im.astype(_BF), al_re, al_im)


def _ssm_prompt_kernel(u_ref, m_ref, wre_ref, wim_ref, vre_ref, vim_ref, alre_ref, alim_ref, d_ref,
                       y_ref, hre_ref, him_ref, xs, ys, loc_re, loc_im, hp_re, hp_im, *, n_chunk):
    L = SSM_CHUNK
    for j in range(L):
        xs[:, j * LANE:(j + 1) * LANE] = u_ref[pl.ds(j, n_chunk, stride=L), :].astype(_BF)
    xb = xs[...]
    y = jnp.dot(xb, m_ref[...], preferred_element_type=_F32)
    loc_re[...] = jnp.dot(xb, wre_ref[...], preferred_element_type=_F32)
    loc_im[...] = jnp.dot(xb, wim_ref[...], preferred_element_type=_F32)
    a_re = alre_ref[...]
    a_im = alim_ref[...]

    def step(c, carry):
        h_re, h_im = carry
        hp_re[pl.ds(c, 1), :] = h_re
        hp_im[pl.ds(c, 1), :] = h_im
        l_re = loc_re[pl.ds(c, 1), :]
        l_im = loc_im[pl.ds(c, 1), :]
        return (a_re * h_re - a_im * h_im + l_re, a_re * h_im + a_im * h_re + l_im)

    zero = jnp.zeros(a_re.shape, _F32)
    h_re, h_im = lax.fori_loop(0, n_chunk, step, (zero, zero))
    y = y + jnp.dot(hp_re[...].astype(_BF), vre_ref[...], preferred_element_type=_F32)
    y = y + jnp.dot(hp_im[...].astype(_BF), vim_ref[...], preferred_element_type=_F32)
    d = d_ref[...]
    for t in range(L):
        x_t = u_ref[pl.ds(t, n_chunk, stride=L), :]
        ys[pl.ds(t, n_chunk, stride=L), :] = y[:, t * LANE:(t + 1) * LANE] + d * x_t
    y_ref[...] = ys[...].astype(y_ref.dtype)
    hre_ref[...] = h_re
    him_ref[...] = h_im


def _ssm_prompt(zug, mats, d_skip, batch, seq, d_ssm):
    m, wre, wim, vre, vim, al_re, al_im = mats
    L = SSM_CHUNK
    nblk = m.shape[0]
    nq = LANE // GROUP_CH
    n_chunk = seq // L
    sw = nq * P_STATE
    per_blk = lambda shape: pl.BlockSpec((None,) + shape, lambda g, b: (g, 0, 0))
    fin_spec = pl.BlockSpec((None, None, 1, sw), lambda g, b: (g, b, 0, 0))
    y, h_re, h_im = pl.pallas_call(
        functools.partial(_ssm_prompt_kernel, n_chunk=n_chunk),
        grid=(nblk, batch),
        in_specs=[pl.BlockSpec((seq, LANE), lambda g, b: (b, g)),
                  per_blk((L * LANE, L * LANE)), per_blk((L * LANE, sw)), per_blk((L * LANE, sw)),
                  per_blk((sw, L * LANE)), per_blk((sw, L * LANE)), per_blk((1, sw)), per_blk((1, sw)),
                  per_blk((1, LANE))],
        out_specs=[pl.BlockSpec((seq, LANE), lambda g, b: (b, g)), fin_spec, fin_spec],
        out_shape=[jax.ShapeDtypeStruct((batch * seq, d_ssm), _BF),
                   jax.ShapeDtypeStruct((nblk, batch, 1, sw), _F32),
                   jax.ShapeDtypeStruct((nblk, batch, 1, sw), _F32)],
        scratch_shapes=[pltpu.VMEM((n_chunk, L * LANE), _BF), pltpu.VMEM((seq, LANE), _F32)]
        + [pltpu.VMEM((n_chunk, sw), _F32) for _ in range(4)],
        compiler_params=_params(("arbitrary", "arbitrary")),
        name="ssm_prompt",
    )(zug, m, wre, wim, vre, vim, al_re, al_im, d_skip.reshape(nblk, 1, LANE))
    fin = lambda h: h.reshape(nblk, batch, nq, P_STATE).transpose(1, 0, 2, 3).reshape(batch, nblk * nq, P_STATE)
    return y, jnp.stack([fin(h_re), fin(h_im)], axis=-1)


def _ssm_sample_kernel(xe_ref, bre_ref, bim_ref, h0r_ref, h0i_ref, ar_ref, ai_ref, cre_ref, cim_ref, seg_ref,
                       xt_ref, d_ref, y_ref, sre_ref, sim_ref):
    ch = xe_ref.shape[0]
    h0r, h0i = h0r_ref[...], h0i_ref[...]
    a_re, a_im = ar_ref[...], ai_ref[...]
    rnd = lambda v: v.astype(_BF).astype(_F32)
    bu_re = rnd(xe_ref[0]) * rnd(bre_ref[0])
    bu_im = rnd(xe_ref[0]) * rnd(bim_ref[0])
    for c in range(1, ch):
        bu_re = bu_re + rnd(xe_ref[c]) * rnd(bre_ref[c])
        bu_im = bu_im + rnd(xe_ref[c]) * rnd(bim_ref[c])
    s_re = bu_re + (a_re * h0r - a_im * h0i)
    s_im = bu_im + (a_re * h0i + a_im * h0r)
    sre_ref[...] = s_re
    sim_ref[...] = s_im
    seg = seg_ref[...]
    sb_re, sb_im = rnd(s_re), rnd(s_im)
    for c in range(ch):
        y_re = jnp.dot(sb_re * rnd(cre_ref[c]), seg, precision=_HI, preferred_element_type=_F32)
        y_im = jnp.dot(sb_im * rnd(cim_ref[c]), seg, precision=_HI, preferred_element_type=_F32)
        y_ref[c] = y_re - y_im + d_ref[c] * xt_ref[c]


def _ssm_sample(u_s, h0, abar_re, abar_im, bb_re, bb_im, c_re, c_im, d_skip):
    db = u_s.shape[0]
    g, p, ch = bb_re.shape
    gp = g * p
    x3 = u_s.reshape(db, g, ch)
    xe = jnp.broadcast_to(x3.transpose(2, 0, 1)[..., None], (ch, db, g, p)).reshape(ch, db, gp)
    flat_b = lambda b: b.transpose(2, 0, 1).reshape(ch, 1, gp)
    flat_c = lambda c: c.transpose(1, 0, 2).reshape(ch, 1, gp)
    seg = jnp.asarray(np.repeat(np.eye(g, dtype=np.float32), p, axis=0))
    xt = x3.transpose(2, 0, 1)
    dt = d_skip.transpose(1, 0).reshape(ch, 1, g)
    y, s_re, s_im = pl.pallas_call(
        _ssm_sample_kernel,
        out_shape=[jax.ShapeDtypeStruct((ch, db, g), _F32), jax.ShapeDtypeStruct((db, gp), _F32),
                   jax.ShapeDtypeStruct((db, gp), _F32)],
        compiler_params=pltpu.CompilerParams(vmem_limit_bytes=VMEM_LIMIT),
        name="ssm_sample",
    )(xe, flat_b(bb_re), flat_b(bb_im), h0[..., 0].reshape(db, gp), h0[..., 1].reshape(db, gp),
      abar_re.reshape(1, gp), abar_im.reshape(1, gp), flat_c(c_re), flat_c(c_im), seg, xt, dt)
    y = y.transpose(1, 2, 0).reshape(db, g * ch)
    h_fin = jnp.stack([s_re.reshape(db, g, p), s_im.reshape(db, g, p)], axis=-1)
    return y, h_fin


def _router_kernel(x_ref, g_ref, wr_ref, idx_ref, gate_ref):
    x = x_ref[...]
    h = x * lax.rsqrt(jnp.mean(x * x, axis=-1, keepdims=True) + EPS) * g_ref[...]
    logits = jnp.dot(h.astype(_BF), wr_ref[...].astype(_BF), preferred_element_type=_F32)
    lane = lax.broadcasted_iota(jnp.int32, logits.shape, 1)
    lane_f = lane.astype(_F32)
    lg = jnp.where(lane < N_EXPERTS, logits, -jnp.inf)
    v1 = jnp.max(lg, axis=-1, keepdims=True)
    i1 = jnp.min(jnp.where(lg == v1, lane_f, float(LANE)), axis=-1, keepdims=True)
    lg2 = jnp.where(lane_f == i1, -jnp.inf, lg)
    v2 = jnp.max(lg2, axis=-1, keepdims=True)
    i2 = jnp.min(jnp.where(lg2 == v2, lane_f, float(LANE)), axis=-1, keepdims=True)
    e = jnp.exp(v2 - v1)
    den = 1.0 + e
    idx_ref[...] = jnp.where(lane == 0, i1, jnp.where(lane == 1, i2, 0.0)).astype(jnp.int32)
    gate_ref[...] = jnp.where(lane == 0, 1.0 / den, jnp.where(lane == 1, e / den, 0.0))


def _router(x, g, wr_pad):
    n, d = x.shape
    tm = _tile(n, 256)
    return pl.pallas_call(
        _router_kernel,
        grid=(n // tm,),
        in_specs=[pl.BlockSpec((tm, d), lambda i: (i, 0)), pl.BlockSpec((1, d), lambda i: (0, 0)),
                  pl.BlockSpec((d, LANE), lambda i: (0, 0))],
        out_specs=[pl.BlockSpec((tm, LANE), lambda i: (i, 0)), pl.BlockSpec((tm, LANE), lambda i: (i, 0))],
        out_shape=[jax.ShapeDtypeStruct((n, LANE), jnp.int32), jax.ShapeDtypeStruct((n, LANE), _F32)],
        compiler_params=_params(("arbitrary",)),
        name="router",
    )(x, g.reshape(1, d), wr_pad)


def _row_copy(src_hbm, row, dst, slot, sem):
    return pltpu.make_async_copy(src_hbm.at[pl.ds(row, 1)], dst.at[pl.ds(slot, 1)], sem)


def _moe_gather_kernel(tok_ref, x_hbm, g_ref, o_ref, buf, sem, *, tm):
    def start(r, c):
        _row_copy(x_hbm, tok_ref[0, 0, r], buf, r, sem).start()
        return c

    lax.fori_loop(0, tm, start, 0)

    def wait(r, c):
        _row_copy(x_hbm, 0, buf, r, sem).wait()
        return c

    lax.fori_loop(0, tm, wait, 0)
    x = buf[...]
    inv = lax.rsqrt(jnp.mean(x * x, axis=-1, keepdims=True) + EPS)
    o_ref[...] = (x * inv * g_ref[...]).astype(o_ref.dtype)


def _moe_gather(x_all, g, row_tok, tm):
    n_rows = row_tok.shape[0]
    d = x_all.shape[1]
    n_tiles = n_rows // tm
    return pl.pallas_call(
        functools.partial(_moe_gather_kernel, tm=tm),
        grid=(n_tiles,),
        in_specs=[pl.BlockSpec((1, 1, tm), lambda i: (i, 0, 0), memory_space=pltpu.SMEM),
                  pl.BlockSpec(memory_space=pl.ANY),
                  pl.BlockSpec((1, d), lambda i: (0, 0))],
        out_specs=pl.BlockSpec((tm, d), lambda i: (i, 0)),
        out_shape=jax.ShapeDtypeStruct((n_rows, d), _BF),
        scratch_shapes=[pltpu.VMEM((tm, d), _F32), pltpu.SemaphoreType.DMA(())],
        compiler_params=_params(("arbitrary",)),
        name="moe_gather",
    )(row_tok.reshape(n_tiles, 1, tm), x_all, g.reshape(1, d))


def _moe_up_kernel(texp_ref, nused_ref, x_ref, w1_ref, w3_ref, o_ref, w1b, w3b):
    i = pl.program_id(1)
    n_used = nused_ref[0]
    prev = texp_ref[jnp.maximum(i - 1, 0)]
    fresh = (i == 0) | (texp_ref[i] != prev)

    @pl.when((i < n_used) & fresh)
    def _():
        w1b[...] = w1_ref[...].astype(_BF)
        w3b[...] = w3_ref[...].astype(_BF)

    @pl.when(i < n_used)
    def _():
        x = x_ref[...]
        a = jnp.dot(x, w1b[...], preferred_element_type=_F32)
        b = jnp.dot(x, w3b[...], preferred_element_type=_F32)
        o_ref[...] = ((a * _sigmoid(a)) * b).astype(o_ref.dtype)

    @pl.when(i >= n_used)
    def _():
        o_ref[...] = jnp.zeros(o_ref.shape, o_ref.dtype)


def _moe_up(xg, w1, w3, layer, tile_exp, n_used, *, tm, tf):
    n_rows, d = xg.shape
    f = w1.shape[3]
    n_tiles = n_rows // tm

    def live(i, nu):
        return jnp.minimum(i, nu[0] - 1)

    grid_spec = pltpu.PrefetchScalarGridSpec(
        num_scalar_prefetch=2,
        grid=(f // tf, n_tiles),
        in_specs=[
            pl.BlockSpec((tm, d), lambda j, i, te, nu: (live(i, nu), 0)),
            pl.BlockSpec((None, None, d, tf), lambda j, i, te, nu: (layer, te[live(i, nu)], 0, j)),
            pl.BlockSpec((None, None, d, tf), lambda j, i, te, nu: (layer, te[live(i, nu)], 0, j)),
        ],
        out_specs=pl.BlockSpec((tm, tf), lambda j, i, te, nu: (i, j)),
        scratch_shapes=[pltpu.VMEM((d, tf), _BF), pltpu.VMEM((d, tf), _BF)],
    )
    return pl.pallas_call(
        _moe_up_kernel,
        grid_spec=grid_spec,
        out_shape=jax.ShapeDtypeStruct((n_rows, f), _BF),
        compiler_params=_params(("arbitrary", "arbitrary")),
        name="moe_up",
    )(tile_exp, n_used, xg, w1, w3)


def _moe_down_kernel(texp_ref, nused_ref, a_ref, w_ref, o_ref):
    i = pl.program_id(0)
    k = pl.program_id(1)

    @pl.when(i < nused_ref[0])
    def _():
        part = jnp.dot(a_ref[...], w_ref[...].astype(_BF), preferred_element_type=_F32)

        @pl.when(k == 0)
        def _():
            o_ref[...] = part

        @pl.when(k > 0)
        def _():
            o_ref[...] += part

    @pl.when((i >= nused_ref[0]) & (k == 0))
    def _():
        o_ref[...] = jnp.zeros(o_ref.shape, o_ref.dtype)


def _moe_down(a, w2, layer, tile_exp, n_used, *, tm, tk):
    n_rows, f = a.shape
    d = w2.shape[3]
    n_tiles = n_rows // tm
    nk = f // tk

    def live(i, nu):
        return jnp.minimum(i, nu[0] - 1)

    def kk(i, k, nu):
        return jnp.where(i < nu[0], k, nk - 1)

    grid_spec = pltpu.PrefetchScalarGridSpec(
        num_scalar_prefetch=2,
        grid=(n_tiles, nk),
        in_specs=[
            pl.BlockSpec((tm, tk), lambda i, k, te, nu: (live(i, nu), kk(i, k, nu))),
            pl.BlockSpec((None, None, tk, d), lambda i, k, te, nu: (layer, te[live(i, nu)], kk(i, k, nu), 0)),
        ],
        out_specs=pl.BlockSpec((tm, d), lambda i, k, te, nu: (i, 0)),
    )
    return pl.pallas_call(
        _moe_down_kernel,
        grid_spec=grid_spec,
        out_shape=jax.ShapeDtypeStruct((n_rows, d), _F32),
        compiler_params=_params(("arbitrary", "arbitrary")),
        name="moe_down",
    )(tile_exp, n_used, a, w2)


def _moe_combine_kernel(pos_ref, x_ref, gate_ref, y_hbm, o_ref, b0, b1, sems, *, tm):
    def start(r, c):
        _row_copy(y_hbm, pos_ref[0, 0, 2 * r], b0, r, sems.at[0]).start()
        _row_copy(y_hbm, pos_ref[0, 0, 2 * r + 1], b1, r, sems.at[1]).start()
        return c

    lax.fori_loop(0, tm, start, 0)

    def wait(r, c):
        _row_copy(y_hbm, 0, b0, r, sems.at[0]).wait()
        _row_copy(y_hbm, 0, b1, r, sems.at[1]).wait()
        return c

    lax.fori_loop(0, tm, wait, 0)
    gate = gate_ref[...]
    o_ref[...] = x_ref[...] + (gate[:, 0:1] * b0[...] + gate[:, 1:2] * b1[...])


def _moe_combine(x, gates, pos, y_rows):
    n, d = x.shape
    tm = _tile(n, 256)
    n_tiles = n // tm
    return pl.pallas_call(
        functools.partial(_moe_combine_kernel, tm=tm),
        grid=(n_tiles,),
        in_specs=[pl.BlockSpec((1, 1, 2 * tm), lambda i: (i, 0, 0), memory_space=pltpu.SMEM),
                  pl.BlockSpec((tm, d), lambda i: (i, 0)),
                  pl.BlockSpec((tm, LANE), lambda i: (i, 0)),
                  pl.BlockSpec(memory_space=pl.ANY)],
        out_specs=pl.BlockSpec((tm, d), lambda i: (i, 0)),
        out_shape=jax.ShapeDtypeStruct((n, d), _F32),
        scratch_shapes=[pltpu.VMEM((tm, d), _F32), pltpu.VMEM((tm, d), _F32), pltpu.SemaphoreType.DMA((2,))],
        compiler_params=_params(("arbitrary",)),
        name="moe_combine",
    )(pos.reshape(n_tiles, 1, 2 * tm), x, gates, y_rows)


def _moe_layer(xp, xs, g, w_router, w1, w3, w2, layer):
    n_p, d = xp.shape
    n_s = xs.shape[0]
    n = n_p + n_s
    wr_pad = jnp.pad(w_router, ((0, 0), (0, LANE - N_EXPERTS)))
    idx_p, gate_p = _router(xp, g, wr_pad)
    idx_s, gate_s = _router(xs, g, wr_pad)
    top_i = jnp.concatenate([idx_p[:, :TOP_K], idx_s[:, :TOP_K]], axis=0)
    n_asg = n * TOP_K
    tm = 512 if n_asg >= 4096 else 64
    n_tiles = -(-n_asg // tm) + N_EXPERTS
    e_flat = top_i.reshape(n_asg)
    onehot = (e_flat[:, None] == jnp.arange(N_EXPERTS)[None, :]).astype(jnp.int32)
    within = jnp.sum((jnp.cumsum(onehot, axis=0) - onehot) * onehot, axis=1)
    counts = jnp.sum(onehot, axis=0)
    padded = (counts + tm - 1) // tm * tm
    pad_end = jnp.cumsum(padded)
    dest = (pad_end - padded)[e_flat] + within
    row_tok = jnp.zeros((n_tiles * tm,), jnp.int32).at[dest].set(jnp.arange(n_asg, dtype=jnp.int32) // TOP_K)
    tile_exp = jnp.minimum(jnp.searchsorted(pad_end, jnp.arange(n_tiles) * tm, side="right"),
                           N_EXPERTS - 1).astype(jnp.int32)
    n_used = (pad_end[-1:] // tm).astype(jnp.int32)
    pos = dest.reshape(n, TOP_K).astype(jnp.int32)

    x_all = jnp.concatenate([xp, xs], axis=0)
    xg = _moe_gather(x_all, g, row_tok, tm)
    f = w1.shape[3]
    a = _moe_up(xg, w1, w3, layer, tile_exp, n_used, tm=tm, tf=_tile(f, 512, LANE))
    y_rows = _moe_down(a, w2, layer, tile_exp, n_used, tm=tm, tk=_tile(f, 512, LANE))
    out_p = _moe_combine(xp, gate_p, pos[:n_p], y_rows)
    out_s = _moe_combine(xs, gate_s, pos[n_p:], y_rows)
    return out_p, out_s


def kernel(x_prompt, x_sample, cache_kv_cmp, cache_kv_sel, cache_kv_win, state_ssm, page_table, norm_mix, norm_ffn, w_in, g_q, g_k, w_cmp, w_attn_out, ssm_a_re, ssm_a_im, ssm_log_dt, ssm_b_re, ssm_b_im, ssm_c_re, ssm_c_im, ssm_d, w_glu, w_out, ffn_w1, ffn_w3, ffn_w2, moe_router, moe_w1, moe_w3, moe_w2):
    batch, seq, d_model = x_prompt.shape
    db, dec_seq, _ = x_sample.shape
    assert dec_seq == 1, "the sample group advances one token per step"
    depth = w_in.shape[0]
    n_pool = cache_kv_cmp.shape[1]
    n_pages = page_table.shape[1]
    past = n_pages * PAGE_SIZE
    d_attn = N_HEADS * HEAD_DIM
    d_kv = N_KV_HEADS * HEAD_DIM
    d_ssm = ssm_d.shape[1] * GROUP_CH
    n_gate = 3 * N_HEADS
    off_gn = d_attn + 6 * d_kv
    off_u = off_gn + n_gate
    n_ug = d_ssm + 2 * d_model
    row_w = N_KV_HEADS * 2 * HEAD_DIM
    nb = seq // BLOCK
    n_p = batch * seq

    tm = _tile(n_p, 1024)
    tn = 512
    assert off_gn % tn == 0 and n_ug % tn == 0 and d_ssm % tn == 0 and d_model % tn == 0

    xp = x_prompt.reshape(n_p, d_model)
    xs = x_sample.reshape(db, d_model)
    flag_cols = np.zeros((1, off_gn), np.float32)
    flag_cols[:, :d_attn] = 1.0
    for br in range(3):
        lo = d_attn + br * 2 * d_kv
        flag_cols[:, lo:lo + d_kv] = 1.0
    flag_cols = jnp.asarray(flag_cols)
    expand = jnp.asarray(np.repeat(np.eye(nb, dtype=np.float32), BLOCK, axis=1)).astype(_BF)
    eye_nb = jnp.asarray(np.eye(nb, dtype=np.float32))
    n_sel = min(N_SELECT, past // BLOCK + 1)

    outs = {k: [] for k in ("cmp_p", "cmp_s", "sel_p", "sel_s", "win_p", "win_s", "ssm_p", "ssm_s")}
    for l in range(depth):
        hp = _rmsnorm(xp, norm_mix[l])
        hs = _rmsnorm(xs, norm_mix[l])
        gain = jnp.concatenate([jnp.tile(g_q[l], N_HEADS)]
                               + [jnp.concatenate([jnp.tile(g_k[l, br], N_KV_HEADS), jnp.ones((d_kv,), _F32)])
                                  for br in range(3)]).reshape(1, off_gn)
        zqkv, zqkv_s = _mm(hp, hs, [(w_in, l, 0)], off_gn, mode="headnorm", out_dtype=_F32, tm=tm, tn=tn,
                           gain=gain, flag=flag_cols, name="in_proj_qkv")
        zg, zg_s = _mm(hp, hs, [(w_in, l, off_gn // LANE)], LANE, mode="plain", out_dtype=_F32, tm=tm, tn=LANE,
                       name="in_proj_gate")
        zug, zug_s = _mm_shifted(hp, hs, w_in, l, off_u, n_ug, tm=tm, tn=tn // 2, name="in_proj_ug")

        def kv_rows(z, lead):
            r = z[:, d_attn:off_gn].reshape(lead + (3, 2, N_KV_HEADS, HEAD_DIM))
            return jnp.swapaxes(r, -3, -2)

        rows_p = kv_rows(zqkv, (batch, seq))
        rows_s = kv_rows(zqkv_s, (db, 1))
        outs["cmp_p"].append(rows_p[:, :, 0])
        outs["sel_p"].append(rows_p[:, :, 1])
        outs["win_p"].append(rows_p[:, seq - min(WINDOW, seq):, 2])
        outs["cmp_s"].append(rows_s[:, :, 0])
        outs["sel_s"].append(rows_s[:, :, 1])

        gates_r = zg[:, :n_gate].reshape(n_p, 3, N_KV_HEADS, GROUP).transpose(2, 0, 1, 3).reshape(
            N_KV_HEADS, n_p, 3 * GROUP)
        wk = w_cmp[l].transpose(0, 2, 1)
        pool = (eye_nb[None, None, :, :, None] * wk[:, :, None, None, :]).reshape(2, N_KV_HEADS, nb, seq)
        o_attn = _nsa_prompt(zqkv, gates_r, pool[0], pool[1], expand, batch, seq)

        q_s = zqkv_s[:, :d_attn].reshape(db, N_HEADS, HEAD_DIM)
        w_exp = jnp.broadcast_to(w_cmp[l].transpose(1, 2, 0)[..., None], (BLOCK, N_KV_HEADS, 2, HEAD_DIM))
        pooled = _pool_pages(cache_kv_cmp, page_table, w_exp, l).reshape(db, past // BLOCK, row_w)
        new_c = rows_s[:, :, 0].reshape(db, 1, row_w)
        o_cmp_s, idx = _cmp_select(q_s, pooled, new_c, w_exp[:1].reshape(1, row_w), past, n_sel)
        idx = idx[..., 0].reshape(-1)
        o_sel_s = _sel_sample(q_s.reshape(db * N_KV_HEADS, GROUP, HEAD_DIM),
                              cache_kv_sel.reshape(depth, n_pool, PAGE_SIZE, 2 * N_KV_HEADS, HEAD_DIM),
                              idx, page_table, rows_s[:, :, 1].reshape(db, 1, 2 * N_KV_HEADS, HEAD_DIM),
                              l, past, n_sel)
        win = cache_kv_win[l]
        w_buf = win.shape[1]
        new_w = rows_s[:, :, 2]
        gates_s = zg_s[:, :n_gate].reshape(db, 3, N_HEADS).transpose(0, 2, 1)
        o_attn_s = _win_combine(q_s, win.reshape(db, w_buf, row_w), new_w.reshape(db, 1, row_w),
                                o_cmp_s, o_sel_s.reshape(db, N_HEADS, HEAD_DIM), gates_s, past)
        outs["win_s"].append(jnp.concatenate([win, new_w], axis=1)[:, 1:])

        abar_re, abar_im, bb_re, bb_im = _ssm_discretise(ssm_a_re[l], ssm_a_im[l], ssm_log_dt[l],
                                                         ssm_b_re[l], ssm_b_im[l])
        mats = _ssm_chunk_matrices(abar_re, abar_im, bb_re, bb_im, ssm_c_re[l], ssm_c_im[l])
        y_p, h_p = _ssm_prompt(zug, mats, ssm_d[l], batch, seq, d_ssm)
        y_s, h_s = _ssm_sample(zug_s[:, :d_ssm], state_ssm[l], abar_re, abar_im, bb_re, bb_im,
                               ssm_c_re[l], ssm_c_im[l], ssm_d[l])
        outs["ssm_p"].append(h_p)
        outs["ssm_s"].append(h_s)

        mg, mg_s = _merge(o_attn, o_attn_s.reshape(db, d_attn).astype(_BF), y_p, y_s.astype(_BF),
                          w_attn_out, w_glu, l, zug, zug_s, d_model, d_ssm, tm=_tile(n_p, 512), tn=tn)
        xp, xs = _mm(mg, mg_s, [(w_out, l, 0)], d_model, mode="resid", out_dtype=_F32, tm=tm, tn=tn,
                     res=xp, res_s=xs, name="out_proj")

        if l % 2 == 0:
            j = l // 2
            hp = _rmsnorm(xp, norm_ffn[l])
            hs = _rmsnorm(xs, norm_ffn[l])
            f = ffn_w1.shape[2]
            a, a_s = _mm(hp, hs, [(ffn_w1, j, 0), (ffn_w3, j, 0)], f, mode="swiglu", out_dtype=_BF,
                         tm=_tile(n_p, 512), tn=_tile(f, 512, LANE), name="ffn_up")
            xp, xs = _ffn_down(a, a_s, ffn_w2, j, xp, xs, tm=_tile(n_p, 512), tk=_tile(f, 256, LANE))
        else:
            j = l // 2
            xp, xs = _moe_layer(xp, xs, norm_ffn[l], moe_router[j], moe_w1, moe_w3, moe_w2, j)

    st = lambda k: jnp.stack(outs[k])
    return (xp.reshape(batch, seq, d_model), xs.reshape(db, 1, d_model), st("cmp_p"), st("cmp_s"), st("sel_p"),
            st("sel_s"), st("win_p"), st("win_s"), st("ssm_p"), st("ssm_s"))
```

```python
import functools
import math

import numpy as np
import jax
import jax.numpy as jnp
from jax import lax
from jax.experimental import pallas as pl
from jax.experimental.pallas import tpu as pltpu

HEAD_DIM = 128
N_HEADS = 16
N_KV_HEADS = 4
GROUP = N_HEADS // N_KV_HEADS
BLOCK = 64
N_SELECT = 16
WINDOW = 512
PAGE_SIZE = 128
GROUP_CH = 16
P_STATE = 64
N_EXPERTS = 8
TOP_K = 2
EPS = 1.0e-6
FORCED_SCORE = 1.0e6
ALIBI_MAX_EXP = 8.0
SCALE = HEAD_DIM ** -0.5
SSM_CHUNK = 16
LANE = 128
SUBLANE = 8
VMEM_LIMIT = 56 << 20

_BF = jnp.bfloat16
_F32 = jnp.float32
_HI = lax.Precision.HIGHEST


def _tile(n, pref, mult=SUBLANE):
    if n <= pref:
        return n
    for t in range(pref - pref % mult, 0, -mult):
        if n % t == 0:
            return t
    return n


def _params(sem):
    return pltpu.CompilerParams(dimension_semantics=sem, vmem_limit_bytes=VMEM_LIMIT)


def _rms_kernel(x_ref, g_ref, o_ref):
    x = x_ref[...]
    inv = lax.rsqrt(jnp.mean(x * x, axis=-1, keepdims=True) + EPS)
    o_ref[...] = (x * inv * g_ref[...]).astype(o_ref.dtype)


def _rmsnorm(x, g, out_dtype=_BF):
    m, d = x.shape
    tm = _tile(m, 256)
    return pl.pallas_call(
        _rms_kernel,
        grid=(m // tm,),
        in_specs=[pl.BlockSpec((tm, d), lambda i: (i, 0)), pl.BlockSpec((1, d), lambda i: (0, 0))],
        out_specs=pl.BlockSpec((tm, d), lambda i: (i, 0)),
        out_shape=jax.ShapeDtypeStruct((m, d), out_dtype),
        compiler_params=_params(("arbitrary",)),
        name="rmsnorm",
    )(x, g.reshape(1, d))


def _sigmoid(x):
    return 1.0 / (1.0 + jnp.exp(-x))


def _mm_kernel(*refs, n_w, mode):
    x_ref, xs_ref = refs[0], refs[1]
    w_refs = refs[2:2 + n_w]
    p = 2 + n_w
    extra = ()
    if mode in ("headnorm", "resid"):
        extra = refs[p:p + 2]
        p += 2
    o_ref, os_ref = refs[p], refs[p + 1]
    wb_refs = refs[p + 2:p + 2 + n_w]
    i = pl.program_id(1)

    @pl.when(i == 0)
    def _():
        for w_ref, wb_ref in zip(w_refs, wb_refs):
            wb_ref[...] = w_ref[...].astype(_BF)

    def compute(x, out_ref, res):
        accs = [jnp.dot(x, wb[...], preferred_element_type=_F32) for wb in wb_refs]
        if mode == "plain":
            out_ref[...] = accs[0].astype(out_ref.dtype)
        elif mode == "swiglu":
            a = accs[0]
            out_ref[...] = ((a * _sigmoid(a)) * accs[1]).astype(out_ref.dtype)
        elif mode == "resid":
            out_ref[...] = (res + accs[0]).astype(out_ref.dtype)
        else:
            acc = accs[0]
            gain_ref, flag_ref = extra
            for c in range(acc.shape[1] // HEAD_DIM):
                sl = slice(c * HEAD_DIM, (c + 1) * HEAD_DIM)
                a = acc[:, sl]
                nrm = a * lax.rsqrt(jnp.mean(a * a, axis=-1, keepdims=True) + EPS) * gain_ref[:, sl]
                out_ref[:, sl] = jnp.where(flag_ref[:, sl] > 0.0, nrm, a).astype(out_ref.dtype)

    compute(x_ref[...], o_ref, extra[0][...] if mode == "resid" else None)

    @pl.when(i == 0)
    def _():
        compute(xs_ref[...], os_ref, extra[1][...] if mode == "resid" else None)


def _mm(x, xs, ws, n_cols, *, mode, out_dtype, tm, tn, gain=None, flag=None, res=None, res_s=None,
        name="mm"):
    m, k = x.shape
    ms = xs.shape[0]
    nj, ni = n_cols // tn, m // tm
    in_specs = [pl.BlockSpec((tm, k), lambda j, i: (i, 0)), pl.BlockSpec((ms, k), lambda j, i: (0, 0))]
    args = [x, xs]
    for w, lid, off in ws:
        in_specs.append(pl.BlockSpec((None, k, tn), functools.partial(lambda j, i, lid, off: (lid, 0, j + off),
                                                                       lid=lid, off=off)))
        args.append(w)
    if mode == "headnorm":
        in_specs += [pl.BlockSpec((1, tn), lambda j, i: (0, j)), pl.BlockSpec((1, tn), lambda j, i: (0, j))]
        args += [gain, flag]
    elif mode == "resid":
        in_specs += [pl.BlockSpec((tm, tn), lambda j, i: (i, j)), pl.BlockSpec((ms, tn), lambda j, i: (0, j))]
        args += [res, res_s]
    return pl.pallas_call(
        functools.partial(_mm_kernel, n_w=len(ws), mode=mode),
        grid=(nj, ni),
        in_specs=in_specs,
        out_specs=[pl.BlockSpec((tm, tn), lambda j, i: (i, j)), pl.BlockSpec((ms, tn), lambda j, i: (0, j))],
        out_shape=[jax.ShapeDtypeStruct((m, n_cols), out_dtype), jax.ShapeDtypeStruct((ms, n_cols), out_dtype)],
        scratch_shapes=[pltpu.VMEM((k, tn), _BF) for _ in ws],
        compiler_params=_params(("arbitrary", "arbitrary")),
        name=name,
    )(*args)


def _merge_kernel(oa_ref, oas_ref, y_ref, ys_ref, wa_ref, w1_ref, w2_ref, ga_ref, gb_ref, gas_ref, gbs_ref,
                  o_ref, os_ref, wab, w1b, w2b):
    i = pl.program_id(1)

    @pl.when(i == 0)
    def _():
        wab[...] = wa_ref[...].astype(_BF)
        w1b[...] = w1_ref[...].astype(_BF)
        w2b[...] = w2_ref[...].astype(_BF)

    def compute(oa, y, ga, gb, out_ref):
        pa = jnp.dot(oa, wab[...], preferred_element_type=_F32)
        ab1 = jnp.dot(y, w1b[...], preferred_element_type=_F32)
        ab2 = jnp.dot(y, w2b[...], preferred_element_type=_F32)
        pb = ab1 * _sigmoid(ab2)
        out_ref[...] = (_sigmoid(ga) * pa + _sigmoid(gb) * pb).astype(out_ref.dtype)

    compute(oa_ref[...], y_ref[...], ga_ref[...], gb_ref[...], o_ref)

    @pl.when(i == 0)
    def _():
        compute(oas_ref[...], ys_ref[...], gas_ref[...], gbs_ref[...], os_ref)


def _merge(oa, oas, y, ys, w_attn_out, w_glu, layer, zug, zug_s, d_model, d_ssm, *, tm, tn):
    m, ka = oa.shape
    ks = y.shape[1]
    ms = oas.shape[0]
    nj, ni = d_model // tn, m // tm
    off_a = d_ssm // tn
    off_b = (d_ssm + d_model) // tn
    nglu = d_model // tn
    return pl.pallas_call(
        _merge_kernel,
        grid=(nj, ni),
        in_specs=[
            pl.BlockSpec((tm, ka), lambda j, i: (i, 0)), pl.BlockSpec((ms, ka), lambda j, i: (0, 0)),
            pl.BlockSpec((tm, ks), lambda j, i: (i, 0)), pl.BlockSpec((ms, ks), lambda j, i: (0, 0)),
            pl.BlockSpec((None, ka, tn), lambda j, i: (layer, 0, j)),
            pl.BlockSpec((None, ks, tn), lambda j, i: (layer, 0, j)),
            pl.BlockSpec((None, ks, tn), lambda j, i: (layer, 0, j + nglu)),
            pl.BlockSpec((tm, tn), lambda j, i: (i, j + off_a)), pl.BlockSpec((tm, tn), lambda j, i: (i, j + off_b)),
            pl.BlockSpec((ms, tn), lambda j, i: (0, j + off_a)), pl.BlockSpec((ms, tn), lambda j, i: (0, j + off_b)),
        ],
        out_specs=[pl.BlockSpec((tm, tn), lambda j, i: (i, j)), pl.BlockSpec((ms, tn), lambda j, i: (0, j))],
        out_shape=[jax.ShapeDtypeStruct((m, d_model), _BF), jax.ShapeDtypeStruct((ms, d_model), _BF)],
        scratch_shapes=[pltpu.VMEM((ka, tn), _BF), pltpu.VMEM((ks, tn), _BF), pltpu.VMEM((ks, tn), _BF)],
        compiler_params=_params(("arbitrary", "arbitrary")),
        name="merge",
    )(oa, oas, y, ys, w_attn_out, w_glu, w_glu, zug, zug, zug_s, zug_s)


def _down_kernel(a_ref, as_ref, w_ref, res_ref, ress_ref, o_ref, os_ref):
    k = pl.program_id(2)
    wb = w_ref[...].astype(_BF)
    part = jnp.dot(a_ref[...], wb, preferred_element_type=_F32)

    @pl.when(k == 0)
    def _():
        o_ref[...] = res_ref[...] + part

    @pl.when(k > 0)
    def _():
        o_ref[...] += part

    part_s = jnp.dot(as_ref[...], wb, preferred_element_type=_F32)

    @pl.when(k == 0)
    def _():
        os_ref[...] = ress_ref[...] + part_s

    @pl.when(k > 0)
    def _():
        os_ref[...] += part_s


def _ffn_down(a, a_s, w2, layer, res, res_s, *, tm, tn, tk):
    m, f = a.shape
    ms = a_s.shape[0]
    n = w2.shape[2]
    ni = m // tm
    out, out_s = pl.pallas_call(
        _down_kernel,
        grid=(ni, n // tn, f // tk),
        in_specs=[
            pl.BlockSpec((tm, tk), lambda i, j, k: (i, k)), pl.BlockSpec((ms, tk), lambda i, j, k: (0, k)),
            pl.BlockSpec((None, tk, tn), lambda i, j, k: (layer, k, j)),
            pl.BlockSpec((tm, tn), lambda i, j, k: (i, j)), pl.BlockSpec((ms, tn), lambda i, j, k: (0, j)),
        ],
        out_specs=[pl.BlockSpec((tm, tn), lambda i, j, k: (i, j)),
                   pl.BlockSpec((None, ms, tn), lambda i, j, k: (i, 0, j))],
        out_shape=[jax.ShapeDtypeStruct((m, n), _F32), jax.ShapeDtypeStruct((ni, ms, n), _F32)],
        compiler_params=_params(("arbitrary", "arbitrary", "arbitrary")),
        name="ffn_down",
    )(a, a_s, w2, res, res_s)
    return out, out_s[0]


def _masked_softmax(s, mask):
    s = jnp.where(mask, s, -jnp.inf)
    m = jnp.max(s, axis=-1, keepdims=True)
    m = jnp.where(jnp.isfinite(m), m, 0.0)
    p = jnp.where(mask, jnp.exp(s - m), 0.0)
    den = jnp.maximum(jnp.sum(p, axis=-1, keepdims=True), 1e-30)
    return p * (1.0 / den)


def _dot_t(a, b, precision=None):
    return lax.dot_general(a, b, (((1,), (1,)), ((), ())), precision=precision, preferred_element_type=_F32)


def _nsa_prompt_kernel(slope_ref, q_ref, kc_ref, vc_ref, ks_ref, vs_ref, kw_ref, vw_ref, gate_ref,
                       pk_ref, pv_ref, e_ref, o_ref, kcp, vcp, ksb, vsb, kwb, vwb, osel, *, tq, seq, nb, band):
    kh = pl.program_id(1)
    qi = pl.program_id(2)

    @pl.when(qi == 0)
    def _():
        kcp[...] = jnp.dot(pk_ref[...].astype(_BF), kc_ref[...].astype(_BF),
                           preferred_element_type=_F32).astype(_BF)
        vcp[...] = jnp.dot(pv_ref[...].astype(_BF), vc_ref[...].astype(_BF),
                           preferred_element_type=_F32).astype(_BF)
        ksb[...] = ks_ref[...].astype(_BF)
        vsb[...] = vs_ref[...].astype(_BF)
        kwb[...] = kw_ref[...].astype(_BF)
        vwb[...] = vw_ref[...].astype(_BF)

    t0 = qi * tq
    qpos = t0 + lax.broadcasted_iota(jnp.int32, (tq, 1), 0)
    qpos_f = qpos.astype(_F32)
    q = q_ref[...]
    gate = _sigmoid(gate_ref[...])

    blk = lax.broadcasted_iota(jnp.int32, (tq, nb), 1)
    center = (blk * BLOCK).astype(_F32) + 0.5 * (BLOCK - 1)
    valid_c = ((blk + 1) * BLOCK - 1) <= qpos
    dist_c = qpos_f - center
    kc = kcp[...]
    vcb = vcp[...]
    o_cmp = []
    imp = jnp.zeros((tq, nb), _F32)
    for g in range(GROUP):
        slope = slope_ref[kh * GROUP + g]
        qg = q[:, g * HEAD_DIM:(g + 1) * HEAD_DIM].astype(_BF)
        s = _dot_t(qg, kc) * SCALE - slope * dist_c
        p = _masked_softmax(s, valid_c)
        imp = imp + p
        o_cmp.append(jnp.dot(p.astype(_BF), vcb, preferred_element_type=_F32))

    cur = qpos // BLOCK
    forced = (blk == cur) | (blk == cur - 1) | (blk == 0)
    score = jnp.where(forced, FORCED_SCORE, jnp.where(blk > cur, -jnp.inf, imp))
    rank = jnp.zeros((tq, nb), _F32)
    for i in range(nb):
        col = score[:, i:i + 1]
        beats = (col > score) | ((col == score) & (blk > i))
        rank = rank + jnp.where(beats, 1.0, 0.0)
    sel = jnp.where(rank < float(min(N_SELECT, nb)), 1.0, 0.0).astype(_BF)
    sel_keys = jnp.dot(sel, e_ref[...], preferred_element_type=_F32) > 0.5

    kpos = lax.broadcasted_iota(jnp.int32, (1, seq), 1)
    dist_s = qpos - kpos
    mask_s = sel_keys & (dist_s >= 0)
    dist_sf = dist_s.astype(_F32)

    ws = pl.multiple_of(jnp.maximum(t0 - WINDOW, 0), math.gcd(tq, WINDOW))
    kpos_w = ws + lax.broadcasted_iota(jnp.int32, (1, band), 1)
    dist_w = qpos - kpos_w
    mask_w = (dist_w >= 0) & (dist_w < WINDOW)
    dist_wf = dist_w.astype(_F32)
    kw_band = kwb[pl.ds(ws, band), :]
    vw_band = vwb[pl.ds(ws, band), :]

    nq = seq // tq
    n_var = 4 if nq % 4 == 0 else 1
    for v in range(n_var):
        klen = seq * (v + 1) // n_var

        @pl.when((qi * n_var) // nq == v)
        def _(klen=klen):
            for g in range(GROUP):
                slope = slope_ref[kh * GROUP + g]
                qg = q[:, g * HEAD_DIM:(g + 1) * HEAD_DIM].astype(_BF)
                s = _dot_t(qg, ksb[:klen, :]) * SCALE - slope * dist_sf[:, :klen]
                p = _masked_softmax(s, mask_s[:, :klen])
                osel[:, g * HEAD_DIM:(g + 1) * HEAD_DIM] = jnp.dot(p.astype(_BF), vsb[:klen, :],
                                                                   preferred_element_type=_F32)

    for g in range(GROUP):
        slope = slope_ref[kh * GROUP + g]
        qg = q[:, g * HEAD_DIM:(g + 1) * HEAD_DIM].astype(_BF)
        o_sel = osel[:, g * HEAD_DIM:(g + 1) * HEAD_DIM]
        s = _dot_t(qg, kw_band) * SCALE - slope * dist_wf
        p = _masked_softmax(s, mask_w)
        o_win = jnp.dot(p.astype(_BF), vw_band, preferred_element_type=_F32)
        o = (gate[:, g:g + 1] * o_cmp[g] + gate[:, GROUP + g:GROUP + g + 1] * o_sel
             + gate[:, 2 * GROUP + g:2 * GROUP + g + 1] * o_win)
        o_ref[:, g * HEAD_DIM:(g + 1) * HEAD_DIM] = o.astype(o_ref.dtype)


def _alibi_slopes():
    h = np.arange(1, N_HEADS + 1, dtype=np.float32)
    return jnp.asarray(np.exp2(-ALIBI_MAX_EXP * h / N_HEADS).astype(np.float32))


def _nsa_prompt(zqkv, gates_r, pool_k, pool_v, expand, batch, seq):
    nb = seq // BLOCK
    tq = _tile(seq, 256)
    nq = seq // tq
    band = min(WINDOW + tq, seq)
    d_attn = N_HEADS * HEAD_DIM
    kv0 = d_attn // HEAD_DIM

    def kv_spec(branch, which):
        base = kv0 + (branch * 2 + which) * N_KV_HEADS
        return pl.BlockSpec((seq, HEAD_DIM), lambda b, k, qi, sl: (b, base + k))

    grid_spec = pltpu.PrefetchScalarGridSpec(
        num_scalar_prefetch=1,
        grid=(batch, N_KV_HEADS, nq),
        in_specs=[
            pl.BlockSpec((tq, GROUP * HEAD_DIM), lambda b, k, qi, sl: (b * nq + qi, k)),
            kv_spec(0, 0), kv_spec(0, 1), kv_spec(1, 0), kv_spec(1, 1), kv_spec(2, 0), kv_spec(2, 1),
            pl.BlockSpec((None, tq, 3 * GROUP), lambda b, k, qi, sl: (k, b * nq + qi, 0)),
            pl.BlockSpec((None, nb, seq), lambda b, k, qi, sl: (k, 0, 0)),
            pl.BlockSpec((None, nb, seq), lambda b, k, qi, sl: (k, 0, 0)),
            pl.BlockSpec((nb, seq), lambda b, k, qi, sl: (0, 0)),
        ],
        out_specs=pl.BlockSpec((tq, GROUP * HEAD_DIM), lambda b, k, qi, sl: (b * nq + qi, k)),
        scratch_shapes=[pltpu.VMEM((nb, HEAD_DIM), _BF), pltpu.VMEM((nb, HEAD_DIM), _BF)]
        + [pltpu.VMEM((seq, HEAD_DIM), _BF) for _ in range(4)] + [pltpu.VMEM((tq, GROUP * HEAD_DIM), _F32)],
    )
    return pl.pallas_call(
        functools.partial(_nsa_prompt_kernel, tq=tq, seq=seq, nb=nb, band=band),
        grid_spec=grid_spec,
        out_shape=jax.ShapeDtypeStruct((batch * seq, d_attn), _BF),
        compiler_params=_params(("arbitrary", "arbitrary", "arbitrary")),
        name="nsa_prompt",
    )(_alibi_slopes(), zqkv, zqkv, zqkv, zqkv, zqkv, zqkv, zqkv, gates_r, pool_k, pool_v, expand)


def _pool_pages_kernel(pt_ref, *refs, pages):
    w_ref, o_ref = refs[pages], refs[pages + 1]
    w = w_ref[...].astype(_BF).astype(_F32)
    per = PAGE_SIZE // BLOCK
    for r in range(pages):
        x = refs[r][...].astype(_BF).astype(_F32)
        for n in range(per):
            o_ref[r * per + n] = jnp.sum(x[n * BLOCK:(n + 1) * BLOCK] * w, axis=0).astype(_BF).astype(_F32)


def _pool_pages(cache, page_table, w_exp, layer):
    db, n_pages = page_table.shape
    per = PAGE_SIZE // BLOCK
    row = cache.shape[3:]
    pages = math.gcd(n_pages, 8)

    def page_spec(r):
        return pl.BlockSpec((None, None, PAGE_SIZE) + row,
                            lambda b, p, pt: (layer, pt[b * n_pages + p * pages + r], 0, 0, 0))

    grid_spec = pltpu.PrefetchScalarGridSpec(
        num_scalar_prefetch=1,
        grid=(db, n_pages // pages),
        in_specs=[page_spec(r) for r in range(pages)] + [pl.BlockSpec((BLOCK,) + row, lambda b, p, pt: (0, 0, 0))],
        out_specs=pl.BlockSpec((None, pages * per) + row, lambda b, p, pt: (b, p, 0, 0)),
    )
    return pl.pallas_call(
        functools.partial(_pool_pages_kernel, pages=pages),
        grid_spec=grid_spec,
        out_shape=jax.ShapeDtypeStruct((db, n_pages * per) + row, _F32),
        compiler_params=_params(("arbitrary", "arbitrary")),
        name="pool_pages",
    )(page_table.reshape(-1), *([cache] * pages), w_exp)


def _cmp_select_kernel(slope_ref, q_ref, pool_ref, new_ref, w0_ref, oc_ref, idx_ref, *, past, n_sel):
    nbp = pool_ref.shape[0]
    qpos = past
    q = q_ref[...]
    pooled_new = new_ref[...] * w0_ref[...]
    blk = lax.broadcasted_iota(jnp.int32, (1, nbp), 1)
    center = (blk * BLOCK).astype(_F32) + 0.5 * (BLOCK - 1)
    valid = ((blk + 1) * BLOCK - 1) <= qpos
    dist = float(qpos) - center
    center_n = float(nbp * BLOCK) + 0.5 * (BLOCK - 1)
    valid_n = jnp.full((1, 1), (nbp + 1) * BLOCK - 1 <= qpos)
    dist_n = float(qpos) - center_n
    rows_i = lax.broadcasted_iota(jnp.int32, (nbp, nbp), 0)
    cols_i = lax.broadcasted_iota(jnp.int32, (nbp, nbp), 1)
    sel_r = lax.broadcasted_iota(jnp.int32, (n_sel, nbp), 0).astype(_F32)
    blk_f = blk.astype(_F32)
    cur = qpos // BLOCK
    for kh in range(N_KV_HEADS):
        kc = pool_ref[:, (2 * kh) * HEAD_DIM:(2 * kh + 1) * HEAD_DIM]
        vc = pool_ref[:, (2 * kh + 1) * HEAD_DIM:(2 * kh + 2) * HEAD_DIM]
        kn = pooled_new[:, (2 * kh) * HEAD_DIM:(2 * kh + 1) * HEAD_DIM]
        vn = pooled_new[:, (2 * kh + 1) * HEAD_DIM:(2 * kh + 2) * HEAD_DIM]
        qk = q[kh * GROUP:(kh + 1) * GROUP]
        slope = jnp.concatenate([jnp.full((1, 1), slope_ref[kh * GROUP + g], _F32) for g in range(GROUP)], axis=0)
        s = _dot_t(qk.astype(_BF), kc.astype(_BF)) * SCALE - slope * dist
        s_n = jnp.sum(qk * kn, axis=-1, keepdims=True) * SCALE - slope * dist_n
        s = jnp.where(valid, s, -jnp.inf)
        s_n = jnp.where(valid_n, s_n, -jnp.inf)
        m = jnp.maximum(jnp.max(s, axis=-1, keepdims=True), s_n)
        m = jnp.where(jnp.isfinite(m), m, 0.0)
        p = jnp.where(valid, jnp.exp(s - m), 0.0)
        p_n = jnp.where(valid_n, jnp.exp(s_n - m), 0.0)
        den = jnp.maximum(jnp.sum(p, axis=-1, keepdims=True) + p_n, 1e-30)
        inv = 1.0 / den
        p = p * inv
        p_n = p_n * inv
        oc_ref[kh * GROUP:(kh + 1) * GROUP, :] = (
            jnp.dot(p.astype(_BF), vc.astype(_BF), preferred_element_type=_F32) + p_n * vn)
        imp = jnp.sum(p, axis=0, keepdims=True)
        imp_n = jnp.sum(p_n, axis=0, keepdims=True)
        forced = (blk == cur) | (blk == cur - 1) | (blk == 0)
        score = jnp.where(forced, FORCED_SCORE, jnp.where(blk > cur, -jnp.inf, imp))
        forced_n = (nbp == cur) | (nbp == cur - 1) | (nbp == 0)
        score_n = jnp.where(forced_n, FORCED_SCORE, jnp.where(nbp > cur, -jnp.inf, imp_n))
        sj = jnp.broadcast_to(score, (nbp, nbp))
        si = jnp.transpose(sj)
        ahead = (si > sj) | ((si == sj) & (rows_i < cols_i))
        rank = jnp.sum(jnp.where(ahead, 1.0, 0.0), axis=0, keepdims=True) + jnp.where(score_n > score, 1.0, 0.0)
        rank_n = jnp.sum(jnp.where(score >= score_n, 1.0, 0.0), axis=-1, keepdims=True)
        onehot = jnp.where(rank == sel_r, 1.0, 0.0)
        sel_col = lax.broadcasted_iota(jnp.int32, (n_sel, 1), 0).astype(_F32)
        idx = jnp.sum(onehot * blk_f, axis=-1, keepdims=True) + jnp.where(rank_n == sel_col, float(nbp), 0.0)
        idx_ref[kh] = jnp.broadcast_to(idx, (n_sel, LANE)).astype(jnp.int32)


def _cmp_select(q_s, pooled, new_rows, w0, past, n_sel):
    db, nbp, width = pooled.shape
    grid_spec = pltpu.PrefetchScalarGridSpec(
        num_scalar_prefetch=1,
        grid=(db,),
        in_specs=[
            pl.BlockSpec((None, N_HEADS, HEAD_DIM), lambda b, sl: (b, 0, 0)),
            pl.BlockSpec((None, nbp, width), lambda b, sl: (b, 0, 0)),
            pl.BlockSpec((None, 1, width), lambda b, sl: (b, 0, 0)),
            pl.BlockSpec((1, width), lambda b, sl: (0, 0)),
        ],
        out_specs=[
            pl.BlockSpec((None, N_HEADS, HEAD_DIM), lambda b, sl: (b, 0, 0)),
            pl.BlockSpec((None, N_KV_HEADS, n_sel, LANE), lambda b, sl: (b, 0, 0, 0)),
        ],
    )
    return pl.pallas_call(
        functools.partial(_cmp_select_kernel, past=past, n_sel=n_sel),
        grid_spec=grid_spec,
        out_shape=[jax.ShapeDtypeStruct((db, N_HEADS, HEAD_DIM), _F32),
                   jax.ShapeDtypeStruct((db, N_KV_HEADS, n_sel, LANE), jnp.int32)],
        compiler_params=_params(("arbitrary",)),
        name="cmp_select",
    )(_alibi_slopes(), q_s, pooled, new_rows, w0)


def _sel_sample_kernel(idx_ref, pt_ref, slope_ref, q_ref, *refs, past, n_sel):
    blk_refs, new_ref, o_ref, rows_ref = refs[:n_sel], refs[n_sel], refs[n_sel + 1], refs[n_sel + 2]
    b = pl.program_id(0)
    kh = pl.program_id(1)
    base = (b * N_KV_HEADS + kh) * n_sel
    rw = 2 * N_KV_HEADS
    row_i = lax.broadcasted_iota(jnp.int32, (BLOCK, 1, 1), 0)
    for jj in range(n_sel):
        kpos_c = idx_ref[base + jj] * BLOCK + row_i
        rows_ref[jj * BLOCK:(jj + 1) * BLOCK] = jnp.where(kpos_c < past, blk_refs[jj][...], new_ref[...])
    qpos = past
    n_rows = n_sel * BLOCK * rw
    rows = rows_ref[...].reshape(n_rows, HEAD_DIM).astype(_BF)
    col = lax.broadcasted_iota(jnp.int32, (1, n_rows), 1)
    pos = col // rw
    blk_of = jnp.zeros((1, n_rows), jnp.int32)
    for jj in range(n_sel):
        blk_of = jnp.where(pos // BLOCK == jj, idx_ref[base + jj], blk_of)
    dist = qpos - (blk_of * BLOCK + pos % BLOCK)
    mask = (dist >= 0) & (col % rw == 2 * kh)
    slope = jnp.concatenate([jnp.full((1, 1), slope_ref[kh * GROUP + g], _F32) for g in range(GROUP)], axis=0)
    s = _dot_t(q_ref[...].astype(_BF), rows) * SCALE - slope * dist.astype(_F32)
    p = _masked_softmax(s, mask)
    p_val = pltpu.roll(p, 1, 1)
    o_ref[...] = jnp.dot(p_val.astype(_BF), rows, preferred_element_type=_F32)


def _sel_sample(q_s, cache, idx, page_table, new_rows, layer, past, n_sel):
    db, n_pages = page_table.shape
    rw = 2 * N_KV_HEADS

    def blk_spec(j):
        def blk_map(b, k, idx_ref, pt_ref, sl):
            blk = idx_ref[(b * N_KV_HEADS + k) * n_sel + j]
            pos = jnp.minimum(blk * BLOCK, past - 1)
            return (layer, pt_ref[b * n_pages + pos // PAGE_SIZE], (pos % PAGE_SIZE) // BLOCK, 0, 0)
        return pl.BlockSpec((None, None, BLOCK, rw, HEAD_DIM), blk_map)

    grid_spec = pltpu.PrefetchScalarGridSpec(
        num_scalar_prefetch=3,
        grid=(db, N_KV_HEADS),
        in_specs=[pl.BlockSpec((None, GROUP, HEAD_DIM), lambda b, k, *_: (b * N_KV_HEADS + k, 0, 0))]
        + [blk_spec(j) for j in range(n_sel)]
        + [pl.BlockSpec((None, 1, rw, HEAD_DIM), lambda b, k, *_: (b, 0, 0, 0))],
        out_specs=pl.BlockSpec((None, GROUP, HEAD_DIM), lambda b, k, *_: (b * N_KV_HEADS + k, 0, 0)),
        scratch_shapes=[pltpu.VMEM((n_sel * BLOCK, rw, HEAD_DIM), _F32)],
    )
    return pl.pallas_call(
        functools.partial(_sel_sample_kernel, past=past, n_sel=n_sel),
        grid_spec=grid_spec,
        out_shape=jax.ShapeDtypeStruct((db * N_KV_HEADS, GROUP, HEAD_DIM), _F32),
        compiler_params=_params(("arbitrary", "arbitrary")),
        name="sel_sample",
    )(idx, page_table.reshape(-1), _alibi_slopes(), q_s, *([cache] * n_sel), new_rows)


def _win_combine_kernel(slope_ref, q_ref, win_ref, new_ref, oc_ref, os_ref, gate_ref, o_ref, *, past, w_buf):
    qpos = past
    q = q_ref[...]
    gate = _sigmoid(gate_ref[...])
    kpos = past - w_buf + lax.broadcasted_iota(jnp.int32, (1, w_buf), 1)
    dist = qpos - kpos
    mask = (dist >= 0) & (dist < WINDOW) & (kpos >= 0)
    dist_f = dist.astype(_F32)
    new = new_ref[...]
    mask_n = jnp.full((1, 1), 0 < WINDOW)
    for kh in range(N_KV_HEADS):
        kw = win_ref[:, (2 * kh) * HEAD_DIM:(2 * kh + 1) * HEAD_DIM].astype(_BF)
        vw = win_ref[:, (2 * kh + 1) * HEAD_DIM:(2 * kh + 2) * HEAD_DIM].astype(_BF)
        kn = new[:, (2 * kh) * HEAD_DIM:(2 * kh + 1) * HEAD_DIM]
        vn = new[:, (2 * kh + 1) * HEAD_DIM:(2 * kh + 2) * HEAD_DIM]
        hs = slice(kh * GROUP, (kh + 1) * GROUP)
        qk = q[hs]
        slope = jnp.concatenate([jnp.full((1, 1), slope_ref[kh * GROUP + g], _F32) for g in range(GROUP)], axis=0)
        s = _dot_t(qk.astype(_BF), kw) * SCALE - slope * dist_f
        qb = qk.astype(_BF).astype(_F32)
        s_n = jnp.sum(qb * kn.astype(_BF).astype(_F32), axis=-1, keepdims=True) * SCALE
        s = jnp.where(mask, s, -jnp.inf)
        s_n = jnp.where(mask_n, s_n, -jnp.inf)
        m = jnp.maximum(jnp.max(s, axis=-1, keepdims=True), s_n)
        m = jnp.where(jnp.isfinite(m), m, 0.0)
        p = jnp.where(mask, jnp.exp(s - m), 0.0)
        p_n = jnp.where(mask_n, jnp.exp(s_n - m), 0.0)
        inv = 1.0 / jnp.maximum(jnp.sum(p, axis=-1, keepdims=True) + p_n, 1e-30)
        o_w = (jnp.dot((p * inv).astype(_BF), vw, preferred_element_type=_F32)
               + (p_n * inv).astype(_BF).astype(_F32) * vn.astype(_BF).astype(_F32))
        g3 = gate[hs]
        o_ref[hs, :] = g3[:, 0:1] * oc_ref[hs, :] + g3[:, 1:2] * os_ref[hs, :] + g3[:, 2:3] * o_w


def _win_combine(q_s, win, new_rows, o_cmp, o_sel, gates, past):
    db, w_buf, width = win.shape
    head = pl.BlockSpec((None, N_HEADS, HEAD_DIM), lambda b, sl: (b, 0, 0))
    grid_spec = pltpu.PrefetchScalarGridSpec(
        num_scalar_prefetch=1,
        grid=(db,),
        in_specs=[
            head,
            pl.BlockSpec((None, w_buf, width), lambda b, sl: (b, 0, 0)),
            pl.BlockSpec((None, 1, width), lambda b, sl: (b, 0, 0)),
            head, head,
            pl.BlockSpec((None, N_HEADS, 3), lambda b, sl: (b, 0, 0)),
        ],
        out_specs=head,
    )
    return pl.pallas_call(
        functools.partial(_win_combine_kernel, past=past, w_buf=w_buf),
        grid_spec=grid_spec,
        out_shape=jax.ShapeDtypeStruct((db, N_HEADS, HEAD_DIM), _F32),
        compiler_params=_params(("arbitrary",)),
        name="win_combine",
    )(_alibi_slopes(), q_s, win, new_rows, o_cmp, o_sel, gates)


def _ssm_discretise(a_re, a_im, log_dt, b_re, b_im):
    dt = jnp.exp(log_dt)[:, None]
    mag = jnp.exp(dt * a_re)
    abar_re = mag * jnp.cos(dt * a_im)
    abar_im = mag * jnp.sin(dt * a_im)
    den = a_re * a_re + a_im * a_im
    cf_re = ((abar_re - 1.0) * a_re + abar_im * a_im) / den
    cf_im = (abar_im * a_re - (abar_re - 1.0) * a_im) / den
    bb_re = cf_re[..., None] * b_re - cf_im[..., None] * b_im
    bb_im = cf_re[..., None] * b_im + cf_im[..., None] * b_re
    return abar_re, abar_im, bb_re, bb_im


def _ssm_chunk_matrices(abar_re, abar_im, bb_re, bb_im, c_re, c_im):
    L = SSM_CHUNK
    g, p, ch = bb_re.shape
    nq = LANE // ch
    nblk = g // nq
    pw_re = [jnp.ones_like(abar_re)]
    pw_im = [jnp.zeros_like(abar_im)]
    for _ in range(L):
        r, i = pw_re[-1], pw_im[-1]
        pw_re.append(r * abar_re - i * abar_im)
        pw_im.append(r * abar_im + i * abar_re)
    pw_re = jnp.stack(pw_re, axis=1)
    pw_im = jnp.stack(pw_im, axis=1)
    bt_re = bb_re.transpose(0, 2, 1)[:, None]
    bt_im = bb_im.transpose(0, 2, 1)[:, None]
    e_re = pw_re[:, :L, None, :] * bt_re - pw_im[:, :L, None, :] * bt_im
    e_im = pw_re[:, :L, None, :] * bt_im + pw_im[:, :L, None, :] * bt_re
    k_lag = (jnp.einsum("gcp,gldp->gldc", c_re, e_re, precision=_HI)
             - jnp.einsum("gcp,gldp->gldc", c_im, e_im, precision=_HI))
    eye = jnp.eye(nq, dtype=_F32)
    bd = jnp.einsum("bqldc,qr->blqdrc", k_lag.reshape(nblk, nq, L, ch, ch), eye).reshape(nblk, L, LANE, LANE)

    def state_in(e):
        eb = e[:, ::-1].reshape(nblk, nq, L, ch, p)
        return jnp.einsum("bqjdp,qr->bjqdrp", eb, eye).reshape(nblk, L * LANE, nq * p)

    pr = pw_re[:, 1:, None, :]
    pi = pw_im[:, 1:, None, :]

    def state_out_t(v):
        vb = v.reshape(nblk, nq, L, ch, p)
        return jnp.einsum("brtcp,rq->btrcqp", vb, eye).reshape(nblk, L * LANE, nq * p)

    vt_re = state_out_t(c_re[:, None] * pr - c_im[:, None] * pi)
    vt_im = state_out_t(-(c_re[:, None] * pi + c_im[:, None] * pr))
    al_re = pw_re[:, L].reshape(nblk, 1, nq * p)
    al_im = pw_im[:, L].reshape(nblk, 1, nq * p)
    return (bd.astype(_BF), state_in(e_re).astype(_BF), state_in(e_im).astype(_BF),
            vt_re.astype(_BF), vt_im.astype(_BF), al_re, al_im)


def _ssm_prompt_kernel(u_ref, bd_ref, wre_ref, wim_ref, vtre_ref, vtim_ref, alre_ref, alim_ref, d_ref,
                       y_ref, hre_ref, him_ref, msc, xs, ys, loc_re, loc_im, hp_re, hp_im, *, n_chunk):
    L = SSM_CHUNK

    @pl.when(pl.program_id(1) == 0)
    def _():
        for j in range(L):
            for t in range(L):
                blk = bd_ref[t - j] if t >= j else jnp.zeros((LANE, LANE), _BF)
                msc[j * LANE:(j + 1) * LANE, t * LANE:(t + 1) * LANE] = blk

    for j in range(L):
        xs[:, j * LANE:(j + 1) * LANE] = u_ref[pl.ds(j, n_chunk, stride=L), :].astype(_BF)
    xb = xs[...]
    y = jnp.dot(xb, msc[...], preferred_element_type=_F32)
    loc_re[...] = jnp.dot(xb, wre_ref[...], preferred_element_type=_F32)
    loc_im[...] = jnp.dot(xb, wim_ref[...], preferred_element_type=_F32)
    a_re = alre_ref[...]
    a_im = alim_ref[...]

    def step(c, carry):
        h_re, h_im = carry
        hp_re[pl.ds(c, 1), :] = h_re
        hp_im[pl.ds(c, 1), :] = h_im
        l_re = loc_re[pl.ds(c, 1), :]
        l_im = loc_im[pl.ds(c, 1), :]
        return (a_re * h_re - a_im * h_im + l_re, a_re * h_im + a_im * h_re + l_im)

    zero = jnp.zeros(a_re.shape, _F32)
    h_re, h_im = lax.fori_loop(0, n_chunk, step, (zero, zero))
    y = y + _dot_t(hp_re[...].astype(_BF), vtre_ref[...])
    y = y + _dot_t(hp_im[...].astype(_BF), vtim_ref[...])
    d = d_ref[...]
    for t in range(L):
        x_t = u_ref[pl.ds(t, n_chunk, stride=L), :]
        ys[pl.ds(t, n_chunk, stride=L), :] = y[:, t * LANE:(t + 1) * LANE] + d * x_t
    y_ref[...] = ys[...].astype(y_ref.dtype)
    hre_ref[...] = h_re
    him_ref[...] = h_im


def _ssm_prompt(zug, mats, d_skip, batch, seq, d_ssm):
    bd, wre, wim, vtre, vtim, al_re, al_im = mats
    L = SSM_CHUNK
    nblk = bd.shape[0]
    nq = LANE // GROUP_CH
    n_chunk = seq // L
    sw = nq * P_STATE
    per_blk = lambda shape: pl.BlockSpec((None,) + shape, lambda g, b: (g, 0, 0))
    fin_spec = pl.BlockSpec((None, None, 1, sw), lambda g, b: (g, b, 0, 0))
    y, h_re, h_im = pl.pallas_call(
        functools.partial(_ssm_prompt_kernel, n_chunk=n_chunk),
        grid=(nblk, batch),
        in_specs=[pl.BlockSpec((seq, LANE), lambda g, b: (b, g)),
                  pl.BlockSpec((None, L, LANE, LANE), lambda g, b: (g, 0, 0, 0)),
                  per_blk((L * LANE, sw)), per_blk((L * LANE, sw)), per_blk((L * LANE, sw)), per_blk((L * LANE, sw)),
                  per_blk((1, sw)), per_blk((1, sw)), per_blk((1, LANE))],
        out_specs=[pl.BlockSpec((seq, LANE), lambda g, b: (b, g)), fin_spec, fin_spec],
        out_shape=[jax.ShapeDtypeStruct((batch * seq, d_ssm), _BF),
                   jax.ShapeDtypeStruct((nblk, batch, 1, sw), _F32),
                   jax.ShapeDtypeStruct((nblk, batch, 1, sw), _F32)],
        scratch_shapes=[pltpu.VMEM((L * LANE, L * LANE), _BF), pltpu.VMEM((n_chunk, L * LANE), _BF),
                        pltpu.VMEM((seq, LANE), _F32)]
        + [pltpu.VMEM((n_chunk, sw), _F32) for _ in range(4)],
        compiler_params=_params(("arbitrary", "arbitrary")),
        name="ssm_prompt",
    )(zug, bd, wre, wim, vtre, vtim, al_re, al_im, d_skip.reshape(nblk, 1, LANE))
    fin = lambda h: h.reshape(nblk, batch, nq, P_STATE).transpose(1, 0, 2, 3).reshape(batch, nblk * nq, P_STATE)
    return y, jnp.stack([fin(h_re), fin(h_im)], axis=-1)


def _ssm_sample_kernel(xe_ref, bre_ref, bim_ref, h0r_ref, h0i_ref, ar_ref, ai_ref, cre_ref, cim_ref, seg_ref,
                       xt_ref, d_ref, y_ref, sre_ref, sim_ref):
    ch = xe_ref.shape[0]
    h0r, h0i = h0r_ref[...], h0i_ref[...]
    a_re, a_im = ar_ref[...], ai_ref[...]
    rnd = lambda v: v.astype(_BF).astype(_F32)
    bu_re = rnd(xe_ref[0]) * rnd(bre_ref[0])
    bu_im = rnd(xe_ref[0]) * rnd(bim_ref[0])
    for c in range(1, ch):
        bu_re = bu_re + rnd(xe_ref[c]) * rnd(bre_ref[c])
        bu_im = bu_im + rnd(xe_ref[c]) * rnd(bim_ref[c])
    s_re = bu_re + (a_re * h0r - a_im * h0i)
    s_im = bu_im + (a_re * h0i + a_im * h0r)
    sre_ref[...] = s_re
    sim_ref[...] = s_im
    seg = seg_ref[...]
    sb_re, sb_im = rnd(s_re), rnd(s_im)
    for c in range(ch):
        y_re = jnp.dot(sb_re * rnd(cre_ref[c]), seg, precision=_HI, preferred_element_type=_F32)
        y_im = jnp.dot(sb_im * rnd(cim_ref[c]), seg, precision=_HI, preferred_element_type=_F32)
        y_ref[c] = y_re - y_im + d_ref[c] * xt_ref[c]


def _ssm_sample(u_s, h0, abar_re, abar_im, bb_re, bb_im, c_re, c_im, d_skip):
    db = u_s.shape[0]
    g, p, ch = bb_re.shape
    gp = g * p
    x3 = u_s.reshape(db, g, ch)
    xe = jnp.broadcast_to(x3.transpose(2, 0, 1)[..., None], (ch, db, g, p)).reshape(ch, db, gp)
    flat_b = lambda b: b.transpose(2, 0, 1).reshape(ch, 1, gp)
    flat_c = lambda c: c.transpose(1, 0, 2).reshape(ch, 1, gp)
    seg = jnp.asarray(np.repeat(np.eye(g, dtype=np.float32), p, axis=0))
    xt = x3.transpose(2, 0, 1)
    dt = d_skip.transpose(1, 0).reshape(ch, 1, g)
    y, s_re, s_im = pl.pallas_call(
        _ssm_sample_kernel,
        out_shape=[jax.ShapeDtypeStruct((ch, db, g), _F32), jax.ShapeDtypeStruct((db, gp), _F32),
                   jax.ShapeDtypeStruct((db, gp), _F32)],
        compiler_params=pltpu.CompilerParams(vmem_limit_bytes=VMEM_LIMIT),
        name="ssm_sample",
    )(xe, flat_b(bb_re), flat_b(bb_im), h0[..., 0].reshape(db, gp), h0[..., 1].reshape(db, gp),
      abar_re.reshape(1, gp), abar_im.reshape(1, gp), flat_c(c_re), flat_c(c_im), seg, xt, dt)
    y = y.transpose(1, 2, 0).reshape(db, g * ch)
    h_fin = jnp.stack([s_re.reshape(db, g, p), s_im.reshape(db, g, p)], axis=-1)
    return y, h_fin


def _router_kernel(x_ref, g_ref, wr_ref, idx_ref, gate_ref):
    x = x_ref[...]
    h = x * lax.rsqrt(jnp.mean(x * x, axis=-1, keepdims=True) + EPS) * g_ref[...]
    logits = jnp.dot(h.astype(_BF), wr_ref[...].astype(_BF), preferred_element_type=_F32)
    lane = lax.broadcasted_iota(jnp.int32, logits.shape, 1)
    lane_f = lane.astype(_F32)
    lg = jnp.where(lane < N_EXPERTS, logits, -jnp.inf)
    v1 = jnp.max(lg, axis=-1, keepdims=True)
    i1 = jnp.min(jnp.where(lg == v1, lane_f, float(LANE)), axis=-1, keepdims=True)
    lg2 = jnp.where(lane_f == i1, -jnp.inf, lg)
    v2 = jnp.max(lg2, axis=-1, keepdims=True)
    i2 = jnp.min(jnp.where(lg2 == v2, lane_f, float(LANE)), axis=-1, keepdims=True)
    e = jnp.exp(v2 - v1)
    den = 1.0 + e
    idx_ref[...] = jnp.where(lane == 0, i1, jnp.where(lane == 1, i2, 0.0)).astype(jnp.int32)
    gate_ref[...] = jnp.where(lane == 0, 1.0 / den, jnp.where(lane == 1, e / den, 0.0))


def _router(x, g, wr_pad):
    n, d = x.shape
    tm = _tile(n, 256)
    return pl.pallas_call(
        _router_kernel,
        grid=(n // tm,),
        in_specs=[pl.BlockSpec((tm, d), lambda i: (i, 0)), pl.BlockSpec((1, d), lambda i: (0, 0)),
                  pl.BlockSpec((d, LANE), lambda i: (0, 0))],
        out_specs=[pl.BlockSpec((tm, LANE), lambda i: (i, 0)), pl.BlockSpec((tm, LANE), lambda i: (i, 0))],
        out_shape=[jax.ShapeDtypeStruct((n, LANE), jnp.int32), jax.ShapeDtypeStruct((n, LANE), _F32)],
        compiler_params=_params(("arbitrary",)),
        name="router",
    )(x, g.reshape(1, d), wr_pad)


def _row_copy(src_hbm, row, dst, slot, sem):
    return pltpu.make_async_copy(src_hbm.at[pl.ds(row, 1)], dst.at[pl.ds(slot, 1)], sem)


def _moe_gather_kernel(tok_ref, x_hbm, g_ref, o_ref, buf, sem, *, tm):
    def start(r, c):
        _row_copy(x_hbm, tok_ref[0, 0, r], buf, r, sem).start()
        return c

    lax.fori_loop(0, tm, start, 0)

    def wait(r, c):
        _row_copy(x_hbm, 0, buf, r, sem).wait()
        return c

    lax.fori_loop(0, tm, wait, 0)
    x = buf[...]
    inv = lax.rsqrt(jnp.mean(x * x, axis=-1, keepdims=True) + EPS)
    o_ref[...] = (x * inv * g_ref[...]).astype(o_ref.dtype)


def _moe_gather(x_all, g, row_tok, tm):
    n_rows = row_tok.shape[0]
    d = x_all.shape[1]
    n_tiles = n_rows // tm
    return pl.pallas_call(
        functools.partial(_moe_gather_kernel, tm=tm),
        grid=(n_tiles,),
        in_specs=[pl.BlockSpec((1, 1, tm), lambda i: (i, 0, 0), memory_space=pltpu.SMEM),
                  pl.BlockSpec(memory_space=pl.ANY),
                  pl.BlockSpec((1, d), lambda i: (0, 0))],
        out_specs=pl.BlockSpec((tm, d), lambda i: (i, 0)),
        out_shape=jax.ShapeDtypeStruct((n_rows, d), _BF),
        scratch_shapes=[pltpu.VMEM((tm, d), _F32), pltpu.SemaphoreType.DMA(())],
        compiler_params=_params(("arbitrary",)),
        name="moe_gather",
    )(row_tok.reshape(n_tiles, 1, tm), x_all, g.reshape(1, d))


def _moe_up_kernel(texp_ref, nused_ref, x_ref, w1_ref, w3_ref, o_ref, w1b, w3b):
    i = pl.program_id(1)
    n_used = nused_ref[0]
    prev = texp_ref[jnp.maximum(i - 1, 0)]
    fresh = (i == 0) | (texp_ref[i] != prev)

    @pl.when((i < n_used) & fresh)
    def _():
        w1b[...] = w1_ref[...].astype(_BF)
        w3b[...] = w3_ref[...].astype(_BF)

    @pl.when(i < n_used)
    def _():
        x = x_ref[...]
        a = jnp.dot(x, w1b[...], preferred_element_type=_F32)
        b = jnp.dot(x, w3b[...], preferred_element_type=_F32)
        o_ref[...] = ((a * _sigmoid(a)) * b).astype(o_ref.dtype)

    @pl.when(i >= n_used)
    def _():
        o_ref[...] = jnp.zeros(o_ref.shape, o_ref.dtype)


def _moe_up(xg, w1, w3, layer, tile_exp, n_used, *, tm, tf):
    n_rows, d = xg.shape
    f = w1.shape[3]
    n_tiles = n_rows // tm

    def live(i, nu):
        return jnp.minimum(i, nu[0] - 1)

    grid_spec = pltpu.PrefetchScalarGridSpec(
        num_scalar_prefetch=2,
        grid=(f // tf, n_tiles),
        in_specs=[
            pl.BlockSpec((tm, d), lambda j, i, te, nu: (live(i, nu), 0)),
            pl.BlockSpec((None, None, d, tf), lambda j, i, te, nu: (layer, te[live(i, nu)], 0, j)),
            pl.BlockSpec((None, None, d, tf), lambda j, i, te, nu: (layer, te[live(i, nu)], 0, j)),
        ],
        out_specs=pl.BlockSpec((tm, tf), lambda j, i, te, nu: (i, j)),
        scratch_shapes=[pltpu.VMEM((d, tf), _BF), pltpu.VMEM((d, tf), _BF)],
    )
    return pl.pallas_call(
        _moe_up_kernel,
        grid_spec=grid_spec,
        out_shape=jax.ShapeDtypeStruct((n_rows, f), _BF),
        compiler_params=_params(("arbitrary", "arbitrary")),
        name="moe_up",
    )(tile_exp, n_used, xg, w1, w3)


def _moe_down_kernel(texp_ref, nused_ref, a_ref, w_ref, o_ref):
    i = pl.program_id(0)
    k = pl.program_id(1)

    @pl.when(i < nused_ref[0])
    def _():
        part = jnp.dot(a_ref[...], w_ref[...].astype(_BF), preferred_element_type=_F32)

        @pl.when(k == 0)
        def _():
            o_ref[...] = part

        @pl.when(k > 0)
        def _():
            o_ref[...] += part

    @pl.when((i >= nused_ref[0]) & (k == 0))
    def _():
        o_ref[...] = jnp.zeros(o_ref.shape, o_ref.dtype)


def _moe_down(a, w2, layer, tile_exp, n_used, *, tm, tk):
    n_rows, f = a.shape
    d = w2.shape[3]
    n_tiles = n_rows // tm
    nk = f // tk

    def live(i, nu):
        return jnp.minimum(i, nu[0] - 1)

    def kk(i, k, nu):
        return jnp.where(i < nu[0], k, nk - 1)

    grid_spec = pltpu.PrefetchScalarGridSpec(
        num_scalar_prefetch=2,
        grid=(n_tiles, nk),
        in_specs=[
            pl.BlockSpec((tm, tk), lambda i, k, te, nu: (live(i, nu), kk(i, k, nu))),
            pl.BlockSpec((None, None, tk, d), lambda i, k, te, nu: (layer, te[live(i, nu)], kk(i, k, nu), 0)),
        ],
        out_specs=pl.BlockSpec((tm, d), lambda i, k, te, nu: (i, 0)),
    )
    return pl.pallas_call(
        _moe_down_kernel,
        grid_spec=grid_spec,
        out_shape=jax.ShapeDtypeStruct((n_rows, d), _F32),
        compiler_params=_params(("arbitrary", "arbitrary")),
        name="moe_down",
    )(tile_exp, n_used, a, w2)


def _moe_combine_kernel(pos_ref, x_ref, gate_ref, y_hbm, o_ref, b0, b1, sems, *, tm):
    def start(r, c):
        _row_copy(y_hbm, pos_ref[0, 0, 2 * r], b0, r, sems.at[0]).start()
        _row_copy(y_hbm, pos_ref[0, 0, 2 * r + 1], b1, r, sems.at[1]).start()
        return c

    lax.fori_loop(0, tm, start, 0)

    def wait(r, c):
        _row_copy(y_hbm, 0, b0, r, sems.at[0]).wait()
        _row_copy(y_hbm, 0, b1, r, sems.at[1]).wait()
        return c

    lax.fori_loop(0, tm, wait, 0)
    gate = gate_ref[...]
    o_ref[...] = x_ref[...] + (gate[:, 0:1] * b0[...] + gate[:, 1:2] * b1[...])


def _moe_combine(x, gates, pos, y_rows):
    n, d = x.shape
    tm = _tile(n, 256)
    n_tiles = n // tm
    return pl.pallas_call(
        functools.partial(_moe_combine_kernel, tm=tm),
        grid=(n_tiles,),
        in_specs=[pl.BlockSpec((1, 1, 2 * tm), lambda i: (i, 0, 0), memory_space=pltpu.SMEM),
                  pl.BlockSpec((tm, d), lambda i: (i, 0)),
                  pl.BlockSpec((tm, LANE), lambda i: (i, 0)),
                  pl.BlockSpec(memory_space=pl.ANY)],
        out_specs=pl.BlockSpec((tm, d), lambda i: (i, 0)),
        out_shape=jax.ShapeDtypeStruct((n, d), _F32),
        scratch_shapes=[pltpu.VMEM((tm, d), _F32), pltpu.VMEM((tm, d), _F32), pltpu.SemaphoreType.DMA((2,))],
        compiler_params=_params(("arbitrary",)),
        name="moe_combine",
    )(pos.reshape(n_tiles, 1, 2 * tm), x, gates, y_rows)


def _moe_layer(xp, xs, g, w_router, w1, w3, w2, layer):
    n_p, d = xp.shape
    n_s = xs.shape[0]
    n = n_p + n_s
    wr_pad = jnp.pad(w_router, ((0, 0), (0, LANE - N_EXPERTS)))
    idx_p, gate_p = _router(xp, g, wr_pad)
    idx_s, gate_s = _router(xs, g, wr_pad)
    top_i = jnp.concatenate([idx_p[:, :TOP_K], idx_s[:, :TOP_K]], axis=0)
    n_asg = n * TOP_K
    tm = 512 if n_asg >= 4096 else 64
    n_tiles = -(-n_asg // tm) + N_EXPERTS
    e_flat = top_i.reshape(n_asg)
    onehot = (e_flat[:, None] == jnp.arange(N_EXPERTS)[None, :]).astype(jnp.int32)
    within = jnp.sum((jnp.cumsum(onehot, axis=0) - onehot) * onehot, axis=1)
    counts = jnp.sum(onehot, axis=0)
    padded = (counts + tm - 1) // tm * tm
    pad_end = jnp.cumsum(padded)
    dest = (pad_end - padded)[e_flat] + within
    row_tok = jnp.zeros((n_tiles * tm,), jnp.int32).at[dest].set(jnp.arange(n_asg, dtype=jnp.int32) // TOP_K)
    tile_exp = jnp.minimum(jnp.searchsorted(pad_end, jnp.arange(n_tiles) * tm, side="right"),
                           N_EXPERTS - 1).astype(jnp.int32)
    n_used = (pad_end[-1:] // tm).astype(jnp.int32)
    pos = dest.reshape(n, TOP_K).astype(jnp.int32)

    x_all = jnp.concatenate([xp, xs], axis=0)
    xg = _moe_gather(x_all, g, row_tok, tm)
    f = w1.shape[3]
    a = _moe_up(xg, w1, w3, layer, tile_exp, n_used, tm=tm, tf=_tile(f, 512, LANE))
    y_rows = _moe_down(a, w2, layer, tile_exp, n_used, tm=tm, tk=_tile(f, 512, LANE))
    out_p = _moe_combine(xp, gate_p, pos[:n_p], y_rows)
    out_s = _moe_combine(xs, gate_s, pos[n_p:], y_rows)
    return out_p, out_s


def kernel(x_prompt, x_sample, cache_kv_cmp, cache_kv_sel, cache_kv_win, state_ssm, page_table, norm_mix, norm_ffn, w_in, g_q, g_k, w_cmp, w_attn_out, ssm_a_re, ssm_a_im, ssm_log_dt, ssm_b_re, ssm_b_im, ssm_c_re, ssm_c_im, ssm_d, w_glu, w_out, ffn_w1, ffn_w3, ffn_w2, moe_router, moe_w1, moe_w3, moe_w2):
    batch, seq, d_model = x_prompt.shape
    db, dec_seq, _ = x_sample.shape
    assert dec_seq == 1, "the sample group advances one token per step"
    depth = w_in.shape[0]
    n_pool = cache_kv_cmp.shape[1]
    n_pages = page_table.shape[1]
    past = n_pages * PAGE_SIZE
    d_attn = N_HEADS * HEAD_DIM
    d_kv = N_KV_HEADS * HEAD_DIM
    d_ssm = ssm_d.shape[1] * GROUP_CH
    n_gate = 3 * N_HEADS
    off_gn = d_attn + 6 * d_kv
    off_u = off_gn + n_gate
    n_ug = d_ssm + 2 * d_model
    row_w = N_KV_HEADS * 2 * HEAD_DIM
    nb = seq // BLOCK
    n_p = batch * seq

    tm = _tile(n_p, 1024)
    tn = 512
    assert off_gn % tn == 0 and n_ug % tn == 0 and d_ssm % tn == 0 and d_model % tn == 0

    xp = x_prompt.reshape(n_p, d_model)
    xs = x_sample.reshape(db, d_model)
    w_ug = w_in[:, :, off_u:]
    flag_cols = np.zeros((1, off_gn), np.float32)
    flag_cols[:, :d_attn] = 1.0
    for br in range(3):
        lo = d_attn + br * 2 * d_kv
        flag_cols[:, lo:lo + d_kv] = 1.0
    flag_cols = jnp.asarray(flag_cols)
    expand = jnp.asarray(np.repeat(np.eye(nb, dtype=np.float32), BLOCK, axis=1)).astype(_BF)
    eye_nb = jnp.asarray(np.eye(nb, dtype=np.float32))
    n_sel = min(N_SELECT, past // BLOCK + 1)

    outs = {k: [] for k in ("cmp_p", "cmp_s", "sel_p", "sel_s", "win_p", "win_s", "ssm_p", "ssm_s")}
    for l in range(depth):
        hp = _rmsnorm(xp, norm_mix[l])
        hs = _rmsnorm(xs, norm_mix[l])
        gain = jnp.concatenate([jnp.tile(g_q[l], N_HEADS)]
                               + [jnp.concatenate([jnp.tile(g_k[l, br], N_KV_HEADS), jnp.ones((d_kv,), _F32)])
                                  for br in range(3)]).reshape(1, off_gn)
        zqkv, zqkv_s = _mm(hp, hs, [(w_in, l, 0)], off_gn, mode="headnorm", out_dtype=_F32, tm=tm, tn=tn,
                           gain=gain, flag=flag_cols, name="in_proj_qkv")
        zg, zg_s = _mm(hp, hs, [(w_in, l, off_gn // LANE)], LANE, mode="plain", out_dtype=_F32, tm=tm, tn=LANE,
                       name="in_proj_gate")
        zug, zug_s = _mm(hp, hs, [(w_ug, l, 0)], n_ug, mode="plain", out_dtype=_F32, tm=tm, tn=tn,
                         name="in_proj_ug")

        def kv_rows(z, lead):
            r = z[:, d_attn:off_gn].reshape(lead + (3, 2, N_KV_HEADS, HEAD_DIM))
            return jnp.swapaxes(r, -3, -2)

        rows_p = kv_rows(zqkv, (batch, seq))
        rows_s = kv_rows(zqkv_s, (db, 1))
        outs["cmp_p"].append(rows_p[:, :, 0])
        outs["sel_p"].append(rows_p[:, :, 1])
        outs["win_p"].append(rows_p[:, seq - min(WINDOW, seq):, 2])
        outs["cmp_s"].append(rows_s[:, :, 0])
        outs["sel_s"].append(rows_s[:, :, 1])

        gates_r = zg[:, :n_gate].reshape(n_p, 3, N_KV_HEADS, GROUP).transpose(2, 0, 1, 3).reshape(
            N_KV_HEADS, n_p, 3 * GROUP)
        wk = w_cmp[l].transpose(0, 2, 1)
        pool = (eye_nb[None, None, :, :, None] * wk[:, :, None, None, :]).reshape(2, N_KV_HEADS, nb, seq)
        o_attn = _nsa_prompt(zqkv, gates_r, pool[0], pool[1], expand, batch, seq)

        q_s = zqkv_s[:, :d_attn].reshape(db, N_HEADS, HEAD_DIM)
        tile_view = lambda c: c.reshape(depth, n_pool, PAGE_SIZE, 2 * N_KV_HEADS, HEAD_DIM)
        w_exp = jnp.broadcast_to(w_cmp[l].transpose(1, 2, 0)[..., None],
                                 (BLOCK, N_KV_HEADS, 2, HEAD_DIM)).reshape(BLOCK, 2 * N_KV_HEADS, HEAD_DIM)
        pooled = _pool_pages(tile_view(cache_kv_cmp), page_table, w_exp, l).reshape(db, past // BLOCK, row_w)
        new_c = rows_s[:, :, 0].reshape(db, 1, row_w)
        o_cmp_s, idx = _cmp_select(q_s, pooled, new_c, w_exp[:1].reshape(1, row_w), past, n_sel)
        idx = idx[..., 0].reshape(-1)
        o_sel_s = _sel_sample(q_s.reshape(db * N_KV_HEADS, GROUP, HEAD_DIM), tile_view(cache_kv_sel),
                              idx, page_table, rows_s[:, :, 1].reshape(db, 1, 2 * N_KV_HEADS, HEAD_DIM),
                              l, past, n_sel)
        win = cache_kv_win[l]
        w_buf = win.shape[1]
        new_w = rows_s[:, :, 2]
        gates_s = zg_s[:, :n_gate].reshape(db, 3, N_HEADS).transpose(0, 2, 1)
        o_attn_s = _win_combine(q_s, win.reshape(db, w_buf, row_w), new_w.reshape(db, 1, row_w),
                                o_cmp_s, o_sel_s.reshape(db, N_HEADS, HEAD_DIM), gates_s, past)
        outs["win_s"].append(jnp.concatenate([win, new_w], axis=1)[:, 1:])

        abar_re, abar_im, bb_re, bb_im = _ssm_discretise(ssm_a_re[l], ssm_a_im[l], ssm_log_dt[l],
                                                         ssm_b_re[l], ssm_b_im[l])
        mats = _ssm_chunk_matrices(abar_re, abar_im, bb_re, bb_im, ssm_c_re[l], ssm_c_im[l])
        y_p, h_p = _ssm_prompt(zug, mats, ssm_d[l], batch, seq, d_ssm)
        y_s, h_s = _ssm_sample(zug_s[:, :d_ssm], state_ssm[l], abar_re, abar_im, bb_re, bb_im,
                               ssm_c_re[l], ssm_c_im[l], ssm_d[l])
        outs["ssm_p"].append(h_p)
        outs["ssm_s"].append(h_s)

        mg, mg_s = _merge(o_attn, o_attn_s.reshape(db, d_attn).astype(_BF), y_p, y_s.astype(_BF),
                          w_attn_out, w_glu, l, zug, zug_s, d_model, d_ssm, tm=_tile(n_p, 512), tn=tn)
        xp, xs = _mm(mg, mg_s, [(w_out, l, 0)], d_model, mode="resid", out_dtype=_F32, tm=tm, tn=tn,
                     res=xp, res_s=xs, name="out_proj")

        if l % 2 == 0:
            j = l // 2
            hp = _rmsnorm(xp, norm_ffn[l])
            hs = _rmsnorm(xs, norm_ffn[l])
            f = ffn_w1.shape[2]
            a, a_s = _mm(hp, hs, [(ffn_w1, j, 0), (ffn_w3, j, 0)], f, mode="swiglu", out_dtype=_BF,
                         tm=_tile(n_p, 512), tn=_tile(f, 512, LANE), name="ffn_up")
            xp, xs = _ffn_down(a, a_s, ffn_w2, j, xp, xs, tm=_tile(n_p, 1024), tn=_tile(d_model, 2048, LANE),
                               tk=_tile(f, 512, LANE))
        else:
            j = l // 2
            xp, xs = _moe_layer(xp, xs, norm_ffn[l], moe_router[j], moe_w1, moe_w3, moe_w2, j)

    st = lambda k: jnp.stack(outs[k])
    return (xp.reshape(batch, seq, d_model), xs.reshape(db, 1, d_model), st("cmp_p"), st("cmp_s"), st("sel_p"),
            st("sel_s"), st("win_p"), st("win_s"), st("ssm_p"), st("ssm_s"))
```

```python
import functools
import math

import numpy as np
import jax
import jax.numpy as jnp
from jax import lax
from jax.experimental import pallas as pl
from jax.experimental.pallas import tpu as pltpu

HEAD_DIM = 128
N_HEADS = 16
N_KV_HEADS = 4
GROUP = N_HEADS // N_KV_HEADS
BLOCK = 64
N_SELECT = 16
WINDOW = 512
PAGE_SIZE = 128
GROUP_CH = 16
P_STATE = 64
N_EXPERTS = 8
TOP_K = 2
EPS = 1.0e-6
FORCED_SCORE = 1.0e6
ALIBI_MAX_EXP = 8.0
SCALE = HEAD_DIM ** -0.5
LOG2E = 1.4426950408889634
SSM_CHUNK = 16
LANE = 128
SUBLANE = 8
VMEM_LIMIT = 56 << 20

_BF = jnp.bfloat16
_F32 = jnp.float32
_HI = lax.Precision.HIGHEST


def _tile(n, pref, mult=SUBLANE):
    if n <= pref:
        return n
    for t in range(pref - pref % mult, 0, -mult):
        if n % t == 0:
            return t
    return n


def _params(sem):
    return pltpu.CompilerParams(dimension_semantics=sem, vmem_limit_bytes=VMEM_LIMIT)


def _rms_kernel(x_ref, g_ref, o_ref):
    x = x_ref[...]
    inv = lax.rsqrt(jnp.mean(x * x, axis=-1, keepdims=True) + EPS)
    o_ref[...] = (x * inv * g_ref[...]).astype(o_ref.dtype)


def _rmsnorm(x, g, out_dtype=_BF):
    m, d = x.shape
    tm = _tile(m, 256)
    return pl.pallas_call(
        _rms_kernel,
        grid=(m // tm,),
        in_specs=[pl.BlockSpec((tm, d), lambda i: (i, 0)), pl.BlockSpec((1, d), lambda i: (0, 0))],
        out_specs=pl.BlockSpec((tm, d), lambda i: (i, 0)),
        out_shape=jax.ShapeDtypeStruct((m, d), out_dtype),
        compiler_params=_params(("arbitrary",)),
        name="rmsnorm",
    )(x, g.reshape(1, d))


def _sigmoid(x):
    return 1.0 / (1.0 + jnp.exp(-x))


def _mm_kernel(*refs, n_w, mode):
    x_ref, xs_ref = refs[0], refs[1]
    w_refs = refs[2:2 + n_w]
    p = 2 + n_w
    extra = ()
    if mode in ("headnorm", "resid"):
        extra = refs[p:p + 2]
        p += 2
    o_ref, os_ref = refs[p], refs[p + 1]
    wb_refs = refs[p + 2:p + 2 + n_w]
    i = pl.program_id(1)

    @pl.when(i == 0)
    def _():
        for w_ref, wb_ref in zip(w_refs, wb_refs):
            wb_ref[...] = w_ref[...].astype(_BF)

    def compute(x, out_ref, res):
        accs = [jnp.dot(x, wb[...], preferred_element_type=_F32) for wb in wb_refs]
        if mode == "plain":
            out_ref[...] = accs[0].astype(out_ref.dtype)
        elif mode == "swiglu":
            a = accs[0]
            out_ref[...] = ((a * _sigmoid(a)) * accs[1]).astype(out_ref.dtype)
        elif mode == "resid":
            out_ref[...] = (res + accs[0]).astype(out_ref.dtype)
        else:
            acc = accs[0]
            gain_ref, flag_ref = extra
            for c in range(acc.shape[1] // HEAD_DIM):
                sl = slice(c * HEAD_DIM, (c + 1) * HEAD_DIM)
                a = acc[:, sl]
                nrm = a * lax.rsqrt(jnp.mean(a * a, axis=-1, keepdims=True) + EPS) * gain_ref[:, sl]
                out_ref[:, sl] = jnp.where(flag_ref[:, sl] > 0.0, nrm, a).astype(out_ref.dtype)

    compute(x_ref[...], o_ref, extra[0][...] if mode == "resid" else None)

    @pl.when(i == 0)
    def _():
        compute(xs_ref[...], os_ref, extra[1][...] if mode == "resid" else None)


def _mm(x, xs, ws, n_cols, *, mode, out_dtype, tm, tn, gain=None, flag=None, res=None, res_s=None,
        name="mm"):
    m, k = x.shape
    ms = xs.shape[0]
    nj, ni = n_cols // tn, m // tm
    in_specs = [pl.BlockSpec((tm, k), lambda j, i: (i, 0)), pl.BlockSpec((ms, k), lambda j, i: (0, 0))]
    args = [x, xs]
    for w, lid, off in ws:
        in_specs.append(pl.BlockSpec((None, k, tn), functools.partial(lambda j, i, lid, off: (lid, 0, j + off),
                                                                       lid=lid, off=off)))
        args.append(w)
    if mode == "headnorm":
        in_specs += [pl.BlockSpec((1, tn), lambda j, i: (0, j)), pl.BlockSpec((1, tn), lambda j, i: (0, j))]
        args += [gain, flag]
    elif mode == "resid":
        in_specs += [pl.BlockSpec((tm, tn), lambda j, i: (i, j)), pl.BlockSpec((ms, tn), lambda j, i: (0, j))]
        args += [res, res_s]
    return pl.pallas_call(
        functools.partial(_mm_kernel, n_w=len(ws), mode=mode),
        grid=(nj, ni),
        in_specs=in_specs,
        out_specs=[pl.BlockSpec((tm, tn), lambda j, i: (i, j)), pl.BlockSpec((ms, tn), lambda j, i: (0, j))],
        out_shape=[jax.ShapeDtypeStruct((m, n_cols), out_dtype), jax.ShapeDtypeStruct((ms, n_cols), out_dtype)],
        scratch_shapes=[pltpu.VMEM((k, tn), _BF) for _ in ws],
        compiler_params=_params(("arbitrary", "arbitrary")),
        name=name,
    )(*args)


def _merge_kernel(oa_ref, oas_ref, y_ref, ys_ref, wa_ref, w1_ref, w2_ref, ga_ref, gb_ref, gas_ref, gbs_ref,
                  o_ref, os_ref, wab, w1b, w2b):
    i = pl.program_id(1)

    @pl.when(i == 0)
    def _():
        wab[...] = wa_ref[...].astype(_BF)
        w1b[...] = w1_ref[...].astype(_BF)
        w2b[...] = w2_ref[...].astype(_BF)

    def compute(oa, y, ga, gb, out_ref):
        pa = jnp.dot(oa, wab[...], preferred_element_type=_F32)
        ab1 = jnp.dot(y, w1b[...], preferred_element_type=_F32)
        ab2 = jnp.dot(y, w2b[...], preferred_element_type=_F32)
        pb = ab1 * _sigmoid(ab2)
        out_ref[...] = (_sigmoid(ga) * pa + _sigmoid(gb) * pb).astype(out_ref.dtype)

    compute(oa_ref[...], y_ref[...], ga_ref[...], gb_ref[...], o_ref)

    @pl.when(i == 0)
    def _():
        compute(oas_ref[...], ys_ref[...], gas_ref[...], gbs_ref[...], os_ref)


def _merge(oa, oas, y, ys, w_attn_out, w_glu, layer, zug, zug_s, d_model, d_ssm, *, tm, tn):
    m, ka = oa.shape
    ks = y.shape[1]
    ms = oas.shape[0]
    nj, ni = d_model // tn, m // tm
    off_a = d_ssm // tn
    off_b = (d_ssm + d_model) // tn
    nglu = d_model // tn
    return pl.pallas_call(
        _merge_kernel,
        grid=(nj, ni),
        in_specs=[
            pl.BlockSpec((tm, ka), lambda j, i: (i, 0)), pl.BlockSpec((ms, ka), lambda j, i: (0, 0)),
            pl.BlockSpec((tm, ks), lambda j, i: (i, 0)), pl.BlockSpec((ms, ks), lambda j, i: (0, 0)),
            pl.BlockSpec((None, ka, tn), lambda j, i: (layer, 0, j)),
            pl.BlockSpec((None, ks, tn), lambda j, i: (layer, 0, j)),
            pl.BlockSpec((None, ks, tn), lambda j, i: (layer, 0, j + nglu)),
            pl.BlockSpec((tm, tn), lambda j, i: (i, j + off_a)), pl.BlockSpec((tm, tn), lambda j, i: (i, j + off_b)),
            pl.BlockSpec((ms, tn), lambda j, i: (0, j + off_a)), pl.BlockSpec((ms, tn), lambda j, i: (0, j + off_b)),
        ],
        out_specs=[pl.BlockSpec((tm, tn), lambda j, i: (i, j)), pl.BlockSpec((ms, tn), lambda j, i: (0, j))],
        out_shape=[jax.ShapeDtypeStruct((m, d_model), _BF), jax.ShapeDtypeStruct((ms, d_model), _BF)],
        scratch_shapes=[pltpu.VMEM((ka, tn), _BF), pltpu.VMEM((ks, tn), _BF), pltpu.VMEM((ks, tn), _BF)],
        compiler_params=_params(("arbitrary", "arbitrary")),
        name="merge",
    )(oa, oas, y, ys, w_attn_out, w_glu, w_glu, zug, zug, zug_s, zug_s)


def _down_kernel(a_ref, as_ref, w_ref, res_ref, ress_ref, o_ref, os_ref):
    k = pl.program_id(2)
    wb = w_ref[...].astype(_BF)
    part = jnp.dot(a_ref[...], wb, preferred_element_type=_F32)

    @pl.when(k == 0)
    def _():
        o_ref[...] = res_ref[...] + part

    @pl.when(k > 0)
    def _():
        o_ref[...] += part

    part_s = jnp.dot(as_ref[...], wb, preferred_element_type=_F32)

    @pl.when(k == 0)
    def _():
        os_ref[...] = ress_ref[...] + part_s

    @pl.when(k > 0)
    def _():
        os_ref[...] += part_s


def _ffn_down(a, a_s, w2, layer, res, res_s, *, tm, tn, tk):
    m, f = a.shape
    ms = a_s.shape[0]
    n = w2.shape[2]
    ni = m // tm
    out, out_s = pl.pallas_call(
        _down_kernel,
        grid=(ni, n // tn, f // tk),
        in_specs=[
            pl.BlockSpec((tm, tk), lambda i, j, k: (i, k)), pl.BlockSpec((ms, tk), lambda i, j, k: (0, k)),
            pl.BlockSpec((None, tk, tn), lambda i, j, k: (layer, k, j)),
            pl.BlockSpec((tm, tn), lambda i, j, k: (i, j)), pl.BlockSpec((ms, tn), lambda i, j, k: (0, j)),
        ],
        out_specs=[pl.BlockSpec((tm, tn), lambda i, j, k: (i, j)),
                   pl.BlockSpec((None, ms, tn), lambda i, j, k: (i, 0, j))],
        out_shape=[jax.ShapeDtypeStruct((m, n), _F32), jax.ShapeDtypeStruct((ni, ms, n), _F32)],
        compiler_params=_params(("arbitrary", "arbitrary", "arbitrary")),
        name="ffn_down",
    )(a, a_s, w2, res, res_s)
    return out, out_s[0]


def _masked_softmax(s, mask):
    s = jnp.where(mask, s, -jnp.inf)
    m = jnp.max(s, axis=-1, keepdims=True)
    m = jnp.where(jnp.isfinite(m), m, 0.0)
    p = jnp.where(mask, jnp.exp(s - m), 0.0)
    den = jnp.maximum(jnp.sum(p, axis=-1, keepdims=True), 1e-30)
    return p * (1.0 / den)


def _softmax_exp2(s2):
    m = jnp.max(s2, axis=-1, keepdims=True)
    m = jnp.where(jnp.isfinite(m), m, 0.0)
    p = jnp.exp2(s2 - m)
    return p, 1.0 / jnp.maximum(jnp.sum(p, axis=-1, keepdims=True), 1e-30)


def _dot_t(a, b, precision=None):
    return lax.dot_general(a, b, (((1,), (1,)), ((), ())), precision=precision, preferred_element_type=_F32)


def _nsa_prompt_kernel(slope_ref, q_ref, kc_ref, vc_ref, ks_ref, vs_ref, kw_ref, vw_ref, gate_ref,
                       pk_ref, pv_ref, e_ref, o_ref, kcp, vcp, ksb, vsb, kwb, vwb, osel, *, tq, seq, nb, band):
    kh = pl.program_id(1)
    qi = pl.program_id(2)

    @pl.when(qi == 0)
    def _():
        kcp[...] = jnp.dot(pk_ref[...].astype(_BF), kc_ref[...].astype(_BF),
                           preferred_element_type=_F32).astype(_BF)
        vcp[...] = jnp.dot(pv_ref[...].astype(_BF), vc_ref[...].astype(_BF),
                           preferred_element_type=_F32).astype(_BF)
        ksb[...] = ks_ref[...].astype(_BF)
        vsb[...] = vs_ref[...].astype(_BF)
        kwb[...] = kw_ref[...].astype(_BF)
        vwb[...] = vw_ref[...].astype(_BF)

    t0 = qi * tq
    qpos = t0 + lax.broadcasted_iota(jnp.int32, (tq, 1), 0)
    qpos_f = qpos.astype(_F32)
    q = q_ref[...]
    gate = _sigmoid(gate_ref[...])

    blk = lax.broadcasted_iota(jnp.int32, (tq, nb), 1)
    center = (blk * BLOCK).astype(_F32) + 0.5 * (BLOCK - 1)
    valid_c = ((blk + 1) * BLOCK - 1) <= qpos
    dist_c = qpos_f - center
    kc = kcp[...]
    vcb = vcp[...]
    o_cmp = []
    imp = jnp.zeros((tq, nb), _F32)
    for g in range(GROUP):
        slope = slope_ref[kh * GROUP + g]
        qg = q[:, g * HEAD_DIM:(g + 1) * HEAD_DIM].astype(_BF)
        s = _dot_t(qg, kc) * SCALE - slope * dist_c
        p = _masked_softmax(s, valid_c)
        imp = imp + p
        o_cmp.append(jnp.dot(p.astype(_BF), vcb, preferred_element_type=_F32))

    cur = qpos // BLOCK
    forced = (blk == cur) | (blk == cur - 1) | (blk == 0)
    score = jnp.where(forced, FORCED_SCORE, jnp.where(blk > cur, -jnp.inf, imp))
    rank = jnp.zeros((tq, nb), _F32)
    for i in range(nb):
        col = score[:, i:i + 1]
        beats = (col > score) | ((col == score) & (blk > i))
        rank = rank + jnp.where(beats, 1.0, 0.0)
    sel = jnp.where(rank < float(min(N_SELECT, nb)), 1.0, 0.0).astype(_BF)
    sel_keys = jnp.dot(sel, e_ref[...], preferred_element_type=_F32) > 0.5

    kpos = lax.broadcasted_iota(jnp.int32, (1, seq), 1)
    dist_s = qpos - kpos
    neg_s = jnp.where(sel_keys & (dist_s >= 0), 0.0, -jnp.inf)
    dist_sf = dist_s.astype(_F32)

    ws = pl.multiple_of(jnp.maximum(t0 - WINDOW, 0), math.gcd(tq, WINDOW))
    kpos_w = ws + lax.broadcasted_iota(jnp.int32, (1, band), 1)
    dist_w = qpos - kpos_w
    neg_w = jnp.where((dist_w >= 0) & (dist_w < WINDOW), 0.0, -jnp.inf)
    dist_wf = dist_w.astype(_F32)
    kw_band = kwb[pl.ds(ws, band), :]
    vw_band = vwb[pl.ds(ws, band), :]

    nq = seq // tq
    n_var = 4 if nq % 4 == 0 else 1
    for v in range(n_var):
        klen = seq * (v + 1) // n_var

        @pl.when((qi * n_var) // nq == v)
        def _(klen=klen):
            for g in range(GROUP):
                c2 = slope_ref[kh * GROUP + g] * LOG2E
                qg = q[:, g * HEAD_DIM:(g + 1) * HEAD_DIM].astype(_BF)
                s2 = _dot_t(qg, ksb[:klen, :]) * (SCALE * LOG2E) + (neg_s[:, :klen] - c2 * dist_sf[:, :klen])
                p, inv = _softmax_exp2(s2)
                osel[:, g * HEAD_DIM:(g + 1) * HEAD_DIM] = inv * jnp.dot(p.astype(_BF), vsb[:klen, :],
                                                                         preferred_element_type=_F32)

    for g in range(GROUP):
        c2 = slope_ref[kh * GROUP + g] * LOG2E
        qg = q[:, g * HEAD_DIM:(g + 1) * HEAD_DIM].astype(_BF)
        o_sel = osel[:, g * HEAD_DIM:(g + 1) * HEAD_DIM]
        s2 = _dot_t(qg, kw_band) * (SCALE * LOG2E) + (neg_w - c2 * dist_wf)
        p, inv = _softmax_exp2(s2)
        o_win = inv * jnp.dot(p.astype(_BF), vw_band, preferred_element_type=_F32)
        o = (gate[:, g:g + 1] * o_cmp[g] + gate[:, GROUP + g:GROUP + g + 1] * o_sel
             + gate[:, 2 * GROUP + g:2 * GROUP + g + 1] * o_win)
        o_ref[:, g * HEAD_DIM:(g + 1) * HEAD_DIM] = o.astype(o_ref.dtype)


def _alibi_slopes():
    h = np.arange(1, N_HEADS + 1, dtype=np.float32)
    return jnp.asarray(np.exp2(-ALIBI_MAX_EXP * h / N_HEADS).astype(np.float32))


def _nsa_prompt(zqkv, gates_r, pool_k, pool_v, expand, batch, seq):
    nb = seq // BLOCK
    tq = _tile(seq, 256)
    nq = seq // tq
    band = min(WINDOW + tq, seq)
    d_attn = N_HEADS * HEAD_DIM
    kv0 = d_attn // HEAD_DIM

    def kv_spec(branch, which):
        base = kv0 + (branch * 2 + which) * N_KV_HEADS
        return pl.BlockSpec((seq, HEAD_DIM), lambda b, k, qi, sl: (b, base + k))

    grid_spec = pltpu.PrefetchScalarGridSpec(
        num_scalar_prefetch=1,
        grid=(batch, N_KV_HEADS, nq),
        in_specs=[
            pl.BlockSpec((tq, GROUP * HEAD_DIM), lambda b, k, qi, sl: (b * nq + qi, k)),
            kv_spec(0, 0), kv_spec(0, 1), kv_spec(1, 0), kv_spec(1, 1), kv_spec(2, 0), kv_spec(2, 1),
            pl.BlockSpec((None, tq, 3 * GROUP), lambda b, k, qi, sl: (k, b * nq + qi, 0)),
            pl.BlockSpec((None, nb, seq), lambda b, k, qi, sl: (k, 0, 0)),
            pl.BlockSpec((None, nb, seq), lambda b, k, qi, sl: (k, 0, 0)),
            pl.BlockSpec((nb, seq), lambda b, k, qi, sl: (0, 0)),
        ],
        out_specs=pl.BlockSpec((tq, GROUP * HEAD_DIM), lambda b, k, qi, sl: (b * nq + qi, k)),
        scratch_shapes=[pltpu.VMEM((nb, HEAD_DIM), _BF), pltpu.VMEM((nb, HEAD_DIM), _BF)]
        + [pltpu.VMEM((seq, HEAD_DIM), _BF) for _ in range(4)] + [pltpu.VMEM((tq, GROUP * HEAD_DIM), _F32)],
    )
    return pl.pallas_call(
        functools.partial(_nsa_prompt_kernel, tq=tq, seq=seq, nb=nb, band=band),
        grid_spec=grid_spec,
        out_shape=jax.ShapeDtypeStruct((batch * seq, d_attn), _BF),
        compiler_params=_params(("arbitrary", "arbitrary", "arbitrary")),
        name="nsa_prompt",
    )(_alibi_slopes(), zqkv, zqkv, zqkv, zqkv, zqkv, zqkv, zqkv, gates_r, pool_k, pool_v, expand)


def _pool_pages_kernel(pt_ref, *refs, pages):
    w_ref, o_ref = refs[pages], refs[pages + 1]
    w = w_ref[...].astype(_BF).astype(_F32)
    per = PAGE_SIZE // BLOCK
    for r in range(pages):
        x = refs[r][...].astype(_BF).astype(_F32)
        for n in range(per):
            o_ref[r * per + n] = jnp.sum(x[n * BLOCK:(n + 1) * BLOCK] * w, axis=0).astype(_BF).astype(_F32)


def _pool_pages(cache, page_table, w_exp, layer):
    db, n_pages = page_table.shape
    per = PAGE_SIZE // BLOCK
    row = cache.shape[3:]
    pages = math.gcd(n_pages, 8)

    def page_spec(r):
        return pl.BlockSpec((None, None, PAGE_SIZE) + row,
                            lambda b, p, pt: (layer, pt[b * n_pages + p * pages + r], 0, 0, 0))

    grid_spec = pltpu.PrefetchScalarGridSpec(
        num_scalar_prefetch=1,
        grid=(db, n_pages // pages),
        in_specs=[page_spec(r) for r in range(pages)] + [pl.BlockSpec((BLOCK,) + row, lambda b, p, pt: (0, 0, 0))],
        out_specs=pl.BlockSpec((None, pages * per) + row, lambda b, p, pt: (b, p, 0, 0)),
    )
    return pl.pallas_call(
        functools.partial(_pool_pages_kernel, pages=pages),
        grid_spec=grid_spec,
        out_shape=jax.ShapeDtypeStruct((db, n_pages * per) + row, _F32),
        compiler_params=_params(("arbitrary", "arbitrary")),
        name="pool_pages",
    )(page_table.reshape(-1), *([cache] * pages), w_exp)


def _cmp_select_kernel(slope_ref, q_ref, pool_ref, new_ref, w0_ref, oc_ref, idx_ref, *, past, n_sel):
    nbp = pool_ref.shape[0]
    qpos = past
    q = q_ref[...]
    pooled_new = new_ref[...] * w0_ref[...]
    blk = lax.broadcasted_iota(jnp.int32, (1, nbp), 1)
    center = (blk * BLOCK).astype(_F32) + 0.5 * (BLOCK - 1)
    valid = ((blk + 1) * BLOCK - 1) <= qpos
    dist = float(qpos) - center
    center_n = float(nbp * BLOCK) + 0.5 * (BLOCK - 1)
    valid_n = jnp.full((1, 1), (nbp + 1) * BLOCK - 1 <= qpos)
    dist_n = float(qpos) - center_n
    rows_i = lax.broadcasted_iota(jnp.int32, (nbp, nbp), 0)
    cols_i = lax.broadcasted_iota(jnp.int32, (nbp, nbp), 1)
    sel_r = lax.broadcasted_iota(jnp.int32, (n_sel, nbp), 0).astype(_F32)
    blk_f = blk.astype(_F32)
    cur = qpos // BLOCK
    for kh in range(N_KV_HEADS):
        kc = pool_ref[:, (2 * kh) * HEAD_DIM:(2 * kh + 1) * HEAD_DIM]
        vc = pool_ref[:, (2 * kh + 1) * HEAD_DIM:(2 * kh + 2) * HEAD_DIM]
        kn = pooled_new[:, (2 * kh) * HEAD_DIM:(2 * kh + 1) * HEAD_DIM]
        vn = pooled_new[:, (2 * kh + 1) * HEAD_DIM:(2 * kh + 2) * HEAD_DIM]
        qk = q[kh * GROUP:(kh + 1) * GROUP]
        slope = jnp.concatenate([jnp.full((1, 1), slope_ref[kh * GROUP + g], _F32) for g in range(GROUP)], axis=0)
        s = _dot_t(qk.astype(_BF), kc.astype(_BF)) * SCALE - slope * dist
        s_n = jnp.sum(qk * kn, axis=-1, keepdims=True) * SCALE - slope * dist_n
        s = jnp.where(valid, s, -jnp.inf)
        s_n = jnp.where(valid_n, s_n, -jnp.inf)
        m = jnp.maximum(jnp.max(s, axis=-1, keepdims=True), s_n)
        m = jnp.where(jnp.isfinite(m), m, 0.0)
        p = jnp.where(valid, jnp.exp(s - m), 0.0)
        p_n = jnp.where(valid_n, jnp.exp(s_n - m), 0.0)
        den = jnp.maximum(jnp.sum(p, axis=-1, keepdims=True) + p_n, 1e-30)
        inv = 1.0 / den
        p = p * inv
        p_n = p_n * inv
        oc_ref[kh * GROUP:(kh + 1) * GROUP, :] = (
            jnp.dot(p.astype(_BF), vc.astype(_BF), preferred_element_type=_F32) + p_n * vn)
        imp = jnp.sum(p, axis=0, keepdims=True)
        imp_n = jnp.sum(p_n, axis=0, keepdims=True)
        forced = (blk == cur) | (blk == cur - 1) | (blk == 0)
        score = jnp.where(forced, FORCED_SCORE, jnp.where(blk > cur, -jnp.inf, imp))
        forced_n = (nbp == cur) | (nbp == cur - 1) | (nbp == 0)
        score_n = jnp.where(forced_n, FORCED_SCORE, jnp.where(nbp > cur, -jnp.inf, imp_n))
        sj = jnp.broadcast_to(score, (nbp, nbp))
        si = jnp.transpose(sj)
        ahead = (si > sj) | ((si == sj) & (rows_i < cols_i))
        rank = jnp.sum(jnp.where(ahead, 1.0, 0.0), axis=0, keepdims=True) + jnp.where(score_n > score, 1.0, 0.0)
        rank_n = jnp.sum(jnp.where(score >= score_n, 1.0, 0.0), axis=-1, keepdims=True)
        onehot = jnp.where(rank == sel_r, 1.0, 0.0)
        sel_col = lax.broadcasted_iota(jnp.int32, (n_sel, 1), 0).astype(_F32)
        idx = jnp.sum(onehot * blk_f, axis=-1, keepdims=True) + jnp.where(rank_n == sel_col, float(nbp), 0.0)
        idx_ref[kh] = jnp.broadcast_to(idx, (n_sel, LANE)).astype(jnp.int32)


def _cmp_select(q_s, pooled, new_rows, w0, past, n_sel):
    db, nbp, width = pooled.shape
    grid_spec = pltpu.PrefetchScalarGridSpec(
        num_scalar_prefetch=1,
        grid=(db,),
        in_specs=[
            pl.BlockSpec((None, N_HEADS, HEAD_DIM), lambda b, sl: (b, 0, 0)),
            pl.BlockSpec((None, nbp, width), lambda b, sl: (b, 0, 0)),
            pl.BlockSpec((None, 1, width), lambda b, sl: (b, 0, 0)),
            pl.BlockSpec((1, width), lambda b, sl: (0, 0)),
        ],
        out_specs=[
            pl.BlockSpec((None, N_HEADS, HEAD_DIM), lambda b, sl: (b, 0, 0)),
            pl.BlockSpec((None, N_KV_HEADS, n_sel, LANE), lambda b, sl: (b, 0, 0, 0)),
        ],
    )
    return pl.pallas_call(
        functools.partial(_cmp_select_kernel, past=past, n_sel=n_sel),
        grid_spec=grid_spec,
        out_shape=[jax.ShapeDtypeStruct((db, N_HEADS, HEAD_DIM), _F32),
                   jax.ShapeDtypeStruct((db, N_KV_HEADS, n_sel, LANE), jnp.int32)],
        compiler_params=_params(("arbitrary",)),
        name="cmp_select",
    )(_alibi_slopes(), q_s, pooled, new_rows, w0)


def _sel_sample_kernel(idx_ref, pt_ref, slope_ref, q_ref, *refs, past, n_sel):
    blk_refs, new_ref, o_ref, rows_ref = refs[:n_sel], refs[n_sel], refs[n_sel + 1], refs[n_sel + 2]
    b = pl.program_id(0)
    kh = pl.program_id(1)
    base = (b * N_KV_HEADS + kh) * n_sel
    rw = 2 * N_KV_HEADS
    row_i = lax.broadcasted_iota(jnp.int32, (BLOCK, 1, 1), 0)
    for jj in range(n_sel):
        kpos_c = idx_ref[base + jj] * BLOCK + row_i
        rows_ref[jj * BLOCK:(jj + 1) * BLOCK] = jnp.where(kpos_c < past, blk_refs[jj][...], new_ref[...])
    qpos = past
    n_rows = n_sel * BLOCK * rw
    rows = rows_ref[...].reshape(n_rows, HEAD_DIM).astype(_BF)
    col = lax.broadcasted_iota(jnp.int32, (1, n_rows), 1)
    pos = col // rw
    blk_of = jnp.zeros((1, n_rows), jnp.int32)
    for jj in range(n_sel):
        blk_of = jnp.where(pos // BLOCK == jj, idx_ref[base + jj], blk_of)
    dist = qpos - (blk_of * BLOCK + pos % BLOCK)
    mask = (dist >= 0) & (col % rw == 2 * kh)
    slope = jnp.concatenate([jnp.full((1, 1), slope_ref[kh * GROUP + g], _F32) for g in range(GROUP)], axis=0)
    s = _dot_t(q_ref[...].astype(_BF), rows) * SCALE - slope * dist.astype(_F32)
    p = _masked_softmax(s, mask)
    p_val = pltpu.roll(p, 1, 1)
    o_ref[...] = jnp.dot(p_val.astype(_BF), rows, preferred_element_type=_F32)


def _sel_sample(q_s, cache, idx, page_table, new_rows, layer, past, n_sel):
    db, n_pages = page_table.shape
    rw = 2 * N_KV_HEADS

    def blk_spec(j):
        def blk_map(b, k, idx_ref, pt_ref, sl):
            blk = idx_ref[(b * N_KV_HEADS + k) * n_sel + j]
            pos = jnp.minimum(blk * BLOCK, past - 1)
            return (layer, pt_ref[b * n_pages + pos // PAGE_SIZE], (pos % PAGE_SIZE) // BLOCK, 0, 0)
        return pl.BlockSpec((None, None, BLOCK, rw, HEAD_DIM), blk_map)

    grid_spec = pltpu.PrefetchScalarGridSpec(
        num_scalar_prefetch=3,
        grid=(db, N_KV_HEADS),
        in_specs=[pl.BlockSpec((None, GROUP, HEAD_DIM), lambda b, k, *_: (b * N_KV_HEADS + k, 0, 0))]
        + [blk_spec(j) for j in range(n_sel)]
        + [pl.BlockSpec((None, 1, rw, HEAD_DIM), lambda b, k, *_: (b, 0, 0, 0))],
        out_specs=pl.BlockSpec((None, GROUP, HEAD_DIM), lambda b, k, *_: (b * N_KV_HEADS + k, 0, 0)),
        scratch_shapes=[pltpu.VMEM((n_sel * BLOCK, rw, HEAD_DIM), _F32)],
    )
    return pl.pallas_call(
        functools.partial(_sel_sample_kernel, past=past, n_sel=n_sel),
        grid_spec=grid_spec,
        out_shape=jax.ShapeDtypeStruct((db * N_KV_HEADS, GROUP, HEAD_DIM), _F32),
        compiler_params=_params(("arbitrary", "arbitrary")),
        name="sel_sample",
    )(idx, page_table.reshape(-1), _alibi_slopes(), q_s, *([cache] * n_sel), new_rows)


def _win_combine_kernel(slope_ref, q_ref, win_ref, new_ref, oc_ref, os_ref, gate_ref, o_ref, *, past, w_buf):
    qpos = past
    q = q_ref[...]
    gate = _sigmoid(gate_ref[...])
    kpos = past - w_buf + lax.broadcasted_iota(jnp.int32, (1, w_buf), 1)
    dist = qpos - kpos
    mask = (dist >= 0) & (dist < WINDOW) & (kpos >= 0)
    dist_f = dist.astype(_F32)
    new = new_ref[...]
    mask_n = jnp.full((1, 1), 0 < WINDOW)
    for kh in range(N_KV_HEADS):
        kw = win_ref[:, (2 * kh) * HEAD_DIM:(2 * kh + 1) * HEAD_DIM].astype(_BF)
        vw = win_ref[:, (2 * kh + 1) * HEAD_DIM:(2 * kh + 2) * HEAD_DIM].astype(_BF)
        kn = new[:, (2 * kh) * HEAD_DIM:(2 * kh + 1) * HEAD_DIM]
        vn = new[:, (2 * kh + 1) * HEAD_DIM:(2 * kh + 2) * HEAD_DIM]
        hs = slice(kh * GROUP, (kh + 1) * GROUP)
        qk = q[hs]
        slope = jnp.concatenate([jnp.full((1, 1), slope_ref[kh * GROUP + g], _F32) for g in range(GROUP)], axis=0)
        s = _dot_t(qk.astype(_BF), kw) * SCALE - slope * dist_f
        qb = qk.astype(_BF).astype(_F32)
        s_n = jnp.sum(qb * kn.astype(_BF).astype(_F32), axis=-1, keepdims=True) * SCALE
        s = jnp.where(mask, s, -jnp.inf)
        s_n = jnp.where(mask_n, s_n, -jnp.inf)
        m = jnp.maximum(jnp.max(s, axis=-1, keepdims=True), s_n)
        m = jnp.where(jnp.isfinite(m), m, 0.0)
        p = jnp.where(mask, jnp.exp(s - m), 0.0)
        p_n = jnp.where(mask_n, jnp.exp(s_n - m), 0.0)
        inv = 1.0 / jnp.maximum(jnp.sum(p, axis=-1, keepdims=True) + p_n, 1e-30)
        o_w = (jnp.dot((p * inv).astype(_BF), vw, preferred_element_type=_F32)
               + (p_n * inv).astype(_BF).astype(_F32) * vn.astype(_BF).astype(_F32))
        g3 = gate[hs]
        o_ref[hs, :] = g3[:, 0:1] * oc_ref[hs, :] + g3[:, 1:2] * os_ref[hs, :] + g3[:, 2:3] * o_w


def _win_combine(q_s, win, new_rows, o_cmp, o_sel, gates, past):
    db, w_buf, width = win.shape
    head = pl.BlockSpec((None, N_HEADS, HEAD_DIM), lambda b, sl: (b, 0, 0))
    grid_spec = pltpu.PrefetchScalarGridSpec(
        num_scalar_prefetch=1,
        grid=(db,),
        in_specs=[
            head,
            pl.BlockSpec((None, w_buf, width), lambda b, sl: (b, 0, 0)),
            pl.BlockSpec((None, 1, width), lambda b, sl: (b, 0, 0)),
            head, head,
            pl.BlockSpec((None, N_HEADS, 3), lambda b, sl: (b, 0, 0)),
        ],
        out_specs=head,
    )
    return pl.pallas_call(
        functools.partial(_win_combine_kernel, past=past, w_buf=w_buf),
        grid_spec=grid_spec,
        out_shape=jax.ShapeDtypeStruct((db, N_HEADS, HEAD_DIM), _F32),
        compiler_params=_params(("arbitrary",)),
        name="win_combine",
    )(_alibi_slopes(), q_s, win, new_rows, o_cmp, o_sel, gates)


def _ssm_discretise(a_re, a_im, log_dt, b_re, b_im):
    dt = jnp.exp(log_dt)[:, None]
    mag = jnp.exp(dt * a_re)
    abar_re = mag * jnp.cos(dt * a_im)
    abar_im = mag * jnp.sin(dt * a_im)
    den = a_re * a_re + a_im * a_im
    cf_re = ((abar_re - 1.0) * a_re + abar_im * a_im) / den
    cf_im = (abar_im * a_re - (abar_re - 1.0) * a_im) / den
    bb_re = cf_re[..., None] * b_re - cf_im[..., None] * b_im
    bb_im = cf_re[..., None] * b_im + cf_im[..., None] * b_re
    return abar_re, abar_im, bb_re, bb_im


def _ssm_chunk_matrices(abar_re, abar_im, bb_re, bb_im, c_re, c_im):
    L = SSM_CHUNK
    g, p, ch = bb_re.shape
    nq = LANE // ch
    nblk = g // nq
    pw_re = [jnp.ones_like(abar_re)]
    pw_im = [jnp.zeros_like(abar_im)]
    for _ in range(L):
        r, i = pw_re[-1], pw_im[-1]
        pw_re.append(r * abar_re - i * abar_im)
        pw_im.append(r * abar_im + i * abar_re)
    pw_re = jnp.stack(pw_re, axis=1)
    pw_im = jnp.stack(pw_im, axis=1)
    bt_re = bb_re.transpose(0, 2, 1)[:, None]
    bt_im = bb_im.transpose(0, 2, 1)[:, None]
    e_re = pw_re[:, :L, None, :] * bt_re - pw_im[:, :L, None, :] * bt_im
    e_im = pw_re[:, :L, None, :] * bt_im + pw_im[:, :L, None, :] * bt_re
    k_lag = (jnp.einsum("gcp,gldp->gldc", c_re, e_re, precision=_HI)
             - jnp.einsum("gcp,gldp->gldc", c_im, e_im, precision=_HI))

    def lane_diag(x):
        w = x.shape[-1]
        flat = x.transpose(0, 2, 1, 3, 4).reshape(nblk, L * LANE, w)
        mask = np.broadcast_to(np.eye(nq, dtype=np.float32)[None, :, None, :, None], (L, nq, ch, nq, w))
        return jnp.tile(flat, (1, 1, nq)) * jnp.asarray(mask.reshape(L * LANE, nq * w))

    bd = lane_diag(k_lag.reshape(nblk, nq, L, ch, ch)).reshape(nblk, L, LANE, LANE)

    def state_in(e):
        return lane_diag(e[:, ::-1].reshape(nblk, nq, L, ch, p))

    pr = pw_re[:, 1:, None, :]
    pi = pw_im[:, 1:, None, :]

    def state_out_t(v):
        return lane_diag(v.reshape(nblk, nq, L, ch, p))

    vt_re = state_out_t(c_re[:, None] * pr - c_im[:, None] * pi)
    vt_im = state_out_t(-(c_re[:, None] * pi + c_im[:, None] * pr))
    al_re = pw_re[:, L].reshape(nblk, 1, nq * p)
    al_im = pw_im[:, L].reshape(nblk, 1, nq * p)
    return (bd.astype(_BF), state_in(e_re).astype(_BF), state_in(e_im).astype(_BF),
            vt_re.astype(_BF), vt_im.astype(_BF), al_re, al_im)


def _ssm_prompt_kernel(u_ref, bd_ref, wre_ref, wim_ref, vtre_ref, vtim_ref, alre_ref, alim_ref, d_ref,
                       y_ref, hre_ref, him_ref, msc, xs, ys, loc_re, loc_im, hp_re, hp_im, *, n_chunk):
    L = SSM_CHUNK

    @pl.when(pl.program_id(1) == 0)
    def _():
        for j in range(L):
            for t in range(L):
                blk = bd_ref[t - j] if t >= j else jnp.zeros((LANE, LANE), _BF)
                msc[j * LANE:(j + 1) * LANE, t * LANE:(t + 1) * LANE] = blk

    for j in range(L):
        xs[:, j * LANE:(j + 1) * LANE] = u_ref[pl.ds(j, n_chunk, stride=L), :].astype(_BF)
    xb = xs[...]
    y = jnp.dot(xb, msc[...], preferred_element_type=_F32)
    loc_re[...] = jnp.dot(xb, wre_ref[...], preferred_element_type=_F32)
    loc_im[...] = jnp.dot(xb, wim_ref[...], preferred_element_type=_F32)
    a_re = alre_ref[...]
    a_im = alim_ref[...]

    def step(c, carry):
        h_re, h_im = carry
        hp_re[pl.ds(c, 1), :] = h_re
        hp_im[pl.ds(c, 1), :] = h_im
        l_re = loc_re[pl.ds(c, 1), :]
        l_im = loc_im[pl.ds(c, 1), :]
        return (a_re * h_re - a_im * h_im + l_re, a_re * h_im + a_im * h_re + l_im)

    zero = jnp.zeros(a_re.shape, _F32)
    h_re, h_im = lax.fori_loop(0, n_chunk, step, (zero, zero))
    y = y + _dot_t(hp_re[...].astype(_BF), vtre_ref[...])
    y = y + _dot_t(hp_im[...].astype(_BF), vtim_ref[...])
    d = d_ref[...]
    for t in range(L):
        x_t = u_ref[pl.ds(t, n_chunk, stride=L), :]
        ys[pl.ds(t, n_chunk, stride=L), :] = y[:, t * LANE:(t + 1) * LANE] + d * x_t
    y_ref[...] = ys[...].astype(y_ref.dtype)
    hre_ref[...] = h_re
    him_ref[...] = h_im


def _ssm_prompt(zug, mats, d_skip, batch, seq, d_ssm):
    bd, wre, wim, vtre, vtim, al_re, al_im = mats
    L = SSM_CHUNK
    nblk = bd.shape[0]
    nq = LANE // GROUP_CH
    n_chunk = seq // L
    sw = nq * P_STATE
    per_blk = lambda shape: pl.BlockSpec((None,) + shape, lambda g, b: (g, 0, 0))
    fin_spec = pl.BlockSpec((None, None, 1, sw), lambda g, b: (g, b, 0, 0))
    y, h_re, h_im = pl.pallas_call(
        functools.partial(_ssm_prompt_kernel, n_chunk=n_chunk),
        grid=(nblk, batch),
        in_specs=[pl.BlockSpec((seq, LANE), lambda g, b: (b, g)),
                  pl.BlockSpec((None, L, LANE, LANE), lambda g, b: (g, 0, 0, 0)),
                  per_blk((L * LANE, sw)), per_blk((L * LANE, sw)), per_blk((L * LANE, sw)), per_blk((L * LANE, sw)),
                  per_blk((1, sw)), per_blk((1, sw)), per_blk((1, LANE))],
        out_specs=[pl.BlockSpec((seq, LANE), lambda g, b: (b, g)), fin_spec, fin_spec],
        out_shape=[jax.ShapeDtypeStruct((batch * seq, d_ssm), _BF),
                   jax.ShapeDtypeStruct((nblk, batch, 1, sw), _F32),
                   jax.ShapeDtypeStruct((nblk, batch, 1, sw), _F32)],
        scratch_shapes=[pltpu.VMEM((L * LANE, L * LANE), _BF), pltpu.VMEM((n_chunk, L * LANE), _BF),
                        pltpu.VMEM((seq, LANE), _F32)]
        + [pltpu.VMEM((n_chunk, sw), _F32) for _ in range(4)],
        compiler_params=_params(("arbitrary", "arbitrary")),
        name="ssm_prompt",
    )(zug, bd, wre, wim, vtre, vtim, al_re, al_im, d_skip.reshape(nblk, 1, LANE))
    fin = lambda h: h.reshape(nblk, batch, nq, P_STATE).transpose(1, 0, 2, 3).reshape(batch, nblk * nq, P_STATE)
    return y, jnp.stack([fin(h_re), fin(h_im)], axis=-1)


def _ssm_sample_kernel(xe_ref, bre_ref, bim_ref, h0r_ref, h0i_ref, ar_ref, ai_ref, cre_ref, cim_ref, seg_ref,
                       xt_ref, d_ref, y_ref, sre_ref, sim_ref):
    ch = xe_ref.shape[0]
    h0r, h0i = h0r_ref[...], h0i_ref[...]
    a_re, a_im = ar_ref[...], ai_ref[...]
    rnd = lambda v: v.astype(_BF).astype(_F32)
    bu_re = rnd(xe_ref[0]) * rnd(bre_ref[0])
    bu_im = rnd(xe_ref[0]) * rnd(bim_ref[0])
    for c in range(1, ch):
        bu_re = bu_re + rnd(xe_ref[c]) * rnd(bre_ref[c])
        bu_im = bu_im + rnd(xe_ref[c]) * rnd(bim_ref[c])
    s_re = bu_re + (a_re * h0r - a_im * h0i)
    s_im = bu_im + (a_re * h0i + a_im * h0r)
    sre_ref[...] = s_re
    sim_ref[...] = s_im
    seg = seg_ref[...]
    sb_re, sb_im = rnd(s_re), rnd(s_im)
    for c in range(ch):
        y_re = jnp.dot(sb_re * rnd(cre_ref[c]), seg, precision=_HI, preferred_element_type=_F32)
        y_im = jnp.dot(sb_im * rnd(cim_ref[c]), seg, precision=_HI, preferred_element_type=_F32)
        y_ref[c] = y_re - y_im + d_ref[c] * xt_ref[c]


def _ssm_sample(u_s, h0, abar_re, abar_im, bb_re, bb_im, c_re, c_im, d_skip):
    db = u_s.shape[0]
    g, p, ch = bb_re.shape
    gp = g * p
    x3 = u_s.reshape(db, g, ch)
    xe = jnp.broadcast_to(x3.transpose(2, 0, 1)[..., None], (ch, db, g, p)).reshape(ch, db, gp)
    flat_b = lambda b: b.transpose(2, 0, 1).reshape(ch, 1, gp)
    flat_c = lambda c: c.transpose(1, 0, 2).reshape(ch, 1, gp)
    seg = jnp.asarray(np.repeat(np.eye(g, dtype=np.float32), p, axis=0))
    xt = x3.transpose(2, 0, 1)
    dt = d_skip.transpose(1, 0).reshape(ch, 1, g)
    y, s_re, s_im = pl.pallas_call(
        _ssm_sample_kernel,
        out_shape=[jax.ShapeDtypeStruct((ch, db, g), _F32), jax.ShapeDtypeStruct((db, gp), _F32),
                   jax.ShapeDtypeStruct((db, gp), _F32)],
        compiler_params=pltpu.CompilerParams(vmem_limit_bytes=VMEM_LIMIT),
        name="ssm_sample",
    )(xe, flat_b(bb_re), flat_b(bb_im), h0[..., 0].reshape(db, gp), h0[..., 1].reshape(db, gp),
      abar_re.reshape(1, gp), abar_im.reshape(1, gp), flat_c(c_re), flat_c(c_im), seg, xt, dt)
    y = y.transpose(1, 2, 0).reshape(db, g * ch)
    h_fin = jnp.stack([s_re.reshape(db, g, p), s_im.reshape(db, g, p)], axis=-1)
    return y, h_fin


def _router_kernel(x_ref, g_ref, wr_ref, idx_ref, gate_ref):
    x = x_ref[...]
    h = x * lax.rsqrt(jnp.mean(x * x, axis=-1, keepdims=True) + EPS) * g_ref[...]
    logits = jnp.dot(h.astype(_BF), wr_ref[...].astype(_BF), preferred_element_type=_F32)
    lane = lax.broadcasted_iota(jnp.int32, logits.shape, 1)
    lane_f = lane.astype(_F32)
    lg = jnp.where(lane < N_EXPERTS, logits, -jnp.inf)
    v1 = jnp.max(lg, axis=-1, keepdims=True)
    i1 = jnp.min(jnp.where(lg == v1, lane_f, float(LANE)), axis=-1, keepdims=True)
    lg2 = jnp.where(lane_f == i1, -jnp.inf, lg)
    v2 = jnp.max(lg2, axis=-1, keepdims=True)
    i2 = jnp.min(jnp.where(lg2 == v2, lane_f, float(LANE)), axis=-1, keepdims=True)
    e = jnp.exp(v2 - v1)
    den = 1.0 + e
    idx_ref[...] = jnp.where(lane == 0, i1, jnp.where(lane == 1, i2, 0.0)).astype(jnp.int32)
    gate_ref[...] = jnp.where(lane == 0, 1.0 / den, jnp.where(lane == 1, e / den, 0.0))


def _router(x, g, wr_pad):
    n, d = x.shape
    tm = _tile(n, 256)
    return pl.pallas_call(
        _router_kernel,
        grid=(n // tm,),
        in_specs=[pl.BlockSpec((tm, d), lambda i: (i, 0)), pl.BlockSpec((1, d), lambda i: (0, 0)),
                  pl.BlockSpec((d, LANE), lambda i: (0, 0))],
        out_specs=[pl.BlockSpec((tm, LANE), lambda i: (i, 0)), pl.BlockSpec((tm, LANE), lambda i: (i, 0))],
        out_shape=[jax.ShapeDtypeStruct((n, LANE), jnp.int32), jax.ShapeDtypeStruct((n, LANE), _F32)],
        compiler_params=_params(("arbitrary",)),
        name="router",
    )(x, g.reshape(1, d), wr_pad)


def _row_copy(src_hbm, row, dst, slot, sem):
    return pltpu.make_async_copy(src_hbm.at[pl.ds(row, 1)], dst.at[pl.ds(slot, 1)], sem)


def _moe_gather_kernel(tok_ref, x_hbm, g_ref, o_ref, buf, sems, *, tm, n_tiles):
    i = pl.program_id(0)
    slot = lax.rem(i, 2)

    def fetch(tile, s):
        def start(r, c):
            _row_copy(x_hbm, tok_ref[tile * tm + r], buf.at[s], r, sems.at[s]).start()
            return c

        lax.fori_loop(0, tm, start, 0)

    @pl.when(i == 0)
    def _():
        fetch(0, 0)

    @pl.when(i + 1 < n_tiles)
    def _():
        fetch(i + 1, 1 - slot)

    def wait(r, c):
        _row_copy(x_hbm, 0, buf.at[slot], r, sems.at[slot]).wait()
        return c

    lax.fori_loop(0, tm, wait, 0)
    x = buf[slot]
    inv = lax.rsqrt(jnp.mean(x * x, axis=-1, keepdims=True) + EPS)
    o_ref[...] = (x * inv * g_ref[...]).astype(o_ref.dtype)


def _moe_gather(x_all, g, row_tok, tm):
    n_rows = row_tok.shape[0]
    d = x_all.shape[1]
    n_tiles = n_rows // tm
    grid_spec = pltpu.PrefetchScalarGridSpec(
        num_scalar_prefetch=1,
        grid=(n_tiles,),
        in_specs=[pl.BlockSpec(memory_space=pl.ANY), pl.BlockSpec((1, d), lambda i, tok: (0, 0))],
        out_specs=pl.BlockSpec((tm, d), lambda i, tok: (i, 0)),
        scratch_shapes=[pltpu.VMEM((2, tm, d), _F32), pltpu.SemaphoreType.DMA((2,))],
    )
    return pl.pallas_call(
        functools.partial(_moe_gather_kernel, tm=tm, n_tiles=n_tiles),
        grid_spec=grid_spec,
        out_shape=jax.ShapeDtypeStruct((n_rows, d), _BF),
        compiler_params=_params(("arbitrary",)),
        name="moe_gather",
    )(row_tok, x_all, g.reshape(1, d))


def _moe_up_kernel(texp_ref, nused_ref, x_ref, w1_ref, w3_ref, o_ref, w1b, w3b):
    i = pl.program_id(1)
    n_used = nused_ref[0]
    prev = texp_ref[jnp.maximum(i - 1, 0)]
    fresh = (i == 0) | (texp_ref[i] != prev)

    @pl.when((i < n_used) & fresh)
    def _():
        w1b[...] = w1_ref[...].astype(_BF)
        w3b[...] = w3_ref[...].astype(_BF)

    @pl.when(i < n_used)
    def _():
        x = x_ref[...]
        a = jnp.dot(x, w1b[...], preferred_element_type=_F32)
        b = jnp.dot(x, w3b[...], preferred_element_type=_F32)
        o_ref[...] = ((a * _sigmoid(a)) * b).astype(o_ref.dtype)

    @pl.when(i >= n_used)
    def _():
        o_ref[...] = jnp.zeros(o_ref.shape, o_ref.dtype)


def _moe_up(xg, w1, w3, layer, tile_exp, n_used, *, tm, tf):
    n_rows, d = xg.shape
    f = w1.shape[3]
    n_tiles = n_rows // tm

    def live(i, nu):
        return jnp.minimum(i, nu[0] - 1)

    grid_spec = pltpu.PrefetchScalarGridSpec(
        num_scalar_prefetch=2,
        grid=(f // tf, n_tiles),
        in_specs=[
            pl.BlockSpec((tm, d), lambda j, i, te, nu: (live(i, nu), 0)),
            pl.BlockSpec((None, None, d, tf), lambda j, i, te, nu: (layer, te[live(i, nu)], 0, j)),
            pl.BlockSpec((None, None, d, tf), lambda j, i, te, nu: (layer, te[live(i, nu)], 0, j)),
        ],
        out_specs=pl.BlockSpec((tm, tf), lambda j, i, te, nu: (i, j)),
        scratch_shapes=[pltpu.VMEM((d, tf), _BF), pltpu.VMEM((d, tf), _BF)],
    )
    return pl.pallas_call(
        _moe_up_kernel,
        grid_spec=grid_spec,
        out_shape=jax.ShapeDtypeStruct((n_rows, f), _BF),
        compiler_params=_params(("arbitrary", "arbitrary")),
        name="moe_up",
    )(tile_exp, n_used, xg, w1, w3)


def _moe_down_kernel(texp_ref, nused_ref, a_ref, w_ref, o_ref):
    i = pl.program_id(0)
    k = pl.program_id(2)

    @pl.when(i < nused_ref[0])
    def _():
        part = jnp.dot(a_ref[...], w_ref[...].astype(_BF), preferred_element_type=_F32)

        @pl.when(k == 0)
        def _():
            o_ref[...] = part

        @pl.when(k > 0)
        def _():
            o_ref[...] += part

    @pl.when((i >= nused_ref[0]) & (k == 0))
    def _():
        o_ref[...] = jnp.zeros(o_ref.shape, o_ref.dtype)


def _moe_down(a, w2, layer, tile_exp, n_used, *, tm, tn, tk):
    n_rows, f = a.shape
    d = w2.shape[3]
    n_tiles = n_rows // tm
    nn, nk = d // tn, f // tk

    def live(i, nu):
        return jnp.minimum(i, nu[0] - 1)

    def last(i, x, n_x, nu):
        return jnp.where(i < nu[0], x, n_x - 1)

    grid_spec = pltpu.PrefetchScalarGridSpec(
        num_scalar_prefetch=2,
        grid=(n_tiles, nn, nk),
        in_specs=[
            pl.BlockSpec((tm, tk), lambda i, j, k, te, nu: (live(i, nu), last(i, k, nk, nu))),
            pl.BlockSpec((None, None, tk, tn),
                         lambda i, j, k, te, nu: (layer, te[live(i, nu)], last(i, k, nk, nu), last(i, j, nn, nu))),
        ],
        out_specs=pl.BlockSpec((tm, tn), lambda i, j, k, te, nu: (i, j)),
    )
    return pl.pallas_call(
        _moe_down_kernel,
        grid_spec=grid_spec,
        out_shape=jax.ShapeDtypeStruct((n_rows, d), _F32),
        compiler_params=_params(("arbitrary", "arbitrary", "arbitrary")),
        name="moe_down",
    )(tile_exp, n_used, a, w2)


def _moe_combine_kernel(pos_ref, x_ref, gate_ref, y_hbm, o_ref, b0, b1, sems, *, tm):
    def start(r, c):
        _row_copy(y_hbm, pos_ref[0, 0, 2 * r], b0, r, sems.at[0]).start()
        _row_copy(y_hbm, pos_ref[0, 0, 2 * r + 1], b1, r, sems.at[1]).start()
        return c

    lax.fori_loop(0, tm, start, 0)

    def wait(r, c):
        _row_copy(y_hbm, 0, b0, r, sems.at[0]).wait()
        _row_copy(y_hbm, 0, b1, r, sems.at[1]).wait()
        return c

    lax.fori_loop(0, tm, wait, 0)
    gate = gate_ref[...]
    o_ref[...] = x_ref[...] + (gate[:, 0:1] * b0[...] + gate[:, 1:2] * b1[...])


def _moe_combine(x, gates, pos, y_rows):
    n, d = x.shape
    tm = _tile(n, 256)
    n_tiles = n // tm
    return pl.pallas_call(
        functools.partial(_moe_combine_kernel, tm=tm),
        grid=(n_tiles,),
        in_specs=[pl.BlockSpec((1, 1, 2 * tm), lambda i: (i, 0, 0), memory_space=pltpu.SMEM),
                  pl.BlockSpec((tm, d), lambda i: (i, 0)),
                  pl.BlockSpec((tm, LANE), lambda i: (i, 0)),
                  pl.BlockSpec(memory_space=pl.ANY)],
        out_specs=pl.BlockSpec((tm, d), lambda i: (i, 0)),
        out_shape=jax.ShapeDtypeStruct((n, d), _F32),
        scratch_shapes=[pltpu.VMEM((tm, d), _F32), pltpu.VMEM((tm, d), _F32), pltpu.SemaphoreType.DMA((2,))],
        compiler_params=_params(("arbitrary",)),
        name="moe_combine",
    )(pos.reshape(n_tiles, 1, 2 * tm), x, gates, y_rows)


def _moe_layer(xp, xs, g, w_router, w1, w3, w2, layer):
    n_p, d = xp.shape
    n_s = xs.shape[0]
    n = n_p + n_s
    wr_pad = jnp.pad(w_router, ((0, 0), (0, LANE - N_EXPERTS)))
    idx_p, gate_p = _router(xp, g, wr_pad)
    idx_s, gate_s = _router(xs, g, wr_pad)
    top_i = jnp.concatenate([idx_p[:, :TOP_K], idx_s[:, :TOP_K]], axis=0)
    n_asg = n * TOP_K
    tm = 512 if n_asg >= 4096 else 64
    n_tiles = -(-n_asg // tm) + N_EXPERTS
    e_flat = top_i.reshape(n_asg)
    onehot = (e_flat[:, None] == jnp.arange(N_EXPERTS)[None, :]).astype(jnp.int32)
    within = jnp.sum((jnp.cumsum(onehot, axis=0) - onehot) * onehot, axis=1)
    counts = jnp.sum(onehot, axis=0)
    padded = (counts + tm - 1) // tm * tm
    pad_end = jnp.cumsum(padded)
    dest = (pad_end - padded)[e_flat] + within
    row_tok = jnp.zeros((n_tiles * tm,), jnp.int32).at[dest].set(jnp.arange(n_asg, dtype=jnp.int32) // TOP_K)
    tile_exp = jnp.minimum(jnp.searchsorted(pad_end, jnp.arange(n_tiles) * tm, side="right"),
                           N_EXPERTS - 1).astype(jnp.int32)
    n_used = (pad_end[-1:] // tm).astype(jnp.int32)
    pos = dest.reshape(n, TOP_K).astype(jnp.int32)

    x_all = jnp.concatenate([xp, xs], axis=0)
    xg = _moe_gather(x_all, g, row_tok, tm)
    f = w1.shape[3]
    a = _moe_up(xg, w1, w3, layer, tile_exp, n_used, tm=tm, tf=_tile(f, 512, LANE))
    y_rows = _moe_down(a, w2, layer, tile_exp, n_used, tm=tm, tn=d, tk=_tile(f, 512, LANE))
    out_p = _moe_combine(xp, gate_p, pos[:n_p], y_rows)
    out_s = _moe_combine(xs, gate_s, pos[n_p:], y_rows)
    return out_p, out_s


def kernel(x_prompt, x_sample, cache_kv_cmp, cache_kv_sel, cache_kv_win, state_ssm, page_table, norm_mix, norm_ffn, w_in, g_q, g_k, w_cmp, w_attn_out, ssm_a_re, ssm_a_im, ssm_log_dt, ssm_b_re, ssm_b_im, ssm_c_re, ssm_c_im, ssm_d, w_glu, w_out, ffn_w1, ffn_w3, ffn_w2, moe_router, moe_w1, moe_w3, moe_w2):
    batch, seq, d_model = x_prompt.shape
    db, dec_seq, _ = x_sample.shape
    assert dec_seq == 1, "the sample group advances one token per step"
    depth = w_in.shape[0]
    n_pool = cache_kv_cmp.shape[1]
    n_pages = page_table.shape[1]
    past = n_pages * PAGE_SIZE
    d_attn = N_HEADS * HEAD_DIM
    d_kv = N_KV_HEADS * HEAD_DIM
    d_ssm = ssm_d.shape[1] * GROUP_CH
    n_gate = 3 * N_HEADS
    off_gn = d_attn + 6 * d_kv
    off_u = off_gn + n_gate
    n_ug = d_ssm + 2 * d_model
    row_w = N_KV_HEADS * 2 * HEAD_DIM
    nb = seq // BLOCK
    n_p = batch * seq

    tm = _tile(n_p, 1024)
    tn = 512
    assert off_gn % tn == 0 and n_ug % tn == 0 and d_ssm % tn == 0 and d_model % tn == 0

    xp = x_prompt.reshape(n_p, d_model)
    xs = x_sample.reshape(db, d_model)
    w_ug = w_in[:, :, off_u:]
    flag_cols = np.zeros((1, off_gn), np.float32)
    flag_cols[:, :d_attn] = 1.0
    for br in range(3):
        lo = d_attn + br * 2 * d_kv
        flag_cols[:, lo:lo + d_kv] = 1.0
    flag_cols = jnp.asarray(flag_cols)
    expand = jnp.asarray(np.repeat(np.eye(nb, dtype=np.float32), BLOCK, axis=1)).astype(_BF)
    eye_nb = jnp.asarray(np.eye(nb, dtype=np.float32))
    n_sel = min(N_SELECT, past // BLOCK + 1)

    outs = {k: [] for k in ("cmp_p", "cmp_s", "sel_p", "sel_s", "win_p", "win_s", "ssm_p", "ssm_s")}
    for l in range(depth):
        hp = _rmsnorm(xp, norm_mix[l])
        hs = _rmsnorm(xs, norm_mix[l])
        gain = jnp.concatenate([jnp.tile(g_q[l], N_HEADS)]
                               + [jnp.concatenate([jnp.tile(g_k[l, br], N_KV_HEADS), jnp.ones((d_kv,), _F32)])
                                  for br in range(3)]).reshape(1, off_gn)
        zqkv, zqkv_s = _mm(hp, hs, [(w_in, l, 0)], off_gn, mode="headnorm", out_dtype=_F32, tm=tm, tn=tn,
                           gain=gain, flag=flag_cols, name="in_proj_qkv")
        zg, zg_s = _mm(hp, hs, [(w_in, l, off_gn // LANE)], LANE, mode="plain", out_dtype=_F32, tm=tm, tn=LANE,
                       name="in_proj_gate")
        zug, zug_s = _mm(hp, hs, [(w_ug, l, 0)], n_ug, mode="plain", out_dtype=_F32, tm=tm, tn=tn,
                         name="in_proj_ug")

        def kv_rows(z, lead):
            r = z[:, d_attn:off_gn].reshape(lead + (3, 2, N_KV_HEADS, HEAD_DIM))
            return jnp.swapaxes(r, -3, -2)

        rows_p = kv_rows(zqkv, (batch, seq))
        rows_s = kv_rows(zqkv_s, (db, 1))
        outs["cmp_p"].append(rows_p[:, :, 0])
        outs["sel_p"].append(rows_p[:, :, 1])
        outs["win_p"].append(rows_p[:, seq - min(WINDOW, seq):, 2])
        outs["cmp_s"].append(rows_s[:, :, 0])
        outs["sel_s"].append(rows_s[:, :, 1])

        gates_r = zg[:, :n_gate].reshape(n_p, 3, N_KV_HEADS, GROUP).transpose(2, 0, 1, 3).reshape(
            N_KV_HEADS, n_p, 3 * GROUP)
        wk = w_cmp[l].transpose(0, 2, 1)
        pool = (eye_nb[None, None, :, :, None] * wk[:, :, None, None, :]).reshape(2, N_KV_HEADS, nb, seq)
        o_attn = _nsa_prompt(zqkv, gates_r, pool[0], pool[1], expand, batch, seq)

        q_s = zqkv_s[:, :d_attn].reshape(db, N_HEADS, HEAD_DIM)
        tile_view = lambda c: c.reshape(depth, n_pool, PAGE_SIZE, 2 * N_KV_HEADS, HEAD_DIM)
        w_exp = jnp.broadcast_to(w_cmp[l].transpose(1, 2, 0)[..., None],
                                 (BLOCK, N_KV_HEADS, 2, HEAD_DIM)).reshape(BLOCK, 2 * N_KV_HEADS, HEAD_DIM)
        pooled = _pool_pages(tile_view(cache_kv_cmp), page_table, w_exp, l).reshape(db, past // BLOCK, row_w)
        new_c = rows_s[:, :, 0].reshape(db, 1, row_w)
        o_cmp_s, idx = _cmp_select(q_s, pooled, new_c, w_exp[:1].reshape(1, row_w), past, n_sel)
        idx = idx[..., 0].reshape(-1)
        o_sel_s = _sel_sample(q_s.reshape(db * N_KV_HEADS, GROUP, HEAD_DIM), tile_view(cache_kv_sel),
                              idx, page_table, rows_s[:, :, 1].reshape(db, 1, 2 * N_KV_HEADS, HEAD_DIM),
                              l, past, n_sel)
        win = cache_kv_win[l]
        w_buf = win.shape[1]
        new_w = rows_s[:, :, 2]
        gates_s = zg_s[:, :n_gate].reshape(db, 3, N_HEADS).transpose(0, 2, 1)
        o_attn_s = _win_combine(q_s, win.reshape(db, w_buf, row_w), new_w.reshape(db, 1, row_w),
                                o_cmp_s, o_sel_s.reshape(db, N_HEADS, HEAD_DIM), gates_s, past)
        outs["win_s"].append(jnp.concatenate([win, new_w], axis=1)[:, 1:])

        abar_re, abar_im, bb_re, bb_im = _ssm_discretise(ssm_a_re[l], ssm_a_im[l], ssm_log_dt[l],
                                                         ssm_b_re[l], ssm_b_im[l])
        mats = _ssm_chunk_matrices(abar_re, abar_im, bb_re, bb_im, ssm_c_re[l], ssm_c_im[l])
        y_p, h_p = _ssm_prompt(zug, mats, ssm_d[l], batch, seq, d_ssm)
        y_s, h_s = _ssm_sample(zug_s[:, :d_ssm], state_ssm[l], abar_re, abar_im, bb_re, bb_im,
                               ssm_c_re[l], ssm_c_im[l], ssm_d[l])
        outs["ssm_p"].append(h_p)
        outs["ssm_s"].append(h_s)

        mg, mg_s = _merge(o_attn, o_attn_s.reshape(db, d_attn).astype(_BF), y_p, y_s.astype(_BF),
                          w_attn_out, w_glu, l, zug, zug_s, d_model, d_ssm, tm=_tile(n_p, 512), tn=tn)
        xp, xs = _mm(mg, mg_s, [(w_out, l, 0)], d_model, mode="resid", out_dtype=_F32, tm=tm, tn=tn,
                     res=xp, res_s=xs, name="out_proj")

        if l % 2 == 0:
            j = l // 2
            hp = _rmsnorm(xp, norm_ffn[l])
            hs = _rmsnorm(xs, norm_ffn[l])
            f = ffn_w1.shape[2]
            a, a_s = _mm(hp, hs, [(ffn_w1, j, 0), (ffn_w3, j, 0)], f, mode="swiglu", out_dtype=_BF,
                         tm=_tile(n_p, 512), tn=_tile(f, 512, LANE), name="ffn_up")
            xp, xs = _ffn_down(a, a_s, ffn_w2, j, xp, xs, tm=_tile(n_p, 1024), tn=_tile(d_model, 1024, LANE),
                               tk=_tile(f, 1024, LANE))
        else:
            j = l // 2
            xp, xs = _moe_layer(xp, xs, norm_ffn[l], moe_router[j], moe_w1, moe_w3, moe_w2, j)

    st = lambda k: jnp.stack(outs[k])
    return (xp.reshape(batch, seq, d_model), xs.reshape(db, 1, d_model), st("cmp_p"), st("cmp_s"), st("sel_p"),
            st("sel_s"), st("win_p"), st("win_s"), st("ssm_p"), st("ssm_s"))
```

```python
import functools
import math

import numpy as np
import jax
import jax.numpy as jnp
from jax import lax
from jax.experimental import pallas as pl
from jax.experimental.pallas import tpu as pltpu

HEAD_DIM = 128
N_HEADS = 16
N_KV_HEADS = 4
GROUP = N_HEADS // N_KV_HEADS
BLOCK = 64
N_SELECT = 16
WINDOW = 512
PAGE_SIZE = 128
GROUP_CH = 16
P_STATE = 64
N_EXPERTS = 8
TOP_K = 2
EPS = 1.0e-6
FORCED_SCORE = 1.0e6
ALIBI_MAX_EXP = 8.0
SCALE = HEAD_DIM ** -0.5
LOG2E = 1.4426950408889634
SSM_CHUNK = 16
LANE = 128
SUBLANE = 8
VMEM_LIMIT = 56 << 20

_BF = jnp.bfloat16
_F32 = jnp.float32
_HI = lax.Precision.HIGHEST


def _tile(n, pref, mult=SUBLANE):
    if n <= pref:
        return n
    for t in range(pref - pref % mult, 0, -mult):
        if n % t == 0:
            return t
    return n


def _params(sem):
    return pltpu.CompilerParams(dimension_semantics=sem, vmem_limit_bytes=VMEM_LIMIT)


def _rms_kernel(x_ref, g_ref, o_ref):
    x = x_ref[...]
    inv = lax.rsqrt(jnp.mean(x * x, axis=-1, keepdims=True) + EPS)
    o_ref[...] = (x * inv * g_ref[...]).astype(o_ref.dtype)


def _rmsnorm(x, g, out_dtype=_BF):
    m, d = x.shape
    tm = _tile(m, 256)
    return pl.pallas_call(
        _rms_kernel,
        grid=(m // tm,),
        in_specs=[pl.BlockSpec((tm, d), lambda i: (i, 0)), pl.BlockSpec((1, d), lambda i: (0, 0))],
        out_specs=pl.BlockSpec((tm, d), lambda i: (i, 0)),
        out_shape=jax.ShapeDtypeStruct((m, d), out_dtype),
        compiler_params=_params(("arbitrary",)),
        name="rmsnorm",
    )(x, g.reshape(1, d))


def _sigmoid(x):
    return 1.0 / (1.0 + jnp.exp(-x))


def _mm_kernel(*refs, n_w, mode):
    x_ref, xs_ref = refs[0], refs[1]
    w_refs = refs[2:2 + n_w]
    p = 2 + n_w
    extra = ()
    if mode in ("headnorm", "resid"):
        extra = refs[p:p + 2]
        p += 2
    o_ref, os_ref = refs[p], refs[p + 1]
    wb_refs = refs[p + 2:p + 2 + n_w]
    i = pl.program_id(1)

    @pl.when(i == 0)
    def _():
        for w_ref, wb_ref in zip(w_refs, wb_refs):
            wb_ref[...] = w_ref[...].astype(_BF)

    def compute(x, out_ref, res):
        accs = [jnp.dot(x, wb[...], preferred_element_type=_F32) for wb in wb_refs]
        if mode == "plain":
            out_ref[...] = accs[0].astype(out_ref.dtype)
        elif mode == "swiglu":
            a = accs[0]
            out_ref[...] = ((a * _sigmoid(a)) * accs[1]).astype(out_ref.dtype)
        elif mode == "resid":
            out_ref[...] = (res + accs[0]).astype(out_ref.dtype)
        else:
            acc = accs[0]
            gain_ref, flag_ref = extra
            for c in range(acc.shape[1] // HEAD_DIM):
                sl = slice(c * HEAD_DIM, (c + 1) * HEAD_DIM)
                a = acc[:, sl]
                nrm = a * lax.rsqrt(jnp.mean(a * a, axis=-1, keepdims=True) + EPS) * gain_ref[:, sl]
                out_ref[:, sl] = jnp.where(flag_ref[:, sl] > 0.0, nrm, a).astype(out_ref.dtype)

    compute(x_ref[...], o_ref, extra[0][...] if mode == "resid" else None)

    @pl.when(i == 0)
    def _():
        compute(xs_ref[...], os_ref, extra[1][...] if mode == "resid" else None)


def _mm(x, xs, ws, n_cols, *, mode, out_dtype, tm, tn, gain=None, flag=None, res=None, res_s=None,
        name="mm"):
    m, k = x.shape
    ms = xs.shape[0]
    nj, ni = n_cols // tn, m // tm
    in_specs = [pl.BlockSpec((tm, k), lambda j, i: (i, 0)), pl.BlockSpec((ms, k), lambda j, i: (0, 0))]
    args = [x, xs]
    for w, lid, off in ws:
        in_specs.append(pl.BlockSpec((None, k, tn), functools.partial(lambda j, i, lid, off: (lid, 0, j + off),
                                                                       lid=lid, off=off)))
        args.append(w)
    if mode == "headnorm":
        in_specs += [pl.BlockSpec((1, tn), lambda j, i: (0, j)), pl.BlockSpec((1, tn), lambda j, i: (0, j))]
        args += [gain, flag]
    elif mode == "resid":
        in_specs += [pl.BlockSpec((tm, tn), lambda j, i: (i, j)), pl.BlockSpec((ms, tn), lambda j, i: (0, j))]
        args += [res, res_s]
    return pl.pallas_call(
        functools.partial(_mm_kernel, n_w=len(ws), mode=mode),
        grid=(nj, ni),
        in_specs=in_specs,
        out_specs=[pl.BlockSpec((tm, tn), lambda j, i: (i, j)), pl.BlockSpec((ms, tn), lambda j, i: (0, j))],
        out_shape=[jax.ShapeDtypeStruct((m, n_cols), out_dtype), jax.ShapeDtypeStruct((ms, n_cols), out_dtype)],
        scratch_shapes=[pltpu.VMEM((k, tn), _BF) for _ in ws],
        compiler_params=_params(("arbitrary", "arbitrary")),
        name=name,
    )(*args)


def _merge_kernel(oa_ref, oas_ref, y_ref, ys_ref, wa_ref, w1_ref, w2_ref, ga_ref, gb_ref, gas_ref, gbs_ref,
                  o_ref, os_ref, wab, w1b, w2b):
    i = pl.program_id(1)

    @pl.when(i == 0)
    def _():
        wab[...] = wa_ref[...].astype(_BF)
        w1b[...] = w1_ref[...].astype(_BF)
        w2b[...] = w2_ref[...].astype(_BF)

    def compute(oa, y, ga, gb, out_ref):
        pa = jnp.dot(oa, wab[...], preferred_element_type=_F32)
        ab1 = jnp.dot(y, w1b[...], preferred_element_type=_F32)
        ab2 = jnp.dot(y, w2b[...], preferred_element_type=_F32)
        pb = ab1 * _sigmoid(ab2)
        out_ref[...] = (_sigmoid(ga) * pa + _sigmoid(gb) * pb).astype(out_ref.dtype)

    compute(oa_ref[...], y_ref[...], ga_ref[...], gb_ref[...], o_ref)

    @pl.when(i == 0)
    def _():
        compute(oas_ref[...], ys_ref[...], gas_ref[...], gbs_ref[...], os_ref)


def _merge(oa, oas, y, ys, w_attn_out, w_glu, layer, zug, zug_s, d_model, d_ssm, *, tm, tn):
    m, ka = oa.shape
    ks = y.shape[1]
    ms = oas.shape[0]
    nj, ni = d_model // tn, m // tm
    off_a = d_ssm // tn
    off_b = (d_ssm + d_model) // tn
    nglu = d_model // tn
    return pl.pallas_call(
        _merge_kernel,
        grid=(nj, ni),
        in_specs=[
            pl.BlockSpec((tm, ka), lambda j, i: (i, 0)), pl.BlockSpec((ms, ka), lambda j, i: (0, 0)),
            pl.BlockSpec((tm, ks), lambda j, i: (i, 0)), pl.BlockSpec((ms, ks), lambda j, i: (0, 0)),
            pl.BlockSpec((None, ka, tn), lambda j, i: (layer, 0, j)),
            pl.BlockSpec((None, ks, tn), lambda j, i: (layer, 0, j)),
            pl.BlockSpec((None, ks, tn), lambda j, i: (layer, 0, j + nglu)),
            pl.BlockSpec((tm, tn), lambda j, i: (i, j + off_a)), pl.BlockSpec((tm, tn), lambda j, i: (i, j + off_b)),
            pl.BlockSpec((ms, tn), lambda j, i: (0, j + off_a)), pl.BlockSpec((ms, tn), lambda j, i: (0, j + off_b)),
        ],
        out_specs=[pl.BlockSpec((tm, tn), lambda j, i: (i, j)), pl.BlockSpec((ms, tn), lambda j, i: (0, j))],
        out_shape=[jax.ShapeDtypeStruct((m, d_model), _BF), jax.ShapeDtypeStruct((ms, d_model), _BF)],
        scratch_shapes=[pltpu.VMEM((ka, tn), _BF), pltpu.VMEM((ks, tn), _BF), pltpu.VMEM((ks, tn), _BF)],
        compiler_params=_params(("arbitrary", "arbitrary")),
        name="merge",
    )(oa, oas, y, ys, w_attn_out, w_glu, w_glu, zug, zug, zug_s, zug_s)


def _down_kernel(a_ref, as_ref, w_ref, res_ref, ress_ref, o_ref, os_ref):
    k = pl.program_id(2)
    wb = w_ref[...].astype(_BF)
    part = jnp.dot(a_ref[...], wb, preferred_element_type=_F32)

    @pl.when(k == 0)
    def _():
        o_ref[...] = res_ref[...] + part

    @pl.when(k > 0)
    def _():
        o_ref[...] += part

    part_s = jnp.dot(as_ref[...], wb, preferred_element_type=_F32)

    @pl.when(k == 0)
    def _():
        os_ref[...] = ress_ref[...] + part_s

    @pl.when(k > 0)
    def _():
        os_ref[...] += part_s


def _ffn_down(a, a_s, w2, layer, res, res_s, *, tm, tn, tk):
    m, f = a.shape
    ms = a_s.shape[0]
    n = w2.shape[2]
    ni = m // tm
    out, out_s = pl.pallas_call(
        _down_kernel,
        grid=(ni, n // tn, f // tk),
        in_specs=[
            pl.BlockSpec((tm, tk), lambda i, j, k: (i, k)), pl.BlockSpec((ms, tk), lambda i, j, k: (0, k)),
            pl.BlockSpec((None, tk, tn), lambda i, j, k: (layer, k, j)),
            pl.BlockSpec((tm, tn), lambda i, j, k: (i, j)), pl.BlockSpec((ms, tn), lambda i, j, k: (0, j)),
        ],
        out_specs=[pl.BlockSpec((tm, tn), lambda i, j, k: (i, j)),
                   pl.BlockSpec((None, ms, tn), lambda i, j, k: (i, 0, j))],
        out_shape=[jax.ShapeDtypeStruct((m, n), _F32), jax.ShapeDtypeStruct((ni, ms, n), _F32)],
        compiler_params=_params(("arbitrary", "arbitrary", "arbitrary")),
        name="ffn_down",
    )(a, a_s, w2, res, res_s)
    return out, out_s[0]


def _masked_softmax(s, mask):
    s = jnp.where(mask, s, -jnp.inf)
    m = jnp.max(s, axis=-1, keepdims=True)
    m = jnp.where(jnp.isfinite(m), m, 0.0)
    p = jnp.where(mask, jnp.exp(s - m), 0.0)
    den = jnp.maximum(jnp.sum(p, axis=-1, keepdims=True), 1e-30)
    return p * (1.0 / den)


def _softmax_exp2(s2):
    m = jnp.max(s2, axis=-1, keepdims=True)
    m = jnp.where(jnp.isfinite(m), m, 0.0)
    p = jnp.exp2(s2 - m)
    return p, 1.0 / jnp.maximum(jnp.sum(p, axis=-1, keepdims=True), 1e-30)


def _dot_t(a, b, precision=None):
    return lax.dot_general(a, b, (((1,), (1,)), ((), ())), precision=precision, preferred_element_type=_F32)


def _nsa_prompt_kernel(slope_ref, q_ref, kc_ref, vc_ref, ks_ref, vs_ref, kw_ref, vw_ref, gate_ref,
                       pk_ref, pv_ref, e_ref, o_ref, kcp, vcp, ksb, vsb, kwb, vwb, osel, *, tq, seq, nb, band):
    kh = pl.program_id(1)
    qi = pl.program_id(2)

    @pl.when(qi == 0)
    def _():
        kcp[...] = jnp.dot(pk_ref[...].astype(_BF), kc_ref[...].astype(_BF),
                           preferred_element_type=_F32).astype(_BF)
        vcp[...] = jnp.dot(pv_ref[...].astype(_BF), vc_ref[...].astype(_BF),
                           preferred_element_type=_F32).astype(_BF)
        ksb[...] = ks_ref[...].astype(_BF)
        vsb[...] = vs_ref[...].astype(_BF)
        kwb[...] = kw_ref[...].astype(_BF)
        vwb[...] = vw_ref[...].astype(_BF)

    t0 = qi * tq
    qpos = t0 + lax.broadcasted_iota(jnp.int32, (tq, 1), 0)
    qpos_f = qpos.astype(_F32)
    q = q_ref[...]
    gate = _sigmoid(gate_ref[...])

    blk = lax.broadcasted_iota(jnp.int32, (tq, nb), 1)
    center = (blk * BLOCK).astype(_F32) + 0.5 * (BLOCK - 1)
    valid_c = ((blk + 1) * BLOCK - 1) <= qpos
    dist_c = qpos_f - center
    kc = kcp[...]
    vcb = vcp[...]
    o_cmp = []
    imp = jnp.zeros((tq, nb), _F32)
    for g in range(GROUP):
        slope = slope_ref[kh * GROUP + g]
        qg = q[:, g * HEAD_DIM:(g + 1) * HEAD_DIM].astype(_BF)
        s = _dot_t(qg, kc) * SCALE - slope * dist_c
        p = _masked_softmax(s, valid_c)
        imp = imp + p
        o_cmp.append(jnp.dot(p.astype(_BF), vcb, preferred_element_type=_F32))

    cur = qpos // BLOCK
    forced = (blk == cur) | (blk == cur - 1) | (blk == 0)
    score = jnp.where(forced, FORCED_SCORE, jnp.where(blk > cur, -jnp.inf, imp))
    rank = jnp.zeros((tq, nb), _F32)
    for i in range(nb):
        col = score[:, i:i + 1]
        beats = (col > score) | ((col == score) & (blk > i))
        rank = rank + jnp.where(beats, 1.0, 0.0)
    sel = jnp.where(rank < float(min(N_SELECT, nb)), 1.0, 0.0).astype(_BF)
    sel_keys = jnp.dot(sel, e_ref[...], preferred_element_type=_F32) > 0.5

    kpos = lax.broadcasted_iota(jnp.int32, (1, seq), 1)
    dist_s = qpos - kpos
    neg_s = jnp.where(sel_keys & (dist_s >= 0), 0.0, -jnp.inf)
    dist_sf = dist_s.astype(_F32)

    ws = pl.multiple_of(jnp.maximum(t0 - WINDOW, 0), math.gcd(tq, WINDOW))
    kpos_w = ws + lax.broadcasted_iota(jnp.int32, (1, band), 1)
    dist_w = qpos - kpos_w
    neg_w = jnp.where((dist_w >= 0) & (dist_w < WINDOW), 0.0, -jnp.inf)
    dist_wf = dist_w.astype(_F32)
    kw_band = kwb[pl.ds(ws, band), :]
    vw_band = vwb[pl.ds(ws, band), :]

    nq = seq // tq
    n_var = 4 if nq % 4 == 0 else 1
    for v in range(n_var):
        klen = seq * (v + 1) // n_var

        @pl.when((qi * n_var) // nq == v)
        def _(klen=klen):
            for g in range(GROUP):
                c2 = slope_ref[kh * GROUP + g] * LOG2E
                qg = q[:, g * HEAD_DIM:(g + 1) * HEAD_DIM].astype(_BF)
                s2 = _dot_t(qg, ksb[:klen, :]) * (SCALE * LOG2E) + (neg_s[:, :klen] - c2 * dist_sf[:, :klen])
                p, inv = _softmax_exp2(s2)
                osel[:, g * HEAD_DIM:(g + 1) * HEAD_DIM] = inv * jnp.dot(p.astype(_BF), vsb[:klen, :],
                                                                         preferred_element_type=_F32)

    for g in range(GROUP):
        c2 = slope_ref[kh * GROUP + g] * LOG2E
        qg = q[:, g * HEAD_DIM:(g + 1) * HEAD_DIM].astype(_BF)
        o_sel = osel[:, g * HEAD_DIM:(g + 1) * HEAD_DIM]
        s2 = _dot_t(qg, kw_band) * (SCALE * LOG2E) + (neg_w - c2 * dist_wf)
        p, inv = _softmax_exp2(s2)
        o_win = inv * jnp.dot(p.astype(_BF), vw_band, preferred_element_type=_F32)
        o = (gate[:, g:g + 1] * o_cmp[g] + gate[:, GROUP + g:GROUP + g + 1] * o_sel
             + gate[:, 2 * GROUP + g:2 * GROUP + g + 1] * o_win)
        o_ref[:, g * HEAD_DIM:(g + 1) * HEAD_DIM] = o.astype(o_ref.dtype)


def _alibi_slopes():
    h = np.arange(1, N_HEADS + 1, dtype=np.float32)
    return jnp.asarray(np.exp2(-ALIBI_MAX_EXP * h / N_HEADS).astype(np.float32))


def _nsa_prompt(zqkv, gates_r, pool_k, pool_v, expand, batch, seq):
    nb = seq // BLOCK
    tq = _tile(seq, 256)
    nq = seq // tq
    band = min(WINDOW + tq, seq)
    d_attn = N_HEADS * HEAD_DIM
    kv0 = d_attn // HEAD_DIM

    def kv_spec(branch, which):
        base = kv0 + (branch * 2 + which) * N_KV_HEADS
        return pl.BlockSpec((seq, HEAD_DIM), lambda b, k, qi, sl: (b, base + k))

    grid_spec = pltpu.PrefetchScalarGridSpec(
        num_scalar_prefetch=1,
        grid=(batch, N_KV_HEADS, nq),
        in_specs=[
            pl.BlockSpec((tq, GROUP * HEAD_DIM), lambda b, k, qi, sl: (b * nq + qi, k)),
            kv_spec(0, 0), kv_spec(0, 1), kv_spec(1, 0), kv_spec(1, 1), kv_spec(2, 0), kv_spec(2, 1),
            pl.BlockSpec((None, tq, 3 * GROUP), lambda b, k, qi, sl: (k, b * nq + qi, 0)),
            pl.BlockSpec((None, nb, seq), lambda b, k, qi, sl: (k, 0, 0)),
            pl.BlockSpec((None, nb, seq), lambda b, k, qi, sl: (k, 0, 0)),
            pl.BlockSpec((nb, seq), lambda b, k, qi, sl: (0, 0)),
        ],
        out_specs=pl.BlockSpec((tq, GROUP * HEAD_DIM), lambda b, k, qi, sl: (b * nq + qi, k)),
        scratch_shapes=[pltpu.VMEM((nb, HEAD_DIM), _BF), pltpu.VMEM((nb, HEAD_DIM), _BF)]
        + [pltpu.VMEM((seq, HEAD_DIM), _BF) for _ in range(4)] + [pltpu.VMEM((tq, GROUP * HEAD_DIM), _F32)],
    )
    return pl.pallas_call(
        functools.partial(_nsa_prompt_kernel, tq=tq, seq=seq, nb=nb, band=band),
        grid_spec=grid_spec,
        out_shape=jax.ShapeDtypeStruct((batch * seq, d_attn), _BF),
        compiler_params=_params(("arbitrary", "arbitrary", "arbitrary")),
        name="nsa_prompt",
    )(_alibi_slopes(), zqkv, zqkv, zqkv, zqkv, zqkv, zqkv, zqkv, gates_r, pool_k, pool_v, expand)


def _pool_pages_kernel(pt_ref, *refs, pages):
    w_ref, o_ref = refs[pages], refs[pages + 1]
    w = w_ref[...].astype(_BF).astype(_F32)
    per = PAGE_SIZE // BLOCK
    for r in range(pages):
        x = refs[r][...].astype(_BF).astype(_F32)
        for n in range(per):
            o_ref[r * per + n] = jnp.sum(x[n * BLOCK:(n + 1) * BLOCK] * w, axis=0).astype(_BF).astype(_F32)


def _pool_pages(cache, page_table, w_exp, layer):
    db, n_pages = page_table.shape
    per = PAGE_SIZE // BLOCK
    row = cache.shape[3:]
    pages = math.gcd(n_pages, 8)

    def page_spec(r):
        return pl.BlockSpec((None, None, PAGE_SIZE) + row,
                            lambda b, p, pt: (layer, pt[b * n_pages + p * pages + r], 0, 0, 0))

    grid_spec = pltpu.PrefetchScalarGridSpec(
        num_scalar_prefetch=1,
        grid=(db, n_pages // pages),
        in_specs=[page_spec(r) for r in range(pages)] + [pl.BlockSpec((BLOCK,) + row, lambda b, p, pt: (0, 0, 0))],
        out_specs=pl.BlockSpec((None, pages * per) + row, lambda b, p, pt: (b, p, 0, 0)),
    )
    return pl.pallas_call(
        functools.partial(_pool_pages_kernel, pages=pages),
        grid_spec=grid_spec,
        out_shape=jax.ShapeDtypeStruct((db, n_pages * per) + row, _F32),
        compiler_params=_params(("arbitrary", "arbitrary")),
        name="pool_pages",
    )(page_table.reshape(-1), *([cache] * pages), w_exp)


def _cmp_select_kernel(slope_ref, q_ref, pool_ref, new_ref, w0_ref, oc_ref, idx_ref, *, past, n_sel):
    nbp = pool_ref.shape[0]
    qpos = past
    q = q_ref[...]
    pooled_new = new_ref[...] * w0_ref[...]
    blk = lax.broadcasted_iota(jnp.int32, (1, nbp), 1)
    center = (blk * BLOCK).astype(_F32) + 0.5 * (BLOCK - 1)
    valid = ((blk + 1) * BLOCK - 1) <= qpos
    dist = float(qpos) - center
    center_n = float(nbp * BLOCK) + 0.5 * (BLOCK - 1)
    valid_n = jnp.full((1, 1), (nbp + 1) * BLOCK - 1 <= qpos)
    dist_n = float(qpos) - center_n
    rows_i = lax.broadcasted_iota(jnp.int32, (nbp, nbp), 0)
    cols_i = lax.broadcasted_iota(jnp.int32, (nbp, nbp), 1)
    sel_r = lax.broadcasted_iota(jnp.int32, (n_sel, nbp), 0).astype(_F32)
    blk_f = blk.astype(_F32)
    cur = qpos // BLOCK
    for kh in range(N_KV_HEADS):
        kc = pool_ref[:, (2 * kh) * HEAD_DIM:(2 * kh + 1) * HEAD_DIM]
        vc = pool_ref[:, (2 * kh + 1) * HEAD_DIM:(2 * kh + 2) * HEAD_DIM]
        kn = pooled_new[:, (2 * kh) * HEAD_DIM:(2 * kh + 1) * HEAD_DIM]
        vn = pooled_new[:, (2 * kh + 1) * HEAD_DIM:(2 * kh + 2) * HEAD_DIM]
        qk = q[kh * GROUP:(kh + 1) * GROUP]
        slope = jnp.concatenate([jnp.full((1, 1), slope_ref[kh * GROUP + g], _F32) for g in range(GROUP)], axis=0)
        s = _dot_t(qk.astype(_BF), kc.astype(_BF)) * SCALE - slope * dist
        s_n = jnp.sum(qk * kn, axis=-1, keepdims=True) * SCALE - slope * dist_n
        s = jnp.where(valid, s, -jnp.inf)
        s_n = jnp.where(valid_n, s_n, -jnp.inf)
        m = jnp.maximum(jnp.max(s, axis=-1, keepdims=True), s_n)
        m = jnp.where(jnp.isfinite(m), m, 0.0)
        p = jnp.where(valid, jnp.exp(s - m), 0.0)
        p_n = jnp.where(valid_n, jnp.exp(s_n - m), 0.0)
        den = jnp.maximum(jnp.sum(p, axis=-1, keepdims=True) + p_n, 1e-30)
        inv = 1.0 / den
        p = p * inv
        p_n = p_n * inv
        oc_ref[kh * GROUP:(kh + 1) * GROUP, :] = (
            jnp.dot(p.astype(_BF), vc.astype(_BF), preferred_element_type=_F32) + p_n * vn)
        imp = jnp.sum(p, axis=0, keepdims=True)
        imp_n = jnp.sum(p_n, axis=0, keepdims=True)
        forced = (blk == cur) | (blk == cur - 1) | (blk == 0)
        score = jnp.where(forced, FORCED_SCORE, jnp.where(blk > cur, -jnp.inf, imp))
        forced_n = (nbp == cur) | (nbp == cur - 1) | (nbp == 0)
        score_n = jnp.where(forced_n, FORCED_SCORE, jnp.where(nbp > cur, -jnp.inf, imp_n))
        sj = jnp.broadcast_to(score, (nbp, nbp))
        si = jnp.transpose(sj)
        ahead = (si > sj) | ((si == sj) & (rows_i < cols_i))
        rank = jnp.sum(jnp.where(ahead, 1.0, 0.0), axis=0, keepdims=True) + jnp.where(score_n > score, 1.0, 0.0)
        rank_n = jnp.sum(jnp.where(score >= score_n, 1.0, 0.0), axis=-1, keepdims=True)
        onehot = jnp.where(rank == sel_r, 1.0, 0.0)
        sel_col = lax.broadcasted_iota(jnp.int32, (n_sel, 1), 0).astype(_F32)
        idx = jnp.sum(onehot * blk_f, axis=-1, keepdims=True) + jnp.where(rank_n == sel_col, float(nbp), 0.0)
        idx_ref[kh] = jnp.broadcast_to(idx, (n_sel, LANE)).astype(jnp.int32)


def _cmp_select(q_s, pooled, new_rows, w0, past, n_sel):
    db, nbp, width = pooled.shape
    grid_spec = pltpu.PrefetchScalarGridSpec(
        num_scalar_prefetch=1,
        grid=(db,),
        in_specs=[
            pl.BlockSpec((None, N_HEADS, HEAD_DIM), lambda b, sl: (b, 0, 0)),
            pl.BlockSpec((None, nbp, width), lambda b, sl: (b, 0, 0)),
            pl.BlockSpec((None, 1, width), lambda b, sl: (b, 0, 0)),
            pl.BlockSpec((1, width), lambda b, sl: (0, 0)),
        ],
        out_specs=[
            pl.BlockSpec((None, N_HEADS, HEAD_DIM), lambda b, sl: (b, 0, 0)),
            pl.BlockSpec((None, N_KV_HEADS, n_sel, LANE), lambda b, sl: (b, 0, 0, 0)),
        ],
    )
    return pl.pallas_call(
        functools.partial(_cmp_select_kernel, past=past, n_sel=n_sel),
        grid_spec=grid_spec,
        out_shape=[jax.ShapeDtypeStruct((db, N_HEADS, HEAD_DIM), _F32),
                   jax.ShapeDtypeStruct((db, N_KV_HEADS, n_sel, LANE), jnp.int32)],
        compiler_params=_params(("arbitrary",)),
        name="cmp_select",
    )(_alibi_slopes(), q_s, pooled, new_rows, w0)


def _sel_sample_kernel(idx_ref, pt_ref, slope_ref, q_ref, *refs, past, n_sel):
    blk_refs, new_ref, o_ref, rows_ref = refs[:n_sel], refs[n_sel], refs[n_sel + 1], refs[n_sel + 2]
    b = pl.program_id(0)
    kh = pl.program_id(1)
    base = (b * N_KV_HEADS + kh) * n_sel
    rw = 2 * N_KV_HEADS
    row_i = lax.broadcasted_iota(jnp.int32, (BLOCK, 1, 1), 0)
    for jj in range(n_sel):
        kpos_c = idx_ref[base + jj] * BLOCK + row_i
        rows_ref[jj * BLOCK:(jj + 1) * BLOCK] = jnp.where(kpos_c < past, blk_refs[jj][...], new_ref[...])
    qpos = past
    n_rows = n_sel * BLOCK * rw
    rows = rows_ref[...].reshape(n_rows, HEAD_DIM).astype(_BF)
    col = lax.broadcasted_iota(jnp.int32, (1, n_rows), 1)
    pos = col // rw
    blk_of = jnp.zeros((1, n_rows), jnp.int32)
    for jj in range(n_sel):
        blk_of = jnp.where(pos // BLOCK == jj, idx_ref[base + jj], blk_of)
    dist = qpos - (blk_of * BLOCK + pos % BLOCK)
    mask = (dist >= 0) & (col % rw == 2 * kh)
    slope = jnp.concatenate([jnp.full((1, 1), slope_ref[kh * GROUP + g], _F32) for g in range(GROUP)], axis=0)
    s = _dot_t(q_ref[...].astype(_BF), rows) * SCALE - slope * dist.astype(_F32)
    p = _masked_softmax(s, mask)
    p_val = pltpu.roll(p, 1, 1)
    o_ref[...] = jnp.dot(p_val.astype(_BF), rows, preferred_element_type=_F32)


def _sel_sample(q_s, cache, idx, page_table, new_rows, layer, past, n_sel):
    db, n_pages = page_table.shape
    rw = 2 * N_KV_HEADS

    def blk_spec(j):
        def blk_map(b, k, idx_ref, pt_ref, sl):
            blk = idx_ref[(b * N_KV_HEADS + k) * n_sel + j]
            pos = jnp.minimum(blk * BLOCK, past - 1)
            return (layer, pt_ref[b * n_pages + pos // PAGE_SIZE], (pos % PAGE_SIZE) // BLOCK, 0, 0)
        return pl.BlockSpec((None, None, BLOCK, rw, HEAD_DIM), blk_map)

    grid_spec = pltpu.PrefetchScalarGridSpec(
        num_scalar_prefetch=3,
        grid=(db, N_KV_HEADS),
        in_specs=[pl.BlockSpec((None, GROUP, HEAD_DIM), lambda b, k, *_: (b * N_KV_HEADS + k, 0, 0))]
        + [blk_spec(j) for j in range(n_sel)]
        + [pl.BlockSpec((None, 1, rw, HEAD_DIM), lambda b, k, *_: (b, 0, 0, 0))],
        out_specs=pl.BlockSpec((None, GROUP, HEAD_DIM), lambda b, k, *_: (b * N_KV_HEADS + k, 0, 0)),
        scratch_shapes=[pltpu.VMEM((n_sel * BLOCK, rw, HEAD_DIM), _F32)],
    )
    return pl.pallas_call(
        functools.partial(_sel_sample_kernel, past=past, n_sel=n_sel),
        grid_spec=grid_spec,
        out_shape=jax.ShapeDtypeStruct((db * N_KV_HEADS, GROUP, HEAD_DIM), _F32),
        compiler_params=_params(("arbitrary", "arbitrary")),
        name="sel_sample",
    )(idx, page_table.reshape(-1), _alibi_slopes(), q_s, *([cache] * n_sel), new_rows)


def _win_combine_kernel(slope_ref, q_ref, win_ref, new_ref, oc_ref, os_ref, gate_ref, o_ref, *, past, w_buf):
    qpos = past
    q = q_ref[...]
    gate = _sigmoid(gate_ref[...])
    kpos = past - w_buf + lax.broadcasted_iota(jnp.int32, (1, w_buf), 1)
    dist = qpos - kpos
    mask = (dist >= 0) & (dist < WINDOW) & (kpos >= 0)
    dist_f = dist.astype(_F32)
    new = new_ref[...]
    mask_n = jnp.full((1, 1), 0 < WINDOW)
    for kh in range(N_KV_HEADS):
        kw = win_ref[:, (2 * kh) * HEAD_DIM:(2 * kh + 1) * HEAD_DIM].astype(_BF)
        vw = win_ref[:, (2 * kh + 1) * HEAD_DIM:(2 * kh + 2) * HEAD_DIM].astype(_BF)
        kn = new[:, (2 * kh) * HEAD_DIM:(2 * kh + 1) * HEAD_DIM]
        vn = new[:, (2 * kh + 1) * HEAD_DIM:(2 * kh + 2) * HEAD_DIM]
        hs = slice(kh * GROUP, (kh + 1) * GROUP)
        qk = q[hs]
        slope = jnp.concatenate([jnp.full((1, 1), slope_ref[kh * GROUP + g], _F32) for g in range(GROUP)], axis=0)
        s = _dot_t(qk.astype(_BF), kw) * SCALE - slope * dist_f
        qb = qk.astype(_BF).astype(_F32)
        s_n = jnp.sum(qb * kn.astype(_BF).astype(_F32), axis=-1, keepdims=True) * SCALE
        s = jnp.where(mask, s, -jnp.inf)
        s_n = jnp.where(mask_n, s_n, -jnp.inf)
        m = jnp.maximum(jnp.max(s, axis=-1, keepdims=True), s_n)
        m = jnp.where(jnp.isfinite(m), m, 0.0)
        p = jnp.where(mask, jnp.exp(s - m), 0.0)
        p_n = jnp.where(mask_n, jnp.exp(s_n - m), 0.0)
        inv = 1.0 / jnp.maximum(jnp.sum(p, axis=-1, keepdims=True) + p_n, 1e-30)
        o_w = (jnp.dot((p * inv).astype(_BF), vw, preferred_element_type=_F32)
               + (p_n * inv).astype(_BF).astype(_F32) * vn.astype(_BF).astype(_F32))
        g3 = gate[hs]
        o_ref[hs, :] = g3[:, 0:1] * oc_ref[hs, :] + g3[:, 1:2] * os_ref[hs, :] + g3[:, 2:3] * o_w


def _win_combine(q_s, win, new_rows, o_cmp, o_sel, gates, past):
    db, w_buf, width = win.shape
    head = pl.BlockSpec((None, N_HEADS, HEAD_DIM), lambda b, sl: (b, 0, 0))
    grid_spec = pltpu.PrefetchScalarGridSpec(
        num_scalar_prefetch=1,
        grid=(db,),
        in_specs=[
            head,
            pl.BlockSpec((None, w_buf, width), lambda b, sl: (b, 0, 0)),
            pl.BlockSpec((None, 1, width), lambda b, sl: (b, 0, 0)),
            head, head,
            pl.BlockSpec((None, N_HEADS, 3), lambda b, sl: (b, 0, 0)),
        ],
        out_specs=head,
    )
    return pl.pallas_call(
        functools.partial(_win_combine_kernel, past=past, w_buf=w_buf),
        grid_spec=grid_spec,
        out_shape=jax.ShapeDtypeStruct((db, N_HEADS, HEAD_DIM), _F32),
        compiler_params=_params(("arbitrary",)),
        name="win_combine",
    )(_alibi_slopes(), q_s, win, new_rows, o_cmp, o_sel, gates)


def _ssm_discretise(a_re, a_im, log_dt, b_re, b_im):
    dt = jnp.exp(log_dt)[:, None]
    mag = jnp.exp(dt * a_re)
    abar_re = mag * jnp.cos(dt * a_im)
    abar_im = mag * jnp.sin(dt * a_im)
    den = a_re * a_re + a_im * a_im
    cf_re = ((abar_re - 1.0) * a_re + abar_im * a_im) / den
    cf_im = (abar_im * a_re - (abar_re - 1.0) * a_im) / den
    bb_re = cf_re[..., None] * b_re - cf_im[..., None] * b_im
    bb_im = cf_re[..., None] * b_im + cf_im[..., None] * b_re
    return abar_re, abar_im, bb_re, bb_im, dt * a_re, dt * a_im


def _ssm_chunk_matrices(dta_re, dta_im, bb_re, bb_im, c_re, c_im):
    L = SSM_CHUNK
    g, p, ch = bb_re.shape
    nq = LANE // ch
    nblk = g // nq
    lags = jnp.arange(L + 1, dtype=_F32)[None, :, None]
    pw_mag = jnp.exp(lags * dta_re[:, None, :])
    pw_re = pw_mag * jnp.cos(lags * dta_im[:, None, :])
    pw_im = pw_mag * jnp.sin(lags * dta_im[:, None, :])
    bt_re = bb_re.transpose(0, 2, 1)[:, None]
    bt_im = bb_im.transpose(0, 2, 1)[:, None]
    e_re = pw_re[:, :L, None, :] * bt_re - pw_im[:, :L, None, :] * bt_im
    e_im = pw_re[:, :L, None, :] * bt_im + pw_im[:, :L, None, :] * bt_re
    k_lag = (jnp.einsum("gcp,gldp->gldc", c_re, e_re, precision=_HI)
             - jnp.einsum("gcp,gldp->gldc", c_im, e_im, precision=_HI))

    def lane_diag(x):
        w = x.shape[-1]
        flat = x.transpose(0, 2, 1, 3, 4).reshape(nblk, L * LANE, w)
        mask = np.broadcast_to(np.eye(nq, dtype=np.float32)[None, :, None, :, None], (L, nq, ch, nq, w))
        return jnp.tile(flat, (1, 1, nq)) * jnp.asarray(mask.reshape(L * LANE, nq * w))

    bd = lane_diag(k_lag.reshape(nblk, nq, L, ch, ch)).reshape(nblk, L, LANE, LANE)

    def state_in(e):
        return lane_diag(e[:, ::-1].reshape(nblk, nq, L, ch, p))

    pr = pw_re[:, 1:, None, :]
    pi = pw_im[:, 1:, None, :]

    def state_out_t(v):
        return lane_diag(v.reshape(nblk, nq, L, ch, p))

    vt_re = state_out_t(c_re[:, None] * pr - c_im[:, None] * pi)
    vt_im = state_out_t(-(c_re[:, None] * pi + c_im[:, None] * pr))
    al_re = pw_re[:, L].reshape(nblk, 1, nq * p)
    al_im = pw_im[:, L].reshape(nblk, 1, nq * p)
    return (bd.astype(_BF), state_in(e_re).astype(_BF), state_in(e_im).astype(_BF),
            vt_re.astype(_BF), vt_im.astype(_BF), al_re, al_im)


def _ssm_prompt_kernel(u_ref, bd_ref, wre_ref, wim_ref, vtre_ref, vtim_ref, alre_ref, alim_ref, d_ref,
                       y_ref, hre_ref, him_ref, msc, xs, ys, loc_re, loc_im, hp_re, hp_im, *, n_chunk):
    L = SSM_CHUNK

    @pl.when(pl.program_id(1) == 0)
    def _():
        for j in range(L):
            for t in range(L):
                blk = bd_ref[t - j] if t >= j else jnp.zeros((LANE, LANE), _BF)
                msc[j * LANE:(j + 1) * LANE, t * LANE:(t + 1) * LANE] = blk

    for j in range(L):
        xs[:, j * LANE:(j + 1) * LANE] = u_ref[pl.ds(j, n_chunk, stride=L), :].astype(_BF)
    xb = xs[...]
    y = jnp.dot(xb, msc[...], preferred_element_type=_F32)
    loc_re[...] = jnp.dot(xb, wre_ref[...], preferred_element_type=_F32)
    loc_im[...] = jnp.dot(xb, wim_ref[...], preferred_element_type=_F32)
    a_re = alre_ref[...]
    a_im = alim_ref[...]

    def step(c, carry):
        h_re, h_im = carry
        hp_re[pl.ds(c, 1), :] = h_re
        hp_im[pl.ds(c, 1), :] = h_im
        l_re = loc_re[pl.ds(c, 1), :]
        l_im = loc_im[pl.ds(c, 1), :]
        return (a_re * h_re - a_im * h_im + l_re, a_re * h_im + a_im * h_re + l_im)

    zero = jnp.zeros(a_re.shape, _F32)
    h_re, h_im = lax.fori_loop(0, n_chunk, step, (zero, zero))
    y = y + _dot_t(hp_re[...].astype(_BF), vtre_ref[...])
    y = y + _dot_t(hp_im[...].astype(_BF), vtim_ref[...])
    d = d_ref[...]
    for t in range(L):
        x_t = u_ref[pl.ds(t, n_chunk, stride=L), :]
        ys[pl.ds(t, n_chunk, stride=L), :] = y[:, t * LANE:(t + 1) * LANE] + d * x_t
    y_ref[...] = ys[...].astype(y_ref.dtype)
    hre_ref[...] = h_re
    him_ref[...] = h_im


def _ssm_prompt(zug, mats, d_skip, batch, seq, d_ssm):
    bd, wre, wim, vtre, vtim, al_re, al_im = mats
    L = SSM_CHUNK
    nblk = bd.shape[0]
    nq = LANE // GROUP_CH
    n_chunk = seq // L
    sw = nq * P_STATE
    per_blk = lambda shape: pl.BlockSpec((None,) + shape, lambda g, b: (g, 0, 0))
    fin_spec = pl.BlockSpec((None, None, 1, sw), lambda g, b: (g, b, 0, 0))
    y, h_re, h_im = pl.pallas_call(
        functools.partial(_ssm_prompt_kernel, n_chunk=n_chunk),
        grid=(nblk, batch),
        in_specs=[pl.BlockSpec((seq, LANE), lambda g, b: (b, g)),
                  pl.BlockSpec((None, L, LANE, LANE), lambda g, b: (g, 0, 0, 0)),
                  per_blk((L * LANE, sw)), per_blk((L * LANE, sw)), per_blk((L * LANE, sw)), per_blk((L * LANE, sw)),
                  per_blk((1, sw)), per_blk((1, sw)), per_blk((1, LANE))],
        out_specs=[pl.BlockSpec((seq, LANE), lambda g, b: (b, g)), fin_spec, fin_spec],
        out_shape=[jax.ShapeDtypeStruct((batch * seq, d_ssm), _BF),
                   jax.ShapeDtypeStruct((nblk, batch, 1, sw), _F32),
                   jax.ShapeDtypeStruct((nblk, batch, 1, sw), _F32)],
        scratch_shapes=[pltpu.VMEM((L * LANE, L * LANE), _BF), pltpu.VMEM((n_chunk, L * LANE), _BF),
                        pltpu.VMEM((seq, LANE), _F32)]
        + [pltpu.VMEM((n_chunk, sw), _F32) for _ in range(4)],
        compiler_params=_params(("arbitrary", "arbitrary")),
        name="ssm_prompt",
    )(zug, bd, wre, wim, vtre, vtim, al_re, al_im, d_skip.reshape(nblk, 1, LANE))
    fin = lambda h: h.reshape(nblk, batch, nq, P_STATE).transpose(1, 0, 2, 3).reshape(batch, nblk * nq, P_STATE)
    return y, jnp.stack([fin(h_re), fin(h_im)], axis=-1)


def _ssm_sample_kernel(xe_ref, bre_ref, bim_ref, h0r_ref, h0i_ref, ar_ref, ai_ref, cre_ref, cim_ref, seg_ref,
                       xt_ref, d_ref, y_ref, sre_ref, sim_ref):
    ch = xe_ref.shape[0]
    h0r, h0i = h0r_ref[...], h0i_ref[...]
    a_re, a_im = ar_ref[...], ai_ref[...]
    rnd = lambda v: v.astype(_BF).astype(_F32)
    bu_re = rnd(xe_ref[0]) * rnd(bre_ref[0])
    bu_im = rnd(xe_ref[0]) * rnd(bim_ref[0])
    for c in range(1, ch):
        bu_re = bu_re + rnd(xe_ref[c]) * rnd(bre_ref[c])
        bu_im = bu_im + rnd(xe_ref[c]) * rnd(bim_ref[c])
    s_re = bu_re + (a_re * h0r - a_im * h0i)
    s_im = bu_im + (a_re * h0i + a_im * h0r)
    sre_ref[...] = s_re
    sim_ref[...] = s_im
    seg = seg_ref[...]
    sb_re, sb_im = rnd(s_re), rnd(s_im)
    for c in range(ch):
        y_re = jnp.dot(sb_re * rnd(cre_ref[c]), seg, precision=_HI, preferred_element_type=_F32)
        y_im = jnp.dot(sb_im * rnd(cim_ref[c]), seg, precision=_HI, preferred_element_type=_F32)
        y_ref[c] = y_re - y_im + d_ref[c] * xt_ref[c]


def _ssm_sample(u_s, h0, abar_re, abar_im, bb_re, bb_im, c_re, c_im, d_skip):
    db = u_s.shape[0]
    g, p, ch = bb_re.shape
    gp = g * p
    x3 = u_s.reshape(db, g, ch)
    xe = jnp.broadcast_to(x3.transpose(2, 0, 1)[..., None], (ch, db, g, p)).reshape(ch, db, gp)
    flat_b = lambda b: b.transpose(2, 0, 1).reshape(ch, 1, gp)
    flat_c = lambda c: c.transpose(1, 0, 2).reshape(ch, 1, gp)
    seg = jnp.asarray(np.repeat(np.eye(g, dtype=np.float32), p, axis=0))
    xt = x3.transpose(2, 0, 1)
    dt = d_skip.transpose(1, 0).reshape(ch, 1, g)
    y, s_re, s_im = pl.pallas_call(
        _ssm_sample_kernel,
        out_shape=[jax.ShapeDtypeStruct((ch, db, g), _F32), jax.ShapeDtypeStruct((db, gp), _F32),
                   jax.ShapeDtypeStruct((db, gp), _F32)],
        compiler_params=pltpu.CompilerParams(vmem_limit_bytes=VMEM_LIMIT),
        name="ssm_sample",
    )(xe, flat_b(bb_re), flat_b(bb_im), h0[..., 0].reshape(db, gp), h0[..., 1].reshape(db, gp),
      abar_re.reshape(1, gp), abar_im.reshape(1, gp), flat_c(c_re), flat_c(c_im), seg, xt, dt)
    y = y.transpose(1, 2, 0).reshape(db, g * ch)
    h_fin = jnp.stack([s_re.reshape(db, g, p), s_im.reshape(db, g, p)], axis=-1)
    return y, h_fin


def _router_kernel(x_ref, g_ref, wr_ref, idx_ref, gate_ref):
    x = x_ref[...]
    h = x * lax.rsqrt(jnp.mean(x * x, axis=-1, keepdims=True) + EPS) * g_ref[...]
    logits = jnp.dot(h.astype(_BF), wr_ref[...].astype(_BF), preferred_element_type=_F32)
    lane = lax.broadcasted_iota(jnp.int32, logits.shape, 1)
    lane_f = lane.astype(_F32)
    lg = jnp.where(lane < N_EXPERTS, logits, -jnp.inf)
    v1 = jnp.max(lg, axis=-1, keepdims=True)
    i1 = jnp.min(jnp.where(lg == v1, lane_f, float(LANE)), axis=-1, keepdims=True)
    lg2 = jnp.where(lane_f == i1, -jnp.inf, lg)
    v2 = jnp.max(lg2, axis=-1, keepdims=True)
    i2 = jnp.min(jnp.where(lg2 == v2, lane_f, float(LANE)), axis=-1, keepdims=True)
    e = jnp.exp(v2 - v1)
    den = 1.0 + e
    idx_ref[...] = jnp.where(lane == 0, i1, jnp.where(lane == 1, i2, 0.0)).astype(jnp.int32)
    gate_ref[...] = jnp.where(lane == 0, 1.0 / den, jnp.where(lane == 1, e / den, 0.0))


def _router(x, g, wr_pad):
    n, d = x.shape
    tm = _tile(n, 256)
    return pl.pallas_call(
        _router_kernel,
        grid=(n // tm,),
        in_specs=[pl.BlockSpec((tm, d), lambda i: (i, 0)), pl.BlockSpec((1, d), lambda i: (0, 0)),
                  pl.BlockSpec((d, LANE), lambda i: (0, 0))],
        out_specs=[pl.BlockSpec((tm, LANE), lambda i: (i, 0)), pl.BlockSpec((tm, LANE), lambda i: (i, 0))],
        out_shape=[jax.ShapeDtypeStruct((n, LANE), jnp.int32), jax.ShapeDtypeStruct((n, LANE), _F32)],
        compiler_params=_params(("arbitrary",)),
        name="router",
    )(x, g.reshape(1, d), wr_pad)


def _row_copy(src_hbm, row, dst, slot, sem):
    return pltpu.make_async_copy(src_hbm.at[pl.ds(row, 1)], dst.at[pl.ds(slot, 1)], sem)


def _moe_gather_kernel(tok_ref, x_hbm, g_ref, o_ref, buf, sems, *, tm, n_tiles):
    i = pl.program_id(0)
    slot = lax.rem(i, 2)

    def fetch(tile, s):
        def start(r, c):
            _row_copy(x_hbm, tok_ref[tile * tm + r], buf.at[s], r, sems.at[s]).start()
            return c

        lax.fori_loop(0, tm, start, 0)

    @pl.when(i == 0)
    def _():
        fetch(0, 0)

    @pl.when(i + 1 < n_tiles)
    def _():
        fetch(i + 1, 1 - slot)

    def wait(r, c):
        _row_copy(x_hbm, 0, buf.at[slot], r, sems.at[slot]).wait()
        return c

    lax.fori_loop(0, tm, wait, 0)
    x = buf[slot]
    inv = lax.rsqrt(jnp.mean(x * x, axis=-1, keepdims=True) + EPS)
    o_ref[...] = (x * inv * g_ref[...]).astype(o_ref.dtype)


def _moe_gather(x_all, g, row_tok, tm):
    n_rows = row_tok.shape[0]
    d = x_all.shape[1]
    n_tiles = n_rows // tm
    grid_spec = pltpu.PrefetchScalarGridSpec(
        num_scalar_prefetch=1,
        grid=(n_tiles,),
        in_specs=[pl.BlockSpec(memory_space=pl.ANY), pl.BlockSpec((1, d), lambda i, tok: (0, 0))],
        out_specs=pl.BlockSpec((tm, d), lambda i, tok: (i, 0)),
        scratch_shapes=[pltpu.VMEM((2, tm, d), _F32), pltpu.SemaphoreType.DMA((2,))],
    )
    return pl.pallas_call(
        functools.partial(_moe_gather_kernel, tm=tm, n_tiles=n_tiles),
        grid_spec=grid_spec,
        out_shape=jax.ShapeDtypeStruct((n_rows, d), _BF),
        compiler_params=_params(("arbitrary",)),
        name="moe_gather",
    )(row_tok, x_all, g.reshape(1, d))


def _moe_up_kernel(texp_ref, live_ref, plive_ref, fresh_ref, x_ref, w1_ref, w3_ref, o_ref, w1b, w3b):
    i = pl.program_id(1)

    @pl.when(fresh_ref[i] > 0)
    def _():
        w1b[...] = w1_ref[...].astype(_BF)
        w3b[...] = w3_ref[...].astype(_BF)

    @pl.when(live_ref[i] > 0)
    def _():
        x = x_ref[...]
        a = jnp.dot(x, w1b[...], preferred_element_type=_F32)
        b = jnp.dot(x, w3b[...], preferred_element_type=_F32)
        o_ref[...] = ((a * _sigmoid(a)) * b).astype(o_ref.dtype)

    @pl.when(live_ref[i] == 0)
    def _():
        o_ref[...] = jnp.zeros(o_ref.shape, o_ref.dtype)


def _moe_up(xg, w1, w3, layer, tile_exp, live, plive, fresh, *, tm, tf):
    n_rows, d = xg.shape
    f = w1.shape[3]
    n_tiles = n_rows // tm
    grid_spec = pltpu.PrefetchScalarGridSpec(
        num_scalar_prefetch=4,
        grid=(f // tf, n_tiles),
        in_specs=[
            pl.BlockSpec((tm, d), lambda j, i, te, lv, pv, fr: (pv[i], 0)),
            pl.BlockSpec((None, None, d, tf), lambda j, i, te, lv, pv, fr: (layer, te[pv[i]], 0, j)),
            pl.BlockSpec((None, None, d, tf), lambda j, i, te, lv, pv, fr: (layer, te[pv[i]], 0, j)),
        ],
        out_specs=pl.BlockSpec((tm, tf), lambda j, i, te, lv, pv, fr: (i, j)),
        scratch_shapes=[pltpu.VMEM((d, tf), _BF), pltpu.VMEM((d, tf), _BF)],
    )
    return pl.pallas_call(
        _moe_up_kernel,
        grid_spec=grid_spec,
        out_shape=jax.ShapeDtypeStruct((n_rows, f), _BF),
        compiler_params=_params(("arbitrary", "arbitrary")),
        name="moe_up",
    )(tile_exp, live, plive, fresh, xg, w1, w3)


def _moe_down_kernel(texp_ref, nused_ref, a_ref, w_ref, o_ref):
    i = pl.program_id(0)
    k = pl.program_id(2)

    @pl.when(i < nused_ref[0])
    def _():
        part = jnp.dot(a_ref[...], w_ref[...].astype(_BF), preferred_element_type=_F32)

        @pl.when(k == 0)
        def _():
            o_ref[...] = part

        @pl.when(k > 0)
        def _():
            o_ref[...] += part

    @pl.when((i >= nused_ref[0]) & (k == 0))
    def _():
        o_ref[...] = jnp.zeros(o_ref.shape, o_ref.dtype)


def _moe_down(a, w2, layer, tile_exp, n_used, *, tm, tn, tk):
    n_rows, f = a.shape
    d = w2.shape[3]
    n_tiles = n_rows // tm
    nn, nk = d // tn, f // tk

    def live(i, nu):
        return jnp.minimum(i, nu[0] - 1)

    def last(i, x, n_x, nu):
        return jnp.where(i < nu[0], x, n_x - 1)

    grid_spec = pltpu.PrefetchScalarGridSpec(
        num_scalar_prefetch=2,
        grid=(n_tiles, nn, nk),
        in_specs=[
            pl.BlockSpec((tm, tk), lambda i, j, k, te, nu: (live(i, nu), last(i, k, nk, nu))),
            pl.BlockSpec((None, None, tk, tn),
                         lambda i, j, k, te, nu: (layer, te[live(i, nu)], last(i, k, nk, nu), last(i, j, nn, nu))),
        ],
        out_specs=pl.BlockSpec((tm, tn), lambda i, j, k, te, nu: (i, j)),
    )
    return pl.pallas_call(
        _moe_down_kernel,
        grid_spec=grid_spec,
        out_shape=jax.ShapeDtypeStruct((n_rows, d), _F32),
        compiler_params=_params(("arbitrary", "arbitrary", "arbitrary")),
        name="moe_down",
    )(tile_exp, n_used, a, w2)


def _moe_combine_kernel(pos_ref, x_ref, gate_ref, y_hbm, o_ref, b0, b1, sems, *, tm):
    def start(r, c):
        _row_copy(y_hbm, pos_ref[0, 0, 2 * r], b0, r, sems.at[0]).start()
        _row_copy(y_hbm, pos_ref[0, 0, 2 * r + 1], b1, r, sems.at[1]).start()
        return c

    lax.fori_loop(0, tm, start, 0)

    def wait(r, c):
        _row_copy(y_hbm, 0, b0, r, sems.at[0]).wait()
        _row_copy(y_hbm, 0, b1, r, sems.at[1]).wait()
        return c

    lax.fori_loop(0, tm, wait, 0)
    gate = gate_ref[...]
    o_ref[...] = x_ref[...] + (gate[:, 0:1] * b0[...] + gate[:, 1:2] * b1[...])


def _moe_combine(x, gates, pos, y_rows):
    n, d = x.shape
    tm = _tile(n, 256)
    n_tiles = n // tm
    return pl.pallas_call(
        functools.partial(_moe_combine_kernel, tm=tm),
        grid=(n_tiles,),
        in_specs=[pl.BlockSpec((1, 1, 2 * tm), lambda i: (i, 0, 0), memory_space=pltpu.SMEM),
                  pl.BlockSpec((tm, d), lambda i: (i, 0)),
                  pl.BlockSpec((tm, LANE), lambda i: (i, 0)),
                  pl.BlockSpec(memory_space=pl.ANY)],
        out_specs=pl.BlockSpec((tm, d), lambda i: (i, 0)),
        out_shape=jax.ShapeDtypeStruct((n, d), _F32),
        scratch_shapes=[pltpu.VMEM((tm, d), _F32), pltpu.VMEM((tm, d), _F32), pltpu.SemaphoreType.DMA((2,))],
        compiler_params=_params(("arbitrary",)),
        name="moe_combine",
    )(pos.reshape(n_tiles, 1, 2 * tm), x, gates, y_rows)


def _moe_layer(xp, xs, g, w_router, w1, w3, w2, layer):
    n_p, d = xp.shape
    n_s = xs.shape[0]
    n = n_p + n_s
    wr_pad = jnp.pad(w_router, ((0, 0), (0, LANE - N_EXPERTS)))
    idx_p, gate_p = _router(xp, g, wr_pad)
    idx_s, gate_s = _router(xs, g, wr_pad)
    top_i = jnp.concatenate([idx_p[:, :TOP_K], idx_s[:, :TOP_K]], axis=0)
    n_asg = n * TOP_K
    tm = 512 if n_asg >= 4096 else 64
    tm2 = 2 * tm
    n_tiles = 2 * (-(-n_asg // tm2) + N_EXPERTS)
    e_flat = top_i.reshape(n_asg)
    onehot = (e_flat[:, None] == jnp.arange(N_EXPERTS)[None, :]).astype(jnp.int32)
    within = jnp.sum((jnp.cumsum(onehot, axis=0) - onehot) * onehot, axis=1)
    counts = jnp.sum(onehot, axis=0)
    padded = (counts + tm2 - 1) // tm2 * tm2
    pad_end = jnp.cumsum(padded)
    start = pad_end - padded
    dest = start[e_flat] + within
    row_tok = jnp.zeros((n_tiles * tm,), jnp.int32).at[dest].set(jnp.arange(n_asg, dtype=jnp.int32) // TOP_K)
    tile_lo = jnp.arange(n_tiles, dtype=jnp.int32) * tm
    tile_exp = jnp.minimum(jnp.searchsorted(pad_end, tile_lo, side="right"), N_EXPERTS - 1).astype(jnp.int32)
    live = ((tile_lo < pad_end[-1]) & (tile_lo - start[tile_exp] < counts[tile_exp])).astype(jnp.int32)
    tile_id = jnp.arange(n_tiles, dtype=jnp.int32)
    last_live = lax.cummax(jnp.where(live > 0, tile_id, -1))
    before = jnp.concatenate([jnp.full((1,), -1, jnp.int32), last_live[:-1]])
    fresh = ((live > 0) & ((before < 0) | (tile_exp != tile_exp[jnp.maximum(before, 0)]))).astype(jnp.int32)
    plive = jnp.maximum(last_live, 0)
    n_used2 = (pad_end[-1:] // tm2).astype(jnp.int32)
    pos = dest.reshape(n, TOP_K).astype(jnp.int32)

    x_all = jnp.concatenate([xp, xs], axis=0)
    xg = _moe_gather(x_all, g, row_tok, tm)
    f = w1.shape[3]
    a = _moe_up(xg, w1, w3, layer, tile_exp, live, plive, fresh, tm=tm, tf=_tile(f, 512, LANE))
    y_rows = _moe_down(a, w2, layer, tile_exp[::2], n_used2, tm=tm2, tn=_tile(d, 2048, LANE),
                       tk=_tile(f, 512, LANE))
    out_p = _moe_combine(xp, gate_p, pos[:n_p], y_rows)
    out_s = _moe_combine(xs, gate_s, pos[n_p:], y_rows)
    return out_p, out_s


def kernel(x_prompt, x_sample, cache_kv_cmp, cache_kv_sel, cache_kv_win, state_ssm, page_table, norm_mix, norm_ffn, w_in, g_q, g_k, w_cmp, w_attn_out, ssm_a_re, ssm_a_im, ssm_log_dt, ssm_b_re, ssm_b_im, ssm_c_re, ssm_c_im, ssm_d, w_glu, w_out, ffn_w1, ffn_w3, ffn_w2, moe_router, moe_w1, moe_w3, moe_w2):
    batch, seq, d_model = x_prompt.shape
    db, dec_seq, _ = x_sample.shape
    assert dec_seq == 1, "the sample group advances one token per step"
    depth = w_in.shape[0]
    n_pool = cache_kv_cmp.shape[1]
    n_pages = page_table.shape[1]
    past = n_pages * PAGE_SIZE
    d_attn = N_HEADS * HEAD_DIM
    d_kv = N_KV_HEADS * HEAD_DIM
    d_ssm = ssm_d.shape[1] * GROUP_CH
    n_gate = 3 * N_HEADS
    off_gn = d_attn + 6 * d_kv
    off_u = off_gn + n_gate
    n_ug = d_ssm + 2 * d_model
    row_w = N_KV_HEADS * 2 * HEAD_DIM
    nb = seq // BLOCK
    n_p = batch * seq

    tm = _tile(n_p, 1024)
    tn = 512
    assert off_gn % tn == 0 and n_ug % tn == 0 and d_ssm % tn == 0 and d_model % tn == 0

    xp = x_prompt.reshape(n_p, d_model)
    xs = x_sample.reshape(db, d_model)
    w_ug = w_in[:, :, off_u:]
    flag_cols = np.zeros((1, off_gn), np.float32)
    flag_cols[:, :d_attn] = 1.0
    for br in range(3):
        lo = d_attn + br * 2 * d_kv
        flag_cols[:, lo:lo + d_kv] = 1.0
    flag_cols = jnp.asarray(flag_cols)
    expand = jnp.asarray(np.repeat(np.eye(nb, dtype=np.float32), BLOCK, axis=1)).astype(_BF)
    eye_nb = jnp.asarray(np.eye(nb, dtype=np.float32))
    n_sel = min(N_SELECT, past // BLOCK + 1)

    outs = {k: [] for k in ("cmp_p", "cmp_s", "sel_p", "sel_s", "win_p", "win_s", "ssm_p", "ssm_s")}
    for l in range(depth):
        hp = _rmsnorm(xp, norm_mix[l])
        hs = _rmsnorm(xs, norm_mix[l])
        gain = jnp.concatenate([jnp.tile(g_q[l], N_HEADS)]
                               + [jnp.concatenate([jnp.tile(g_k[l, br], N_KV_HEADS), jnp.ones((d_kv,), _F32)])
                                  for br in range(3)]).reshape(1, off_gn)
        zqkv, zqkv_s = _mm(hp, hs, [(w_in, l, 0)], off_gn, mode="headnorm", out_dtype=_F32, tm=tm, tn=tn,
                           gain=gain, flag=flag_cols, name="in_proj_qkv")
        zg, zg_s = _mm(hp, hs, [(w_in, l, off_gn // LANE)], LANE, mode="plain", out_dtype=_F32, tm=tm, tn=LANE,
                       name="in_proj_gate")
        zug, zug_s = _mm(hp, hs, [(w_ug, l, 0)], n_ug, mode="plain", out_dtype=_F32, tm=tm, tn=tn,
                         name="in_proj_ug")

        def kv_rows(z, lead):
            r = z[:, d_attn:off_gn].reshape(lead + (3, 2, N_KV_HEADS, HEAD_DIM))
            return jnp.swapaxes(r, -3, -2)

        rows_p = kv_rows(zqkv, (batch, seq))
        rows_s = kv_rows(zqkv_s, (db, 1))
        outs["cmp_p"].append(rows_p[:, :, 0])
        outs["sel_p"].append(rows_p[:, :, 1])
        outs["win_p"].append(rows_p[:, seq - min(WINDOW, seq):, 2])
        outs["cmp_s"].append(rows_s[:, :, 0])
        outs["sel_s"].append(rows_s[:, :, 1])

        gates_r = zg[:, :n_gate].reshape(n_p, 3, N_KV_HEADS, GROUP).transpose(2, 0, 1, 3).reshape(
            N_KV_HEADS, n_p, 3 * GROUP)
        wk = w_cmp[l].transpose(0, 2, 1)
        pool = (eye_nb[None, None, :, :, None] * wk[:, :, None, None, :]).reshape(2, N_KV_HEADS, nb, seq)
        o_attn = _nsa_prompt(zqkv, gates_r, pool[0], pool[1], expand, batch, seq)

        q_s = zqkv_s[:, :d_attn].reshape(db, N_HEADS, HEAD_DIM)
        tile_view = lambda c: c.reshape(depth, n_pool, PAGE_SIZE, 2 * N_KV_HEADS, HEAD_DIM)
        w_exp = jnp.broadcast_to(w_cmp[l].transpose(1, 2, 0)[..., None],
                                 (BLOCK, N_KV_HEADS, 2, HEAD_DIM)).reshape(BLOCK, 2 * N_KV_HEADS, HEAD_DIM)
        pooled = _pool_pages(tile_view(cache_kv_cmp), page_table, w_exp, l).reshape(db, past // BLOCK, row_w)
        new_c = rows_s[:, :, 0].reshape(db, 1, row_w)
        o_cmp_s, idx = _cmp_select(q_s, pooled, new_c, w_exp[:1].reshape(1, row_w), past, n_sel)
        idx = idx[..., 0].reshape(-1)
        o_sel_s = _sel_sample(q_s.reshape(db * N_KV_HEADS, GROUP, HEAD_DIM), tile_view(cache_kv_sel),
                              idx, page_table, rows_s[:, :, 1].reshape(db, 1, 2 * N_KV_HEADS, HEAD_DIM),
                              l, past, n_sel)
        win = cache_kv_win[l]
        w_buf = win.shape[1]
        new_w = rows_s[:, :, 2]
        gates_s = zg_s[:, :n_gate].reshape(db, 3, N_HEADS).transpose(0, 2, 1)
        o_attn_s = _win_combine(q_s, win.reshape(db, w_buf, row_w), new_w.reshape(db, 1, row_w),
                                o_cmp_s, o_sel_s.reshape(db, N_HEADS, HEAD_DIM), gates_s, past)
        outs["win_s"].append(jnp.concatenate([win, new_w], axis=1)[:, 1:])

        abar_re, abar_im, bb_re, bb_im, dta_re, dta_im = _ssm_discretise(
            ssm_a_re[l], ssm_a_im[l], ssm_log_dt[l], ssm_b_re[l], ssm_b_im[l])
        mats = _ssm_chunk_matrices(dta_re, dta_im, bb_re, bb_im, ssm_c_re[l], ssm_c_im[l])
        y_p, h_p = _ssm_prompt(zug, mats, ssm_d[l], batch, seq, d_ssm)
        y_s, h_s = _ssm_sample(zug_s[:, :d_ssm], state_ssm[l], abar_re, abar_im, bb_re, bb_im,
                               ssm_c_re[l], ssm_c_im[l], ssm_d[l])
        outs["ssm_p"].append(h_p)
        outs["ssm_s"].append(h_s)

        mg, mg_s = _merge(o_attn, o_attn_s.reshape(db, d_attn).astype(_BF), y_p, y_s.astype(_BF),
                          w_attn_out, w_glu, l, zug, zug_s, d_model, d_ssm, tm=_tile(n_p, 512), tn=tn)
        xp, xs = _mm(mg, mg_s, [(w_out, l, 0)], d_model, mode="resid", out_dtype=_F32, tm=tm, tn=tn,
                     res=xp, res_s=xs, name="out_proj")

        if l % 2 == 0:
            j = l // 2
            hp = _rmsnorm(xp, norm_ffn[l])
            hs = _rmsnorm(xs, norm_ffn[l])
            f = ffn_w1.shape[2]
            a, a_s = _mm(hp, hs, [(ffn_w1, j, 0), (ffn_w3, j, 0)], f, mode="swiglu", out_dtype=_BF,
                         tm=_tile(n_p, 512), tn=_tile(f, 512, LANE), name="ffn_up")
            xp, xs = _ffn_down(a, a_s, ffn_w2, j, xp, xs, tm=_tile(n_p, 1024), tn=_tile(d_model, 1024, LANE),
                               tk=_tile(f, 1024, LANE))
        else:
            j = l // 2
            xp, xs = _moe_layer(xp, xs, norm_ffn[l], moe_router[j], moe_w1, moe_w3, moe_w2, j)

    st = lambda k: jnp.stack(outs[k])
    return (xp.reshape(batch, seq, d_model), xs.reshape(db, 1, d_model), st("cmp_p"), st("cmp_s"), st("sel_p"),
            st("sel_s"), st("win_p"), st("win_s"), st("ssm_p"), st("ssm_s"))
```

```python
import functools
import math

import numpy as np
import jax
import jax.numpy as jnp
from jax import lax
from jax.experimental import pallas as pl
from jax.experimental.pallas import tpu as pltpu

HEAD_DIM = 128
N_HEADS = 16
N_KV_HEADS = 4
GROUP = N_HEADS // N_KV_HEADS
BLOCK = 64
N_SELECT = 16
WINDOW = 512
PAGE_SIZE = 128
GROUP_CH = 16
P_STATE = 64
N_EXPERTS = 8
TOP_K = 2
EPS = 1.0e-6
FORCED_SCORE = 1.0e6
ALIBI_MAX_EXP = 8.0
SCALE = HEAD_DIM ** -0.5
LOG2E = 1.4426950408889634
SSM_CHUNK = 16
LANE = 128
SUBLANE = 8
VMEM_LIMIT = 56 << 20

_BF = jnp.bfloat16
_F32 = jnp.float32
_HI = lax.Precision.HIGHEST


def _tile(n, pref, mult=SUBLANE):
    if n <= pref:
        return n
    for t in range(pref - pref % mult, 0, -mult):
        if n % t == 0:
            return t
    return n


def _params(sem):
    return pltpu.CompilerParams(dimension_semantics=sem, vmem_limit_bytes=VMEM_LIMIT)


def _rms_kernel(x_ref, g_ref, o_ref):
    x = x_ref[...]
    inv = lax.rsqrt(jnp.mean(x * x, axis=-1, keepdims=True) + EPS)
    o_ref[...] = (x * inv * g_ref[...]).astype(o_ref.dtype)


def _rmsnorm(x, g, out_dtype=_BF):
    m, d = x.shape
    tm = _tile(m, 256)
    return pl.pallas_call(
        _rms_kernel,
        grid=(m // tm,),
        in_specs=[pl.BlockSpec((tm, d), lambda i: (i, 0)), pl.BlockSpec((1, d), lambda i: (0, 0))],
        out_specs=pl.BlockSpec((tm, d), lambda i: (i, 0)),
        out_shape=jax.ShapeDtypeStruct((m, d), out_dtype),
        compiler_params=_params(("arbitrary",)),
        name="rmsnorm",
    )(x, g.reshape(1, d))


def _sigmoid(x):
    return 1.0 / (1.0 + jnp.exp(-x))


def _mm_kernel(*refs, n_w, mode):
    x_ref, xs_ref = refs[0], refs[1]
    w_refs = refs[2:2 + n_w]
    p = 2 + n_w
    extra = ()
    if mode in ("headnorm", "resid"):
        extra = refs[p:p + 2]
        p += 2
    o_ref, os_ref = refs[p], refs[p + 1]
    wb_refs = refs[p + 2:p + 2 + n_w]
    i = pl.program_id(1)

    @pl.when(i == 0)
    def _():
        for w_ref, wb_ref in zip(w_refs, wb_refs):
            wb_ref[...] = w_ref[...].astype(_BF)

    def compute(x, out_ref, res):
        accs = [jnp.dot(x, wb[...], preferred_element_type=_F32) for wb in wb_refs]
        if mode == "plain":
            out_ref[...] = accs[0].astype(out_ref.dtype)
        elif mode == "swiglu":
            a = accs[0]
            out_ref[...] = ((a * _sigmoid(a)) * accs[1]).astype(out_ref.dtype)
        elif mode == "resid":
            out_ref[...] = (res + accs[0]).astype(out_ref.dtype)
        else:
            acc = accs[0]
            gain_ref, flag_ref = extra
            for c in range(acc.shape[1] // HEAD_DIM):
                sl = slice(c * HEAD_DIM, (c + 1) * HEAD_DIM)
                a = acc[:, sl]
                nrm = a * lax.rsqrt(jnp.mean(a * a, axis=-1, keepdims=True) + EPS) * gain_ref[:, sl]
                out_ref[:, sl] = jnp.where(flag_ref[:, sl] > 0.0, nrm, a).astype(out_ref.dtype)

    compute(x_ref[...], o_ref, extra[0][...] if mode == "resid" else None)

    @pl.when(i == 0)
    def _():
        compute(xs_ref[...], os_ref, extra[1][...] if mode == "resid" else None)


def _mm(x, xs, ws, n_cols, *, mode, out_dtype, tm, tn, gain=None, flag=None, res=None, res_s=None,
        name="mm"):
    m, k = x.shape
    ms = xs.shape[0]
    nj, ni = n_cols // tn, m // tm
    in_specs = [pl.BlockSpec((tm, k), lambda j, i: (i, 0)), pl.BlockSpec((ms, k), lambda j, i: (0, 0))]
    args = [x, xs]
    for w, lid, off in ws:
        in_specs.append(pl.BlockSpec((None, k, tn), functools.partial(lambda j, i, lid, off: (lid, 0, j + off),
                                                                       lid=lid, off=off)))
        args.append(w)
    if mode == "headnorm":
        in_specs += [pl.BlockSpec((1, tn), lambda j, i: (0, j)), pl.BlockSpec((1, tn), lambda j, i: (0, j))]
        args += [gain, flag]
    elif mode == "resid":
        in_specs += [pl.BlockSpec((tm, tn), lambda j, i: (i, j)), pl.BlockSpec((ms, tn), lambda j, i: (0, j))]
        args += [res, res_s]
    return pl.pallas_call(
        functools.partial(_mm_kernel, n_w=len(ws), mode=mode),
        grid=(nj, ni),
        in_specs=in_specs,
        out_specs=[pl.BlockSpec((tm, tn), lambda j, i: (i, j)), pl.BlockSpec((ms, tn), lambda j, i: (0, j))],
        out_shape=[jax.ShapeDtypeStruct((m, n_cols), out_dtype), jax.ShapeDtypeStruct((ms, n_cols), out_dtype)],
        scratch_shapes=[pltpu.VMEM((k, tn), _BF) for _ in ws],
        compiler_params=_params(("arbitrary", "arbitrary")),
        name=name,
    )(*args)


def _merge_kernel(oa_ref, oas_ref, y_ref, ys_ref, wa_ref, w1_ref, w2_ref, ga_ref, gb_ref, gas_ref, gbs_ref,
                  o_ref, os_ref, wab, w1b, w2b):
    i = pl.program_id(1)

    @pl.when(i == 0)
    def _():
        wab[...] = wa_ref[...].astype(_BF)
        w1b[...] = w1_ref[...].astype(_BF)
        w2b[...] = w2_ref[...].astype(_BF)

    def compute(oa, y, ga, gb, out_ref):
        pa = jnp.dot(oa, wab[...], preferred_element_type=_F32)
        ab1 = jnp.dot(y, w1b[...], preferred_element_type=_F32)
        ab2 = jnp.dot(y, w2b[...], preferred_element_type=_F32)
        pb = ab1 * _sigmoid(ab2)
        out_ref[...] = (_sigmoid(ga) * pa + _sigmoid(gb) * pb).astype(out_ref.dtype)

    compute(oa_ref[...], y_ref[...], ga_ref[...], gb_ref[...], o_ref)

    @pl.when(i == 0)
    def _():
        compute(oas_ref[...], ys_ref[...], gas_ref[...], gbs_ref[...], os_ref)


def _merge(oa, oas, y, ys, w_attn_out, w_glu, layer, zug, zug_s, d_model, d_ssm, *, tm, tn):
    m, ka = oa.shape
    ks = y.shape[1]
    ms = oas.shape[0]
    nj, ni = d_model // tn, m // tm
    off_a = d_ssm // tn
    off_b = (d_ssm + d_model) // tn
    nglu = d_model // tn
    return pl.pallas_call(
        _merge_kernel,
        grid=(nj, ni),
        in_specs=[
            pl.BlockSpec((tm, ka), lambda j, i: (i, 0)), pl.BlockSpec((ms, ka), lambda j, i: (0, 0)),
            pl.BlockSpec((tm, ks), lambda j, i: (i, 0)), pl.BlockSpec((ms, ks), lambda j, i: (0, 0)),
            pl.BlockSpec((None, ka, tn), lambda j, i: (layer, 0, j)),
            pl.BlockSpec((None, ks, tn), lambda j, i: (layer, 0, j)),
            pl.BlockSpec((None, ks, tn), lambda j, i: (layer, 0, j + nglu)),
            pl.BlockSpec((tm, tn), lambda j, i: (i, j + off_a)), pl.BlockSpec((tm, tn), lambda j, i: (i, j + off_b)),
            pl.BlockSpec((ms, tn), lambda j, i: (0, j + off_a)), pl.BlockSpec((ms, tn), lambda j, i: (0, j + off_b)),
        ],
        out_specs=[pl.BlockSpec((tm, tn), lambda j, i: (i, j)), pl.BlockSpec((ms, tn), lambda j, i: (0, j))],
        out_shape=[jax.ShapeDtypeStruct((m, d_model), _BF), jax.ShapeDtypeStruct((ms, d_model), _BF)],
        scratch_shapes=[pltpu.VMEM((ka, tn), _BF), pltpu.VMEM((ks, tn), _BF), pltpu.VMEM((ks, tn), _BF)],
        compiler_params=_params(("arbitrary", "arbitrary")),
        name="merge",
    )(oa, oas, y, ys, w_attn_out, w_glu, w_glu, zug, zug, zug_s, zug_s)


def _down_kernel(a_ref, as_ref, w_ref, res_ref, ress_ref, o_ref, os_ref):
    k = pl.program_id(2)
    wb = w_ref[...].astype(_BF)
    part = jnp.dot(a_ref[...], wb, preferred_element_type=_F32)

    @pl.when(k == 0)
    def _():
        o_ref[...] = res_ref[...] + part

    @pl.when(k > 0)
    def _():
        o_ref[...] += part

    part_s = jnp.dot(as_ref[...], wb, preferred_element_type=_F32)

    @pl.when(k == 0)
    def _():
        os_ref[...] = ress_ref[...] + part_s

    @pl.when(k > 0)
    def _():
        os_ref[...] += part_s


def _ffn_down(a, a_s, w2, layer, res, res_s, *, tm, tn, tk):
    m, f = a.shape
    ms = a_s.shape[0]
    n = w2.shape[2]
    ni = m // tm
    out, out_s = pl.pallas_call(
        _down_kernel,
        grid=(ni, n // tn, f // tk),
        in_specs=[
            pl.BlockSpec((tm, tk), lambda i, j, k: (i, k)), pl.BlockSpec((ms, tk), lambda i, j, k: (0, k)),
            pl.BlockSpec((None, tk, tn), lambda i, j, k: (layer, k, j)),
            pl.BlockSpec((tm, tn), lambda i, j, k: (i, j)), pl.BlockSpec((ms, tn), lambda i, j, k: (0, j)),
        ],
        out_specs=[pl.BlockSpec((tm, tn), lambda i, j, k: (i, j)),
                   pl.BlockSpec((None, ms, tn), lambda i, j, k: (i, 0, j))],
        out_shape=[jax.ShapeDtypeStruct((m, n), _F32), jax.ShapeDtypeStruct((ni, ms, n), _F32)],
        compiler_params=_params(("arbitrary", "arbitrary", "arbitrary")),
        name="ffn_down",
    )(a, a_s, w2, res, res_s)
    return out, out_s[0]


def _masked_softmax(s, mask):
    s = jnp.where(mask, s, -jnp.inf)
    m = jnp.max(s, axis=-1, keepdims=True)
    m = jnp.where(jnp.isfinite(m), m, 0.0)
    p = jnp.where(mask, jnp.exp(s - m), 0.0)
    den = jnp.maximum(jnp.sum(p, axis=-1, keepdims=True), 1e-30)
    return p * (1.0 / den)


def _softmax_exp2(s2):
    m = jnp.max(s2, axis=-1, keepdims=True)
    m = jnp.where(jnp.isfinite(m), m, 0.0)
    p = jnp.exp2(s2 - m)
    return p, 1.0 / jnp.maximum(jnp.sum(p, axis=-1, keepdims=True), 1e-30)


def _dot_t(a, b, precision=None):
    return lax.dot_general(a, b, (((1,), (1,)), ((), ())), precision=precision, preferred_element_type=_F32)


def _nsa_prompt_kernel(slope_ref, q_ref, kc_ref, vc_ref, ks_ref, vs_ref, kw_ref, vw_ref, gate_ref,
                       pk_ref, pv_ref, e_ref, o_ref, kcp, vcp, ksb, vsb, kwb, vwb, osel, *, tq, seq, nb, band):
    kh = pl.program_id(1)
    qi = pl.program_id(2)

    @pl.when(qi == 0)
    def _():
        kcp[...] = jnp.dot(pk_ref[...].astype(_BF), kc_ref[...].astype(_BF),
                           preferred_element_type=_F32).astype(_BF)
        vcp[...] = jnp.dot(pv_ref[...].astype(_BF), vc_ref[...].astype(_BF),
                           preferred_element_type=_F32).astype(_BF)
        ksb[...] = ks_ref[...].astype(_BF)
        vsb[...] = vs_ref[...].astype(_BF)
        kwb[...] = kw_ref[...].astype(_BF)
        vwb[...] = vw_ref[...].astype(_BF)

    t0 = qi * tq
    qpos = t0 + lax.broadcasted_iota(jnp.int32, (tq, 1), 0)
    qpos_f = qpos.astype(_F32)
    q = q_ref[...]
    gate = _sigmoid(gate_ref[...])

    blk = lax.broadcasted_iota(jnp.int32, (tq, nb), 1)
    center = (blk * BLOCK).astype(_F32) + 0.5 * (BLOCK - 1)
    valid_c = ((blk + 1) * BLOCK - 1) <= qpos
    dist_c = qpos_f - center
    kc = kcp[...]
    vcb = vcp[...]
    o_cmp = []
    imp = jnp.zeros((tq, nb), _F32)
    for g in range(GROUP):
        slope = slope_ref[kh * GROUP + g]
        qg = q[:, g * HEAD_DIM:(g + 1) * HEAD_DIM].astype(_BF)
        s = _dot_t(qg, kc) * SCALE - slope * dist_c
        p = _masked_softmax(s, valid_c)
        imp = imp + p
        o_cmp.append(jnp.dot(p.astype(_BF), vcb, preferred_element_type=_F32))

    cur = qpos // BLOCK
    forced = (blk == cur) | (blk == cur - 1) | (blk == 0)
    score = jnp.where(forced, FORCED_SCORE, jnp.where(blk > cur, -jnp.inf, imp))
    rank = jnp.zeros((tq, nb), _F32)
    for i in range(nb):
        col = score[:, i:i + 1]
        beats = (col > score) | ((col == score) & (blk > i))
        rank = rank + jnp.where(beats, 1.0, 0.0)
    sel = jnp.where(rank < float(min(N_SELECT, nb)), 1.0, 0.0).astype(_BF)
    sel_keys = jnp.dot(sel, e_ref[...], preferred_element_type=_F32) > 0.5

    kpos = lax.broadcasted_iota(jnp.int32, (1, seq), 1)
    dist_s = qpos - kpos
    neg_s = jnp.where(sel_keys & (dist_s >= 0), 0.0, -jnp.inf)
    dist_sf = dist_s.astype(_F32)

    ws = pl.multiple_of(jnp.maximum(t0 - WINDOW, 0), math.gcd(tq, WINDOW))
    kpos_w = ws + lax.broadcasted_iota(jnp.int32, (1, band), 1)
    dist_w = qpos - kpos_w
    neg_w = jnp.where((dist_w >= 0) & (dist_w < WINDOW), 0.0, -jnp.inf)
    dist_wf = dist_w.astype(_F32)
    kw_band = kwb[pl.ds(ws, band), :]
    vw_band = vwb[pl.ds(ws, band), :]

    nq = seq // tq
    n_var = 4 if nq % 4 == 0 else 1
    for v in range(n_var):
        klen = seq * (v + 1) // n_var

        @pl.when((qi * n_var) // nq == v)
        def _(klen=klen):
            for g in range(GROUP):
                c2 = slope_ref[kh * GROUP + g] * LOG2E
                qg = q[:, g * HEAD_DIM:(g + 1) * HEAD_DIM].astype(_BF)
                s2 = _dot_t(qg, ksb[:klen, :]) * (SCALE * LOG2E) + (neg_s[:, :klen] - c2 * dist_sf[:, :klen])
                p, inv = _softmax_exp2(s2)
                osel[:, g * HEAD_DIM:(g + 1) * HEAD_DIM] = inv * jnp.dot(p.astype(_BF), vsb[:klen, :],
                                                                         preferred_element_type=_F32)

    for g in range(GROUP):
        c2 = slope_ref[kh * GROUP + g] * LOG2E
        qg = q[:, g * HEAD_DIM:(g + 1) * HEAD_DIM].astype(_BF)
        o_sel = osel[:, g * HEAD_DIM:(g + 1) * HEAD_DIM]
        s2 = _dot_t(qg, kw_band) * (SCALE * LOG2E) + (neg_w - c2 * dist_wf)
        p, inv = _softmax_exp2(s2)
        o_win = inv * jnp.dot(p.astype(_BF), vw_band, preferred_element_type=_F32)
        o = (gate[:, g:g + 1] * o_cmp[g] + gate[:, GROUP + g:GROUP + g + 1] * o_sel
             + gate[:, 2 * GROUP + g:2 * GROUP + g + 1] * o_win)
        o_ref[:, g * HEAD_DIM:(g + 1) * HEAD_DIM] = o.astype(o_ref.dtype)


def _alibi_slopes():
    h = np.arange(1, N_HEADS + 1, dtype=np.float32)
    return jnp.asarray(np.exp2(-ALIBI_MAX_EXP * h / N_HEADS).astype(np.float32))


def _nsa_prompt(zqkv, gates_r, pool_k, pool_v, expand, batch, seq):
    nb = seq // BLOCK
    tq = _tile(seq, 256)
    nq = seq // tq
    band = min(WINDOW + tq, seq)
    d_attn = N_HEADS * HEAD_DIM
    kv0 = d_attn // HEAD_DIM

    def kv_spec(branch, which):
        base = kv0 + (branch * 2 + which) * N_KV_HEADS
        return pl.BlockSpec((seq, HEAD_DIM), lambda b, k, qi, sl: (b, base + k))

    grid_spec = pltpu.PrefetchScalarGridSpec(
        num_scalar_prefetch=1,
        grid=(batch, N_KV_HEADS, nq),
        in_specs=[
            pl.BlockSpec((tq, GROUP * HEAD_DIM), lambda b, k, qi, sl: (b * nq + qi, k)),
            kv_spec(0, 0), kv_spec(0, 1), kv_spec(1, 0), kv_spec(1, 1), kv_spec(2, 0), kv_spec(2, 1),
            pl.BlockSpec((None, tq, 3 * GROUP), lambda b, k, qi, sl: (k, b * nq + qi, 0)),
            pl.BlockSpec((None, nb, seq), lambda b, k, qi, sl: (k, 0, 0)),
            pl.BlockSpec((None, nb, seq), lambda b, k, qi, sl: (k, 0, 0)),
            pl.BlockSpec((nb, seq), lambda b, k, qi, sl: (0, 0)),
        ],
        out_specs=pl.BlockSpec((tq, GROUP * HEAD_DIM), lambda b, k, qi, sl: (b * nq + qi, k)),
        scratch_shapes=[pltpu.VMEM((nb, HEAD_DIM), _BF), pltpu.VMEM((nb, HEAD_DIM), _BF)]
        + [pltpu.VMEM((seq, HEAD_DIM), _BF) for _ in range(4)] + [pltpu.VMEM((tq, GROUP * HEAD_DIM), _F32)],
    )
    return pl.pallas_call(
        functools.partial(_nsa_prompt_kernel, tq=tq, seq=seq, nb=nb, band=band),
        grid_spec=grid_spec,
        out_shape=jax.ShapeDtypeStruct((batch * seq, d_attn), _BF),
        compiler_params=_params(("arbitrary", "arbitrary", "arbitrary")),
        name="nsa_prompt",
    )(_alibi_slopes(), zqkv, zqkv, zqkv, zqkv, zqkv, zqkv, zqkv, gates_r, pool_k, pool_v, expand)


def _pool_pages_kernel(pt_ref, *refs, pages):
    w_ref, o_ref = refs[pages], refs[pages + 1]
    w = w_ref[...].astype(_BF).astype(_F32)
    per = PAGE_SIZE // BLOCK
    for r in range(pages):
        x = refs[r][...].astype(_BF).astype(_F32)
        for n in range(per):
            o_ref[r * per + n] = jnp.sum(x[n * BLOCK:(n + 1) * BLOCK] * w, axis=0).astype(_BF).astype(_F32)


def _pool_pages(cache, page_table, w_exp, layer):
    db, n_pages = page_table.shape
    per = PAGE_SIZE // BLOCK
    row = cache.shape[3:]
    pages = math.gcd(n_pages, 8)

    def page_spec(r):
        return pl.BlockSpec((None, None, PAGE_SIZE) + row,
                            lambda b, p, pt: (layer, pt[b * n_pages + p * pages + r], 0, 0, 0))

    grid_spec = pltpu.PrefetchScalarGridSpec(
        num_scalar_prefetch=1,
        grid=(db, n_pages // pages),
        in_specs=[page_spec(r) for r in range(pages)] + [pl.BlockSpec((BLOCK,) + row, lambda b, p, pt: (0, 0, 0))],
        out_specs=pl.BlockSpec((None, pages * per) + row, lambda b, p, pt: (b, p, 0, 0)),
    )
    return pl.pallas_call(
        functools.partial(_pool_pages_kernel, pages=pages),
        grid_spec=grid_spec,
        out_shape=jax.ShapeDtypeStruct((db, n_pages * per) + row, _F32),
        compiler_params=_params(("arbitrary", "arbitrary")),
        name="pool_pages",
    )(page_table.reshape(-1), *([cache] * pages), w_exp)


def _cmp_select_kernel(slope_ref, q_ref, pool_ref, new_ref, w0_ref, oc_ref, idx_ref, *, past, n_sel):
    nbp = pool_ref.shape[0]
    qpos = past
    q = q_ref[...]
    pooled_new = new_ref[...] * w0_ref[...]
    blk = lax.broadcasted_iota(jnp.int32, (1, nbp), 1)
    center = (blk * BLOCK).astype(_F32) + 0.5 * (BLOCK - 1)
    valid = ((blk + 1) * BLOCK - 1) <= qpos
    dist = float(qpos) - center
    center_n = float(nbp * BLOCK) + 0.5 * (BLOCK - 1)
    valid_n = jnp.full((1, 1), (nbp + 1) * BLOCK - 1 <= qpos)
    dist_n = float(qpos) - center_n
    rows_i = lax.broadcasted_iota(jnp.int32, (nbp, nbp), 0)
    cols_i = lax.broadcasted_iota(jnp.int32, (nbp, nbp), 1)
    sel_r = lax.broadcasted_iota(jnp.int32, (n_sel, nbp), 0).astype(_F32)
    blk_f = blk.astype(_F32)
    cur = qpos // BLOCK
    for kh in range(N_KV_HEADS):
        kc = pool_ref[:, (2 * kh) * HEAD_DIM:(2 * kh + 1) * HEAD_DIM]
        vc = pool_ref[:, (2 * kh + 1) * HEAD_DIM:(2 * kh + 2) * HEAD_DIM]
        kn = pooled_new[:, (2 * kh) * HEAD_DIM:(2 * kh + 1) * HEAD_DIM]
        vn = pooled_new[:, (2 * kh + 1) * HEAD_DIM:(2 * kh + 2) * HEAD_DIM]
        qk = q[kh * GROUP:(kh + 1) * GROUP]
        slope = jnp.concatenate([jnp.full((1, 1), slope_ref[kh * GROUP + g], _F32) for g in range(GROUP)], axis=0)
        s = _dot_t(qk.astype(_BF), kc.astype(_BF)) * SCALE - slope * dist
        s_n = jnp.sum(qk * kn, axis=-1, keepdims=True) * SCALE - slope * dist_n
        s = jnp.where(valid, s, -jnp.inf)
        s_n = jnp.where(valid_n, s_n, -jnp.inf)
        m = jnp.maximum(jnp.max(s, axis=-1, keepdims=True), s_n)
        m = jnp.where(jnp.isfinite(m), m, 0.0)
        p = jnp.where(valid, jnp.exp(s - m), 0.0)
        p_n = jnp.where(valid_n, jnp.exp(s_n - m), 0.0)
        den = jnp.maximum(jnp.sum(p, axis=-1, keepdims=True) + p_n, 1e-30)
        inv = 1.0 / den
        p = p * inv
        p_n = p_n * inv
        oc_ref[kh * GROUP:(kh + 1) * GROUP, :] = (
            jnp.dot(p.astype(_BF), vc.astype(_BF), preferred_element_type=_F32) + p_n * vn)
        imp = jnp.sum(p, axis=0, keepdims=True)
        imp_n = jnp.sum(p_n, axis=0, keepdims=True)
        forced = (blk == cur) | (blk == cur - 1) | (blk == 0)
        score = jnp.where(forced, FORCED_SCORE, jnp.where(blk > cur, -jnp.inf, imp))
        forced_n = (nbp == cur) | (nbp == cur - 1) | (nbp == 0)
        score_n = jnp.where(forced_n, FORCED_SCORE, jnp.where(nbp > cur, -jnp.inf, imp_n))
        sj = jnp.broadcast_to(score, (nbp, nbp))
        si = jnp.transpose(sj)
        ahead = (si > sj) | ((si == sj) & (rows_i < cols_i))
        rank = jnp.sum(jnp.where(ahead, 1.0, 0.0), axis=0, keepdims=True) + jnp.where(score_n > score, 1.0, 0.0)
        rank_n = jnp.sum(jnp.where(score >= score_n, 1.0, 0.0), axis=-1, keepdims=True)
        onehot = jnp.where(rank == sel_r, 1.0, 0.0)
        sel_col = lax.broadcasted_iota(jnp.int32, (n_sel, 1), 0).astype(_F32)
        idx = jnp.sum(onehot * blk_f, axis=-1, keepdims=True) + jnp.where(rank_n == sel_col, float(nbp), 0.0)
        idx_ref[kh] = jnp.broadcast_to(idx, (n_sel, LANE)).astype(jnp.int32)


def _cmp_select(q_s, pooled, new_rows, w0, past, n_sel):
    db, nbp, width = pooled.shape
    grid_spec = pltpu.PrefetchScalarGridSpec(
        num_scalar_prefetch=1,
        grid=(db,),
        in_specs=[
            pl.BlockSpec((None, N_HEADS, HEAD_DIM), lambda b, sl: (b, 0, 0)),
            pl.BlockSpec((None, nbp, width), lambda b, sl: (b, 0, 0)),
            pl.BlockSpec((None, 1, width), lambda b, sl: (b, 0, 0)),
            pl.BlockSpec((1, width), lambda b, sl: (0, 0)),
        ],
        out_specs=[
            pl.BlockSpec((None, N_HEADS, HEAD_DIM), lambda b, sl: (b, 0, 0)),
            pl.BlockSpec((None, N_KV_HEADS, n_sel, LANE), lambda b, sl: (b, 0, 0, 0)),
        ],
    )
    return pl.pallas_call(
        functools.partial(_cmp_select_kernel, past=past, n_sel=n_sel),
        grid_spec=grid_spec,
        out_shape=[jax.ShapeDtypeStruct((db, N_HEADS, HEAD_DIM), _F32),
                   jax.ShapeDtypeStruct((db, N_KV_HEADS, n_sel, LANE), jnp.int32)],
        compiler_params=_params(("arbitrary",)),
        name="cmp_select",
    )(_alibi_slopes(), q_s, pooled, new_rows, w0)


def _sel_sample_kernel(idx_ref, pt_ref, slope_ref, q_ref, *refs, past, n_sel):
    blk_refs, new_ref, o_ref, rows_ref = refs[:n_sel], refs[n_sel], refs[n_sel + 1], refs[n_sel + 2]
    b = pl.program_id(0)
    kh = pl.program_id(1)
    base = (b * N_KV_HEADS + kh) * n_sel
    rw = 2 * N_KV_HEADS
    row_i = lax.broadcasted_iota(jnp.int32, (BLOCK, 1, 1), 0)
    for jj in range(n_sel):
        kpos_c = idx_ref[base + jj] * BLOCK + row_i
        rows_ref[jj * BLOCK:(jj + 1) * BLOCK] = jnp.where(kpos_c < past, blk_refs[jj][...], new_ref[...])
    qpos = past
    n_rows = n_sel * BLOCK * rw
    rows = rows_ref[...].reshape(n_rows, HEAD_DIM).astype(_BF)
    col = lax.broadcasted_iota(jnp.int32, (1, n_rows), 1)
    pos = col // rw
    blk_of = jnp.zeros((1, n_rows), jnp.int32)
    for jj in range(n_sel):
        blk_of = jnp.where(pos // BLOCK == jj, idx_ref[base + jj], blk_of)
    dist = qpos - (blk_of * BLOCK + pos % BLOCK)
    mask = (dist >= 0) & (col % rw == 2 * kh)
    slope = jnp.concatenate([jnp.full((1, 1), slope_ref[kh * GROUP + g], _F32) for g in range(GROUP)], axis=0)
    s = _dot_t(q_ref[...].astype(_BF), rows) * SCALE - slope * dist.astype(_F32)
    p = _masked_softmax(s, mask)
    p_val = pltpu.roll(p, 1, 1)
    o_ref[...] = jnp.dot(p_val.astype(_BF), rows, preferred_element_type=_F32)


def _sel_sample(q_s, cache, idx, page_table, new_rows, layer, past, n_sel):
    db, n_pages = page_table.shape
    rw = 2 * N_KV_HEADS

    def blk_spec(j):
        def blk_map(b, k, idx_ref, pt_ref, sl):
            blk = idx_ref[(b * N_KV_HEADS + k) * n_sel + j]
            pos = jnp.minimum(blk * BLOCK, past - 1)
            return (layer, pt_ref[b * n_pages + pos // PAGE_SIZE], (pos % PAGE_SIZE) // BLOCK, 0, 0)
        return pl.BlockSpec((None, None, BLOCK, rw, HEAD_DIM), blk_map)

    grid_spec = pltpu.PrefetchScalarGridSpec(
        num_scalar_prefetch=3,
        grid=(db, N_KV_HEADS),
        in_specs=[pl.BlockSpec((None, GROUP, HEAD_DIM), lambda b, k, *_: (b * N_KV_HEADS + k, 0, 0))]
        + [blk_spec(j) for j in range(n_sel)]
        + [pl.BlockSpec((None, 1, rw, HEAD_DIM), lambda b, k, *_: (b, 0, 0, 0))],
        out_specs=pl.BlockSpec((None, GROUP, HEAD_DIM), lambda b, k, *_: (b * N_KV_HEADS + k, 0, 0)),
        scratch_shapes=[pltpu.VMEM((n_sel * BLOCK, rw, HEAD_DIM), _F32)],
    )
    return pl.pallas_call(
        functools.partial(_sel_sample_kernel, past=past, n_sel=n_sel),
        grid_spec=grid_spec,
        out_shape=jax.ShapeDtypeStruct((db * N_KV_HEADS, GROUP, HEAD_DIM), _F32),
        compiler_params=_params(("arbitrary", "arbitrary")),
        name="sel_sample",
    )(idx, page_table.reshape(-1), _alibi_slopes(), q_s, *([cache] * n_sel), new_rows)


def _win_combine_kernel(slope_ref, q_ref, win_ref, new_ref, oc_ref, os_ref, gate_ref, o_ref, *, past, w_buf):
    qpos = past
    q = q_ref[...]
    gate = _sigmoid(gate_ref[...])
    kpos = past - w_buf + lax.broadcasted_iota(jnp.int32, (1, w_buf), 1)
    dist = qpos - kpos
    mask = (dist >= 0) & (dist < WINDOW) & (kpos >= 0)
    dist_f = dist.astype(_F32)
    new = new_ref[...]
    mask_n = jnp.full((1, 1), 0 < WINDOW)
    for kh in range(N_KV_HEADS):
        kw = win_ref[:, (2 * kh) * HEAD_DIM:(2 * kh + 1) * HEAD_DIM].astype(_BF)
        vw = win_ref[:, (2 * kh + 1) * HEAD_DIM:(2 * kh + 2) * HEAD_DIM].astype(_BF)
        kn = new[:, (2 * kh) * HEAD_DIM:(2 * kh + 1) * HEAD_DIM]
        vn = new[:, (2 * kh + 1) * HEAD_DIM:(2 * kh + 2) * HEAD_DIM]
        hs = slice(kh * GROUP, (kh + 1) * GROUP)
        qk = q[hs]
        slope = jnp.concatenate([jnp.full((1, 1), slope_ref[kh * GROUP + g], _F32) for g in range(GROUP)], axis=0)
        s = _dot_t(qk.astype(_BF), kw) * SCALE - slope * dist_f
        qb = qk.astype(_BF).astype(_F32)
        s_n = jnp.sum(qb * kn.astype(_BF).astype(_F32), axis=-1, keepdims=True) * SCALE
        s = jnp.where(mask, s, -jnp.inf)
        s_n = jnp.where(mask_n, s_n, -jnp.inf)
        m = jnp.maximum(jnp.max(s, axis=-1, keepdims=True), s_n)
        m = jnp.where(jnp.isfinite(m), m, 0.0)
        p = jnp.where(mask, jnp.exp(s - m), 0.0)
        p_n = jnp.where(mask_n, jnp.exp(s_n - m), 0.0)
        inv = 1.0 / jnp.maximum(jnp.sum(p, axis=-1, keepdims=True) + p_n, 1e-30)
        o_w = (jnp.dot((p * inv).astype(_BF), vw, preferred_element_type=_F32)
               + (p_n * inv).astype(_BF).astype(_F32) * vn.astype(_BF).astype(_F32))
        g3 = gate[hs]
        o_ref[hs, :] = g3[:, 0:1] * oc_ref[hs, :] + g3[:, 1:2] * os_ref[hs, :] + g3[:, 2:3] * o_w


def _win_combine(q_s, win, new_rows, o_cmp, o_sel, gates, past):
    db, w_buf, width = win.shape
    head = pl.BlockSpec((None, N_HEADS, HEAD_DIM), lambda b, sl: (b, 0, 0))
    grid_spec = pltpu.PrefetchScalarGridSpec(
        num_scalar_prefetch=1,
        grid=(db,),
        in_specs=[
            head,
            pl.BlockSpec((None, w_buf, width), lambda b, sl: (b, 0, 0)),
            pl.BlockSpec((None, 1, width), lambda b, sl: (b, 0, 0)),
            head, head,
            pl.BlockSpec((None, N_HEADS, 3), lambda b, sl: (b, 0, 0)),
        ],
        out_specs=head,
    )
    return pl.pallas_call(
        functools.partial(_win_combine_kernel, past=past, w_buf=w_buf),
        grid_spec=grid_spec,
        out_shape=jax.ShapeDtypeStruct((db, N_HEADS, HEAD_DIM), _F32),
        compiler_params=_params(("arbitrary",)),
        name="win_combine",
    )(_alibi_slopes(), q_s, win, new_rows, o_cmp, o_sel, gates)


def _ssm_discretise(a_re, a_im, log_dt, b_re, b_im):
    dt = jnp.exp(log_dt)[:, None]
    mag = jnp.exp(dt * a_re)
    abar_re = mag * jnp.cos(dt * a_im)
    abar_im = mag * jnp.sin(dt * a_im)
    den = a_re * a_re + a_im * a_im
    cf_re = ((abar_re - 1.0) * a_re + abar_im * a_im) / den
    cf_im = (abar_im * a_re - (abar_re - 1.0) * a_im) / den
    bb_re = cf_re[..., None] * b_re - cf_im[..., None] * b_im
    bb_im = cf_re[..., None] * b_im + cf_im[..., None] * b_re
    return abar_re, abar_im, bb_re, bb_im


def _ssm_chunk_matrices(abar_re, abar_im, bb_re, bb_im, c_re, c_im):
    L = SSM_CHUNK
    g, p, ch = bb_re.shape
    nq = LANE // ch
    nblk = g // nq
    pw_re = [jnp.ones_like(abar_re)]
    pw_im = [jnp.zeros_like(abar_im)]
    for _ in range(L):
        r, i = pw_re[-1], pw_im[-1]
        pw_re.append(r * abar_re - i * abar_im)
        pw_im.append(r * abar_im + i * abar_re)
    pw_re = jnp.stack(pw_re, axis=1)
    pw_im = jnp.stack(pw_im, axis=1)
    bt_re = bb_re.transpose(0, 2, 1)[:, None]
    bt_im = bb_im.transpose(0, 2, 1)[:, None]
    e_re = pw_re[:, :L, None, :] * bt_re - pw_im[:, :L, None, :] * bt_im
    e_im = pw_re[:, :L, None, :] * bt_im + pw_im[:, :L, None, :] * bt_re
    k_lag = (jnp.einsum("gcp,gldp->gldc", c_re, e_re, precision=_HI)
             - jnp.einsum("gcp,gldp->gldc", c_im, e_im, precision=_HI))

    def lane_diag(x):
        w = x.shape[-1]
        flat = x.transpose(0, 2, 1, 3, 4).reshape(nblk, L * LANE, w)
        mask = np.broadcast_to(np.eye(nq, dtype=np.float32)[None, :, None, :, None], (L, nq, ch, nq, w))
        return jnp.tile(flat, (1, 1, nq)) * jnp.asarray(mask.reshape(L * LANE, nq * w))

    bd = lane_diag(k_lag.reshape(nblk, nq, L, ch, ch)).reshape(nblk, L, LANE, LANE)

    def state_in(e):
        return lane_diag(e[:, ::-1].reshape(nblk, nq, L, ch, p))

    pr = pw_re[:, 1:, None, :]
    pi = pw_im[:, 1:, None, :]

    def state_out_t(v):
        return lane_diag(v.reshape(nblk, nq, L, ch, p))

    vt_re = state_out_t(c_re[:, None] * pr - c_im[:, None] * pi)
    vt_im = state_out_t(-(c_re[:, None] * pi + c_im[:, None] * pr))
    al_re = pw_re[:, L].reshape(nblk, 1, nq * p)
    al_im = pw_im[:, L].reshape(nblk, 1, nq * p)
    return (bd.astype(_BF), state_in(e_re).astype(_BF), state_in(e_im).astype(_BF),
            vt_re.astype(_BF), vt_im.astype(_BF), al_re, al_im)


def _ssm_prompt_kernel(u_ref, bd_ref, wre_ref, wim_ref, vtre_ref, vtim_ref, alre_ref, alim_ref, d_ref,
                       y_ref, hre_ref, him_ref, msc, xs, ys, loc_re, loc_im, hp_re, hp_im, *, n_chunk):
    L = SSM_CHUNK

    @pl.when(pl.program_id(1) == 0)
    def _():
        for j in range(L):
            for t in range(L):
                blk = bd_ref[t - j] if t >= j else jnp.zeros((LANE, LANE), _BF)
                msc[j * LANE:(j + 1) * LANE, t * LANE:(t + 1) * LANE] = blk

    for j in range(L):
        xs[:, j * LANE:(j + 1) * LANE] = u_ref[pl.ds(j, n_chunk, stride=L), :].astype(_BF)
    xb = xs[...]
    y = jnp.dot(xb, msc[...], preferred_element_type=_F32)
    loc_re[...] = jnp.dot(xb, wre_ref[...], preferred_element_type=_F32)
    loc_im[...] = jnp.dot(xb, wim_ref[...], preferred_element_type=_F32)
    a_re = alre_ref[...]
    a_im = alim_ref[...]

    def step(c, carry):
        h_re, h_im = carry
        hp_re[pl.ds(c, 1), :] = h_re
        hp_im[pl.ds(c, 1), :] = h_im
        l_re = loc_re[pl.ds(c, 1), :]
        l_im = loc_im[pl.ds(c, 1), :]
        return (a_re * h_re - a_im * h_im + l_re, a_re * h_im + a_im * h_re + l_im)

    zero = jnp.zeros(a_re.shape, _F32)
    h_re, h_im = lax.fori_loop(0, n_chunk, step, (zero, zero))
    y = y + _dot_t(hp_re[...].astype(_BF), vtre_ref[...])
    y = y + _dot_t(hp_im[...].astype(_BF), vtim_ref[...])
    d = d_ref[...]
    for t in range(L):
        x_t = u_ref[pl.ds(t, n_chunk, stride=L), :]
        ys[pl.ds(t, n_chunk, stride=L), :] = y[:, t * LANE:(t + 1) * LANE] + d * x_t
    y_ref[...] = ys[...].astype(y_ref.dtype)
    hre_ref[...] = h_re
    him_ref[...] = h_im


def _ssm_prompt(zug, mats, d_skip, batch, seq, d_ssm):
    bd, wre, wim, vtre, vtim, al_re, al_im = mats
    L = SSM_CHUNK
    nblk = bd.shape[0]
    nq = LANE // GROUP_CH
    n_chunk = seq // L
    sw = nq * P_STATE
    per_blk = lambda shape: pl.BlockSpec((None,) + shape, lambda g, b: (g, 0, 0))
    fin_spec = pl.BlockSpec((None, None, 1, sw), lambda g, b: (g, b, 0, 0))
    y, h_re, h_im = pl.pallas_call(
        functools.partial(_ssm_prompt_kernel, n_chunk=n_chunk),
        grid=(nblk, batch),
        in_specs=[pl.BlockSpec((seq, LANE), lambda g, b: (b, g)),
                  pl.BlockSpec((None, L, LANE, LANE), lambda g, b: (g, 0, 0, 0)),
                  per_blk((L * LANE, sw)), per_blk((L * LANE, sw)), per_blk((L * LANE, sw)), per_blk((L * LANE, sw)),
                  per_blk((1, sw)), per_blk((1, sw)), per_blk((1, LANE))],
        out_specs=[pl.BlockSpec((seq, LANE), lambda g, b: (b, g)), fin_spec, fin_spec],
        out_shape=[jax.ShapeDtypeStruct((batch * seq, d_ssm), _BF),
                   jax.ShapeDtypeStruct((nblk, batch, 1, sw), _F32),
                   jax.ShapeDtypeStruct((nblk, batch, 1, sw), _F32)],
        scratch_shapes=[pltpu.VMEM((L * LANE, L * LANE), _BF), pltpu.VMEM((n_chunk, L * LANE), _BF),
                        pltpu.VMEM((seq, LANE), _F32)]
        + [pltpu.VMEM((n_chunk, sw), _F32) for _ in range(4)],
        compiler_params=_params(("arbitrary", "arbitrary")),
        name="ssm_prompt",
    )(zug, bd, wre, wim, vtre, vtim, al_re, al_im, d_skip.reshape(nblk, 1, LANE))
    fin = lambda h: h.reshape(nblk, batch, nq, P_STATE).transpose(1, 0, 2, 3).reshape(batch, nblk * nq, P_STATE)
    return y, jnp.stack([fin(h_re), fin(h_im)], axis=-1)


def _ssm_sample_kernel(xe_ref, bre_ref, bim_ref, h0r_ref, h0i_ref, ar_ref, ai_ref, cre_ref, cim_ref, seg_ref,
                       xt_ref, d_ref, y_ref, sre_ref, sim_ref):
    ch = xe_ref.shape[0]
    h0r, h0i = h0r_ref[...], h0i_ref[...]
    a_re, a_im = ar_ref[...], ai_ref[...]
    rnd = lambda v: v.astype(_BF).astype(_F32)
    bu_re = rnd(xe_ref[0]) * rnd(bre_ref[0])
    bu_im = rnd(xe_ref[0]) * rnd(bim_ref[0])
    for c in range(1, ch):
        bu_re = bu_re + rnd(xe_ref[c]) * rnd(bre_ref[c])
        bu_im = bu_im + rnd(xe_ref[c]) * rnd(bim_ref[c])
    s_re = bu_re + (a_re * h0r - a_im * h0i)
    s_im = bu_im + (a_re * h0i + a_im * h0r)
    sre_ref[...] = s_re
    sim_ref[...] = s_im
    seg = seg_ref[...]
    sb_re, sb_im = rnd(s_re), rnd(s_im)
    for c in range(ch):
        y_re = jnp.dot(sb_re * rnd(cre_ref[c]), seg, precision=_HI, preferred_element_type=_F32)
        y_im = jnp.dot(sb_im * rnd(cim_ref[c]), seg, precision=_HI, preferred_element_type=_F32)
        y_ref[c] = y_re - y_im + d_ref[c] * xt_ref[c]


def _ssm_sample(u_s, h0, abar_re, abar_im, bb_re, bb_im, c_re, c_im, d_skip):
    db = u_s.shape[0]
    g, p, ch = bb_re.shape
    gp = g * p
    x3 = u_s.reshape(db, g, ch)
    xe = jnp.broadcast_to(x3.transpose(2, 0, 1)[..., None], (ch, db, g, p)).reshape(ch, db, gp)
    flat_b = lambda b: b.transpose(2, 0, 1).reshape(ch, 1, gp)
    flat_c = lambda c: c.transpose(1, 0, 2).reshape(ch, 1, gp)
    seg = jnp.asarray(np.repeat(np.eye(g, dtype=np.float32), p, axis=0))
    xt = x3.transpose(2, 0, 1)
    dt = d_skip.transpose(1, 0).reshape(ch, 1, g)
    y, s_re, s_im = pl.pallas_call(
        _ssm_sample_kernel,
        out_shape=[jax.ShapeDtypeStruct((ch, db, g), _F32), jax.ShapeDtypeStruct((db, gp), _F32),
                   jax.ShapeDtypeStruct((db, gp), _F32)],
        compiler_params=pltpu.CompilerParams(vmem_limit_bytes=VMEM_LIMIT),
        name="ssm_sample",
    )(xe, flat_b(bb_re), flat_b(bb_im), h0[..., 0].reshape(db, gp), h0[..., 1].reshape(db, gp),
      abar_re.reshape(1, gp), abar_im.reshape(1, gp), flat_c(c_re), flat_c(c_im), seg, xt, dt)
    y = y.transpose(1, 2, 0).reshape(db, g * ch)
    h_fin = jnp.stack([s_re.reshape(db, g, p), s_im.reshape(db, g, p)], axis=-1)
    return y, h_fin


def _router_kernel(x_ref, g_ref, wr_ref, idx_ref, gate_ref):
    x = x_ref[...]
    h = x * lax.rsqrt(jnp.mean(x * x, axis=-1, keepdims=True) + EPS) * g_ref[...]
    logits = jnp.dot(h.astype(_BF), wr_ref[...].astype(_BF), preferred_element_type=_F32)
    lane = lax.broadcasted_iota(jnp.int32, logits.shape, 1)
    lane_f = lane.astype(_F32)
    lg = jnp.where(lane < N_EXPERTS, logits, -jnp.inf)
    v1 = jnp.max(lg, axis=-1, keepdims=True)
    i1 = jnp.min(jnp.where(lg == v1, lane_f, float(LANE)), axis=-1, keepdims=True)
    lg2 = jnp.where(lane_f == i1, -jnp.inf, lg)
    v2 = jnp.max(lg2, axis=-1, keepdims=True)
    i2 = jnp.min(jnp.where(lg2 == v2, lane_f, float(LANE)), axis=-1, keepdims=True)
    e = jnp.exp(v2 - v1)
    den = 1.0 + e
    idx_ref[...] = jnp.where(lane == 0, i1, jnp.where(lane == 1, i2, 0.0)).astype(jnp.int32)
    gate_ref[...] = jnp.where(lane == 0, 1.0 / den, jnp.where(lane == 1, e / den, 0.0))


def _router(x, g, wr_pad):
    n, d = x.shape
    tm = _tile(n, 256)
    return pl.pallas_call(
        _router_kernel,
        grid=(n // tm,),
        in_specs=[pl.BlockSpec((tm, d), lambda i: (i, 0)), pl.BlockSpec((1, d), lambda i: (0, 0)),
                  pl.BlockSpec((d, LANE), lambda i: (0, 0))],
        out_specs=[pl.BlockSpec((tm, LANE), lambda i: (i, 0)), pl.BlockSpec((tm, LANE), lambda i: (i, 0))],
        out_shape=[jax.ShapeDtypeStruct((n, LANE), jnp.int32), jax.ShapeDtypeStruct((n, LANE), _F32)],
        compiler_params=_params(("arbitrary",)),
        name="router",
    )(x, g.reshape(1, d), wr_pad)


def _row_copy(src_hbm, row, dst, slot, sem):
    return pltpu.make_async_copy(src_hbm.at[pl.ds(row, 1)], dst.at[pl.ds(slot, 1)], sem)


def _moe_gather_kernel(tok_ref, x_hbm, g_ref, o_ref, buf, sems, *, tm, n_tiles):
    i = pl.program_id(0)
    slot = lax.rem(i, 2)

    def fetch(tile, s):
        def start(r, c):
            _row_copy(x_hbm, tok_ref[tile * tm + r], buf.at[s], r, sems.at[s]).start()
            return c

        lax.fori_loop(0, tm, start, 0, unroll=8)

    @pl.when(i == 0)
    def _():
        fetch(0, 0)

    @pl.when(i + 1 < n_tiles)
    def _():
        fetch(i + 1, 1 - slot)

    def wait(r, c):
        _row_copy(x_hbm, 0, buf.at[slot], r, sems.at[slot]).wait()
        return c

    lax.fori_loop(0, tm, wait, 0, unroll=8)
    x = buf[slot]
    inv = lax.rsqrt(jnp.mean(x * x, axis=-1, keepdims=True) + EPS)
    o_ref[...] = (x * inv * g_ref[...]).astype(o_ref.dtype)


def _moe_gather(x_all, g, row_tok, tm):
    n_rows = row_tok.shape[0]
    d = x_all.shape[1]
    n_tiles = n_rows // tm
    grid_spec = pltpu.PrefetchScalarGridSpec(
        num_scalar_prefetch=1,
        grid=(n_tiles,),
        in_specs=[pl.BlockSpec(memory_space=pl.ANY), pl.BlockSpec((1, d), lambda i, tok: (0, 0))],
        out_specs=pl.BlockSpec((tm, d), lambda i, tok: (i, 0)),
        scratch_shapes=[pltpu.VMEM((2, tm, d), _F32), pltpu.SemaphoreType.DMA((2,))],
    )
    return pl.pallas_call(
        functools.partial(_moe_gather_kernel, tm=tm, n_tiles=n_tiles),
        grid_spec=grid_spec,
        out_shape=jax.ShapeDtypeStruct((n_rows, d), _BF),
        compiler_params=_params(("arbitrary",)),
        name="moe_gather",
    )(row_tok, x_all, g.reshape(1, d))


def _moe_up_kernel(texp_ref, nused_ref, x_ref, w1_ref, w3_ref, o_ref, w1b, w3b):
    i = pl.program_id(1)
    n_used = nused_ref[0]
    prev = texp_ref[jnp.maximum(i - 1, 0)]
    fresh = (i == 0) | (texp_ref[i] != prev)

    @pl.when((i < n_used) & fresh)
    def _():
        w1b[...] = w1_ref[...].astype(_BF)
        w3b[...] = w3_ref[...].astype(_BF)

    @pl.when(i < n_used)
    def _():
        x = x_ref[...]
        a = jnp.dot(x, w1b[...], preferred_element_type=_F32)
        b = jnp.dot(x, w3b[...], preferred_element_type=_F32)
        o_ref[...] = ((a * _sigmoid(a)) * b).astype(o_ref.dtype)

    @pl.when(i >= n_used)
    def _():
        o_ref[...] = jnp.zeros(o_ref.shape, o_ref.dtype)


def _moe_up(xg, w1, w3, layer, tile_exp, n_used, *, tm, tf):
    n_rows, d = xg.shape
    f = w1.shape[3]
    n_tiles = n_rows // tm

    def live(i, nu):
        return jnp.minimum(i, nu[0] - 1)

    grid_spec = pltpu.PrefetchScalarGridSpec(
        num_scalar_prefetch=2,
        grid=(f // tf, n_tiles),
        in_specs=[
            pl.BlockSpec((tm, d), lambda j, i, te, nu: (live(i, nu), 0)),
            pl.BlockSpec((None, None, d, tf), lambda j, i, te, nu: (layer, te[live(i, nu)], 0, j)),
            pl.BlockSpec((None, None, d, tf), lambda j, i, te, nu: (layer, te[live(i, nu)], 0, j)),
        ],
        out_specs=pl.BlockSpec((tm, tf), lambda j, i, te, nu: (i, j)),
        scratch_shapes=[pltpu.VMEM((d, tf), _BF), pltpu.VMEM((d, tf), _BF)],
    )
    return pl.pallas_call(
        _moe_up_kernel,
        grid_spec=grid_spec,
        out_shape=jax.ShapeDtypeStruct((n_rows, f), _BF),
        compiler_params=_params(("arbitrary", "arbitrary")),
        name="moe_up",
    )(tile_exp, n_used, xg, w1, w3)


def _moe_down_kernel(texp_ref, nused_ref, a_ref, w_ref, o_ref):
    i = pl.program_id(0)
    k = pl.program_id(2)

    @pl.when(i < nused_ref[0])
    def _():
        part = jnp.dot(a_ref[...], w_ref[...].astype(_BF), preferred_element_type=_F32)

        @pl.when(k == 0)
        def _():
            o_ref[...] = part

        @pl.when(k > 0)
        def _():
            o_ref[...] += part

    @pl.when((i >= nused_ref[0]) & (k == 0))
    def _():
        o_ref[...] = jnp.zeros(o_ref.shape, o_ref.dtype)


def _moe_down(a, w2, layer, tile_exp, n_used, *, tm, tn, tk):
    n_rows, f = a.shape
    d = w2.shape[3]
    n_tiles = n_rows // tm
    nn, nk = d // tn, f // tk

    def live(i, nu):
        return jnp.minimum(i, nu[0] - 1)

    def last(i, x, n_x, nu):
        return jnp.where(i < nu[0], x, n_x - 1)

    grid_spec = pltpu.PrefetchScalarGridSpec(
        num_scalar_prefetch=2,
        grid=(n_tiles, nn, nk),
        in_specs=[
            pl.BlockSpec((tm, tk), lambda i, j, k, te, nu: (live(i, nu), last(i, k, nk, nu))),
            pl.BlockSpec((None, None, tk, tn),
                         lambda i, j, k, te, nu: (layer, te[live(i, nu)], last(i, k, nk, nu), last(i, j, nn, nu))),
        ],
        out_specs=pl.BlockSpec((tm, tn), lambda i, j, k, te, nu: (i, j)),
    )
    return pl.pallas_call(
        _moe_down_kernel,
        grid_spec=grid_spec,
        out_shape=jax.ShapeDtypeStruct((n_rows, d), _F32),
        compiler_params=_params(("arbitrary", "arbitrary", "arbitrary")),
        name="moe_down",
    )(tile_exp, n_used, a, w2)


def _moe_combine_kernel(pos_ref, x_ref, gate_ref, y_hbm, o_ref, b0, b1, sems, *, tm):
    def start(r, c):
        _row_copy(y_hbm, pos_ref[0, 0, 2 * r], b0, r, sems.at[0]).start()
        _row_copy(y_hbm, pos_ref[0, 0, 2 * r + 1], b1, r, sems.at[1]).start()
        return c

    lax.fori_loop(0, tm, start, 0, unroll=8)

    def wait(r, c):
        _row_copy(y_hbm, 0, b0, r, sems.at[0]).wait()
        _row_copy(y_hbm, 0, b1, r, sems.at[1]).wait()
        return c

    lax.fori_loop(0, tm, wait, 0, unroll=8)
    gate = gate_ref[...]
    o_ref[...] = x_ref[...] + (gate[:, 0:1] * b0[...] + gate[:, 1:2] * b1[...])


def _moe_combine(x, gates, pos, y_rows):
    n, d = x.shape
    tm = _tile(n, 256)
    n_tiles = n // tm
    return pl.pallas_call(
        functools.partial(_moe_combine_kernel, tm=tm),
        grid=(n_tiles,),
        in_specs=[pl.BlockSpec((1, 1, 2 * tm), lambda i: (i, 0, 0), memory_space=pltpu.SMEM),
                  pl.BlockSpec((tm, d), lambda i: (i, 0)),
                  pl.BlockSpec((tm, LANE), lambda i: (i, 0)),
                  pl.BlockSpec(memory_space=pl.ANY)],
        out_specs=pl.BlockSpec((tm, d), lambda i: (i, 0)),
        out_shape=jax.ShapeDtypeStruct((n, d), _F32),
        scratch_shapes=[pltpu.VMEM((tm, d), _F32), pltpu.VMEM((tm, d), _F32), pltpu.SemaphoreType.DMA((2,))],
        compiler_params=_params(("arbitrary",)),
        name="moe_combine",
    )(pos.reshape(n_tiles, 1, 2 * tm), x, gates, y_rows)


def _moe_layer(xp, xs, g, w_router, w1, w3, w2, layer):
    n_p, d = xp.shape
    n_s = xs.shape[0]
    n = n_p + n_s
    wr_pad = jnp.pad(w_router, ((0, 0), (0, LANE - N_EXPERTS)))
    idx_p, gate_p = _router(xp, g, wr_pad)
    idx_s, gate_s = _router(xs, g, wr_pad)
    top_i = jnp.concatenate([idx_p[:, :TOP_K], idx_s[:, :TOP_K]], axis=0)
    n_asg = n * TOP_K
    tm = 512 if n_asg >= 4096 else 64
    n_tiles = -(-n_asg // tm) + N_EXPERTS
    e_flat = top_i.reshape(n_asg)
    onehot = (e_flat[:, None] == jnp.arange(N_EXPERTS)[None, :]).astype(jnp.int32)
    within = jnp.sum((jnp.cumsum(onehot, axis=0) - onehot) * onehot, axis=1)
    counts = jnp.sum(onehot, axis=0)
    padded = (counts + tm - 1) // tm * tm
    pad_end = jnp.cumsum(padded)
    dest = (pad_end - padded)[e_flat] + within
    row_tok = jnp.zeros((n_tiles * tm,), jnp.int32).at[dest].set(jnp.arange(n_asg, dtype=jnp.int32) // TOP_K)
    tile_exp = jnp.minimum(jnp.searchsorted(pad_end, jnp.arange(n_tiles) * tm, side="right"),
                           N_EXPERTS - 1).astype(jnp.int32)
    n_used = (pad_end[-1:] // tm).astype(jnp.int32)
    pos = dest.reshape(n, TOP_K).astype(jnp.int32)

    x_all = jnp.concatenate([xp, xs], axis=0)
    xg = _moe_gather(x_all, g, row_tok, tm)
    f = w1.shape[3]
    a = _moe_up(xg, w1, w3, layer, tile_exp, n_used, tm=tm, tf=_tile(f, 512, LANE))
    y_rows = _moe_down(a, w2, layer, tile_exp, n_used, tm=tm, tn=d, tk=_tile(f, 512, LANE))
    out_p = _moe_combine(xp, gate_p, pos[:n_p], y_rows)
    out_s = _moe_combine(xs, gate_s, pos[n_p:], y_rows)
    return out_p, out_s


def kernel(x_prompt, x_sample, cache_kv_cmp, cache_kv_sel, cache_kv_win, state_ssm, page_table, norm_mix, norm_ffn, w_in, g_q, g_k, w_cmp, w_attn_out, ssm_a_re, ssm_a_im, ssm_log_dt, ssm_b_re, ssm_b_im, ssm_c_re, ssm_c_im, ssm_d, w_glu, w_out, ffn_w1, ffn_w3, ffn_w2, moe_router, moe_w1, moe_w3, moe_w2):
    batch, seq, d_model = x_prompt.shape
    db, dec_seq, _ = x_sample.shape
    assert dec_seq == 1, "the sample group advances one token per step"
    depth = w_in.shape[0]
    n_pool = cache_kv_cmp.shape[1]
    n_pages = page_table.shape[1]
    past = n_pages * PAGE_SIZE
    d_attn = N_HEADS * HEAD_DIM
    d_kv = N_KV_HEADS * HEAD_DIM
    d_ssm = ssm_d.shape[1] * GROUP_CH
    n_gate = 3 * N_HEADS
    off_gn = d_attn + 6 * d_kv
    off_u = off_gn + n_gate
    n_ug = d_ssm + 2 * d_model
    row_w = N_KV_HEADS * 2 * HEAD_DIM
    nb = seq // BLOCK
    n_p = batch * seq

    tm = _tile(n_p, 1024)
    tn = 512
    assert off_gn % tn == 0 and n_ug % tn == 0 and d_ssm % tn == 0 and d_model % tn == 0

    xp = x_prompt.reshape(n_p, d_model)
    xs = x_sample.reshape(db, d_model)
    w_ug = w_in[:, :, off_u:]
    flag_cols = np.zeros((1, off_gn), np.float32)
    flag_cols[:, :d_attn] = 1.0
    for br in range(3):
        lo = d_attn + br * 2 * d_kv
        flag_cols[:, lo:lo + d_kv] = 1.0
    flag_cols = jnp.asarray(flag_cols)
    expand = jnp.asarray(np.repeat(np.eye(nb, dtype=np.float32), BLOCK, axis=1)).astype(_BF)
    eye_nb = jnp.asarray(np.eye(nb, dtype=np.float32))
    n_sel = min(N_SELECT, past // BLOCK + 1)

    outs = {k: [] for k in ("cmp_p", "cmp_s", "sel_p", "sel_s", "win_p", "win_s", "ssm_p", "ssm_s")}
    for l in range(depth):
        hp = _rmsnorm(xp, norm_mix[l])
        hs = _rmsnorm(xs, norm_mix[l])
        gain = jnp.concatenate([jnp.tile(g_q[l], N_HEADS)]
                               + [jnp.concatenate([jnp.tile(g_k[l, br], N_KV_HEADS), jnp.ones((d_kv,), _F32)])
                                  for br in range(3)]).reshape(1, off_gn)
        zqkv, zqkv_s = _mm(hp, hs, [(w_in, l, 0)], off_gn, mode="headnorm", out_dtype=_F32, tm=tm, tn=tn,
                           gain=gain, flag=flag_cols, name="in_proj_qkv")
        zg, zg_s = _mm(hp, hs, [(w_in, l, off_gn // LANE)], LANE, mode="plain", out_dtype=_F32, tm=tm, tn=LANE,
                       name="in_proj_gate")
        zug, zug_s = _mm(hp, hs, [(w_ug, l, 0)], n_ug, mode="plain", out_dtype=_F32, tm=tm, tn=tn,
                         name="in_proj_ug")

        def kv_rows(z, lead):
            r = z[:, d_attn:off_gn].reshape(lead + (3, 2, N_KV_HEADS, HEAD_DIM))
            return jnp.swapaxes(r, -3, -2)

        rows_p = kv_rows(zqkv, (batch, seq))
        rows_s = kv_rows(zqkv_s, (db, 1))
        outs["cmp_p"].append(rows_p[:, :, 0])
        outs["sel_p"].append(rows_p[:, :, 1])
        outs["win_p"].append(rows_p[:, seq - min(WINDOW, seq):, 2])
        outs["cmp_s"].append(rows_s[:, :, 0])
        outs["sel_s"].append(rows_s[:, :, 1])

        gates_r = zg[:, :n_gate].reshape(n_p, 3, N_KV_HEADS, GROUP).transpose(2, 0, 1, 3).reshape(
            N_KV_HEADS, n_p, 3 * GROUP)
        wk = w_cmp[l].transpose(0, 2, 1)
        pool = (eye_nb[None, None, :, :, None] * wk[:, :, None, None, :]).reshape(2, N_KV_HEADS, nb, seq)
        o_attn = _nsa_prompt(zqkv, gates_r, pool[0], pool[1], expand, batch, seq)

        q_s = zqkv_s[:, :d_attn].reshape(db, N_HEADS, HEAD_DIM)
        tile_view = lambda c: c.reshape(depth, n_pool, PAGE_SIZE, 2 * N_KV_HEADS, HEAD_DIM)
        w_exp = jnp.broadcast_to(w_cmp[l].transpose(1, 2, 0)[..., None],
                                 (BLOCK, N_KV_HEADS, 2, HEAD_DIM)).reshape(BLOCK, 2 * N_KV_HEADS, HEAD_DIM)
        pooled = _pool_pages(tile_view(cache_kv_cmp), page_table, w_exp, l).reshape(db, past // BLOCK, row_w)
        new_c = rows_s[:, :, 0].reshape(db, 1, row_w)
        o_cmp_s, idx = _cmp_select(q_s, pooled, new_c, w_exp[:1].reshape(1, row_w), past, n_sel)
        idx = idx[..., 0].reshape(-1)
        o_sel_s = _sel_sample(q_s.reshape(db * N_KV_HEADS, GROUP, HEAD_DIM), tile_view(cache_kv_sel),
                              idx, page_table, rows_s[:, :, 1].reshape(db, 1, 2 * N_KV_HEADS, HEAD_DIM),
                              l, past, n_sel)
        win = cache_kv_win[l]
        w_buf = win.shape[1]
        new_w = rows_s[:, :, 2]
        gates_s = zg_s[:, :n_gate].reshape(db, 3, N_HEADS).transpose(0, 2, 1)
        o_attn_s = _win_combine(q_s, win.reshape(db, w_buf, row_w), new_w.reshape(db, 1, row_w),
                                o_cmp_s, o_sel_s.reshape(db, N_HEADS, HEAD_DIM), gates_s, past)
        outs["win_s"].append(jnp.concatenate([win, new_w], axis=1)[:, 1:])

        abar_re, abar_im, bb_re, bb_im = _ssm_discretise(ssm_a_re[l], ssm_a_im[l], ssm_log_dt[l],
                                                         ssm_b_re[l], ssm_b_im[l])
        mats = _ssm_chunk_matrices(abar_re, abar_im, bb_re, bb_im, ssm_c_re[l], ssm_c_im[l])
        y_p, h_p = _ssm_prompt(zug, mats, ssm_d[l], batch, seq, d_ssm)
        y_s, h_s = _ssm_sample(zug_s[:, :d_ssm], state_ssm[l], abar_re, abar_im, bb_re, bb_im,
                               ssm_c_re[l], ssm_c_im[l], ssm_d[l])
        outs["ssm_p"].append(h_p)
        outs["ssm_s"].append(h_s)

        mg, mg_s = _merge(o_attn, o_attn_s.reshape(db, d_attn).astype(_BF), y_p, y_s.astype(_BF),
                          w_attn_out, w_glu, l, zug, zug_s, d_model, d_ssm, tm=_tile(n_p, 512), tn=tn)
        xp, xs = _mm(mg, mg_s, [(w_out, l, 0)], d_model, mode="resid", out_dtype=_F32, tm=tm, tn=tn,
                     res=xp, res_s=xs, name="out_proj")

        if l % 2 == 0:
            j = l // 2
            hp = _rmsnorm(xp, norm_ffn[l])
            hs = _rmsnorm(xs, norm_ffn[l])
            f = ffn_w1.shape[2]
            a, a_s = _mm(hp, hs, [(ffn_w1, j, 0), (ffn_w3, j, 0)], f, mode="swiglu", out_dtype=_BF,
                         tm=_tile(n_p, 512), tn=_tile(f, 512, LANE), name="ffn_up")
            xp, xs = _ffn_down(a, a_s, ffn_w2, j, xp, xs, tm=_tile(n_p, 1024), tn=_tile(d_model, 1024, LANE),
                               tk=_tile(f, 1024, LANE))
        else:
            j = l // 2
            xp, xs = _moe_layer(xp, xs, norm_ffn[l], moe_router[j], moe_w1, moe_w3, moe_w2, j)

    st = lambda k: jnp.stack(outs[k])
    return (xp.reshape(batch, seq, d_model), xs.reshape(db, 1, d_model), st("cmp_p"), st("cmp_s"), st("sel_p"),
            st("sel_s"), st("win_p"), st("win_s"), st("ssm_p"), st("ssm_s"))
```
